```python
import numpy as np
import jax, jax.numpy as jnp
from jax import lax

D_MODEL = 2048
BATCH = 8
SEQ = 2048
DEPTH = 1

EPS = 1e-6
NEG_INF = -1e30
FORCE = 1e9
ROPE_THETA = 500000.0
Q_BLOCK = 128

MLA_HEADS = 16
MLA_NOPE = 128
MLA_ROPE = 64
MLA_QK = MLA_NOPE + MLA_ROPE
MLA_V = 128
MLA_Q_RANK = 512
MLA_KV_RANK = 512

NSA_HEADS = 16
NSA_GROUPS = 4
NSA_HPG = NSA_HEADS // NSA_GROUPS
NSA_DK = 192
NSA_DV = 128
NSA_ROPE = NSA_DK // 4
CMP_LEN = 32
CMP_STRIDE = 16
CMP_HIDDEN_K = 4 * NSA_DK
CMP_HIDDEN_V = 4 * NSA_DV
SLC_LEN = 64
SLC_TOPK = 16
WINDOW = 512
SLC_Q_CHUNK = 32

N_BRANCHES = 2
D_FF = -(-8 * D_MODEL // (3 * 256)) * 256

IN_SIZES = (
    MLA_Q_RANK, MLA_KV_RANK, MLA_ROPE,
    NSA_HEADS * NSA_DK,
    NSA_GROUPS * NSA_DK, NSA_GROUPS * NSA_DV,
    NSA_GROUPS * NSA_DK, NSA_GROUPS * NSA_DV,
    NSA_GROUPS * NSA_DK, NSA_GROUPS * NSA_DV,
    NSA_HEADS * 3,
    N_BRANCHES * D_MODEL,
)
D_IN = sum(IN_SIZES)

kernel_name = "hybrid_mla_nsa_gated_swiglu"


def rmsnorm(x, g):
    xf = x.astype(jnp.float32)
    y = xf * lax.rsqrt(jnp.mean(xf * xf, axis=-1, keepdims=True) + EPS)
    return (y * g.astype(jnp.float32)).astype(x.dtype)


def rope_tables(pos, rot_dim):
    inv = ROPE_THETA ** (-jnp.arange(0, rot_dim, 2, dtype=jnp.float32) / rot_dim)
    ang = pos.astype(jnp.float32)[:, None] * inv[None, :]
    return jnp.cos(ang), jnp.sin(ang)


def apply_rope(x, cos, sin, start, rot_dim):
    half = rot_dim // 2
    xr = x[..., start:start + rot_dim].astype(jnp.float32)
    x1, x2 = xr[..., :half], xr[..., half:]
    rot = jnp.concatenate([x1 * cos - x2 * sin, x2 * cos + x1 * sin], axis=-1).astype(x.dtype)
    return jnp.concatenate([x[..., :start], rot, x[..., start + rot_dim:]], axis=-1)


def masked_softmax(s, mask):
    s = jnp.where(mask, s, NEG_INF)
    m = jnp.max(s, axis=-1, keepdims=True)
    p = jnp.where(mask, jnp.exp(s - m), 0.0)
    return p / jnp.maximum(jnp.sum(p, axis=-1, keepdims=True), 1e-30)


def dense_causal_attention(q, k, v, scale):
    T = q.shape[2]
    outs = []
    for c in range(T // Q_BLOCK):
        s0, e = c * Q_BLOCK, (c + 1) * Q_BLOCK
        s = jnp.einsum('bhqd,bhkd->bhqk', q[:, :, s0:e], k[:, :, :e]).astype(jnp.float32) * scale
        mask = jnp.arange(e)[None, :] <= jnp.arange(s0, e)[:, None]
        p = masked_softmax(s, mask)
        outs.append(jnp.einsum('bhqk,bhkd->bhqd', p.astype(v.dtype), v[:, :, :e]))
    return jnp.concatenate(outs, axis=2)


def sliding_window_attention(q, k, v, scale):
    B, G, R, T, _ = q.shape
    span = Q_BLOCK + WINDOW
    k_pad = jnp.pad(k, ((0, 0), (0, 0), (WINDOW, 0), (0, 0)))
    v_pad = jnp.pad(v, ((0, 0), (0, 0), (WINDOW, 0), (0, 0)))

    def block(c):
        s0 = c * Q_BLOCK
        qb = lax.dynamic_slice_in_dim(q, s0, Q_BLOCK, axis=3)
        kb = lax.dynamic_slice_in_dim(k_pad, s0, span, axis=2)
        vb = lax.dynamic_slice_in_dim(v_pad, s0, span, axis=2)
        t = s0 + jnp.arange(Q_BLOCK)
        pk = s0 - WINDOW + jnp.arange(span)
        d = t[:, None] - pk[None, :]
        mask = (pk[None, :] >= 0) & (d >= 0) & (d < WINDOW)
        s = jnp.einsum('bgrqd,bgkd->bgrqk', qb, kb).astype(jnp.float32) * scale
        p = masked_softmax(s, mask)
        return jnp.einsum('bgrqk,bgkd->bgrqd', p.astype(vb.dtype), vb)

    o = lax.map(block, jnp.arange(T // Q_BLOCK))
    o = jnp.moveaxis(o, 0, 3)
    return o.reshape(B, G, R, T, o.shape[-1])


def compress_blocks(x, pe, w1, b1, w2, b2):
    T = x.shape[2]
    ncmp = (T - CMP_LEN) // CMP_STRIDE + 1
    idx = np.arange(ncmp)[:, None] * CMP_STRIDE + np.arange(CMP_LEN)[None, :]
    blocks = x[:, :, idx] + pe
    flat = blocks.reshape(blocks.shape[:3] + (-1,))
    return jax.nn.gelu(flat @ w1 + b1) @ w2 + b2


def select_blocks(p_cmp, T):
    nb = T // SLC_LEN
    ncmp = p_cmp.shape[-1]
    cs = np.arange(ncmp)[:, None] * CMP_STRIDE
    ss = np.arange(nb)[None, :] * SLC_LEN
    overlap = np.clip(np.minimum(cs + CMP_LEN, ss + SLC_LEN) - np.maximum(cs, ss), 0, None) / CMP_LEN
    imp = jnp.einsum('bgrtc,cj->bgtj', p_cmp, jnp.asarray(overlap, jnp.float32))
    tb = jnp.arange(T)[:, None] // SLC_LEN
    jb = jnp.arange(nb)[None, :]
    forced = (jb == 0) | (jb == tb) | (jb == tb - 1)
    valid = jb <= tb
    score = jnp.where(forced, FORCE, jnp.where(valid, imp, -FORCE))
    _, idx = lax.top_k(score, min(SLC_TOPK, nb))
    return idx


def selected_block_attention(q, k, v, sel_idx, scale):
    B, G, R, T, _ = q.shape
    nb = T // SLC_LEN
    n = sel_idx.shape[-1]
    kb_all = k.reshape(B, G, nb, SLC_LEN, k.shape[-1])
    vb_all = v.reshape(B, G, nb, SLC_LEN, v.shape[-1])
    bi = jnp.arange(B)[:, None, None, None]
    gi = jnp.arange(G)[None, :, None, None]

    def chunk(c):
        s0 = c * SLC_Q_CHUNK
        qc = lax.dynamic_slice_in_dim(q, s0, SLC_Q_CHUNK, axis=3)
        idx = lax.dynamic_slice_in_dim(sel_idx, s0, SLC_Q_CHUNK, axis=2)
        kg = kb_all[bi, gi, idx]
        vg = vb_all[bi, gi, idx]
        s = jnp.einsum('bgrqd,bgqnld->bgrqnl', qc, kg).astype(jnp.float32) * scale
        t = s0 + jnp.arange(SLC_Q_CHUNK)
        pk = idx[..., None] * SLC_LEN + jnp.arange(SLC_LEN)
        mask = (pk <= t[None, None, :, None, None])[:, :, None].reshape(B, G, 1, SLC_Q_CHUNK, n * SLC_LEN)
        p = masked_softmax(s.reshape(B, G, R, SLC_Q_CHUNK, n * SLC_LEN), mask)
        p = p.reshape(B, G, R, SLC_Q_CHUNK, n, SLC_LEN)
        return jnp.einsum('bgrqnl,bgqnld->bgrqd', p.astype(vg.dtype), vg)

    o = lax.map(chunk, jnp.arange(T // SLC_Q_CHUNK))
    o = jnp.moveaxis(o, 0, 3)
    return o.reshape(B, G, R, T, o.shape[-1])


def hybrid_layer(x, attn_norm_g, w_in, mla_q_lat_g, mla_kv_lat_g, mla_w_uq, mla_w_uk, mla_w_uv,
                 mla_q_norm_g, mla_k_norm_g, nsa_q_norm_g, nsa_k_norm_g, cmp_pe_k, cmp_pe_v,
                 cmp_k_w1, cmp_k_b1, cmp_k_w2, cmp_k_b2, cmp_v_w1, cmp_v_b1, cmp_v_w2, cmp_v_b2,
                 w_proj_mla, w_proj_nsa, w_out, ffn_norm_g, w_ffn_gate, w_ffn_up, w_ffn_down):
    B, T, _ = x.shape
    pos = jnp.arange(T)
    xn = rmsnorm(x, attn_norm_g)
    proj = xn @ w_in
    offsets = [int(o) for o in np.cumsum(IN_SIZES)[:-1]]
    (cq, ckv, kr, q_nsa, kc, vc, ks, vs, kw, vw, g_nsa, g_merge) = jnp.split(proj, offsets, axis=-1)

    cq = rmsnorm(cq, mla_q_lat_g)
    q = (cq @ mla_w_uq).reshape(B, T, MLA_HEADS, MLA_QK).transpose(0, 2, 1, 3)
    ckv = rmsnorm(ckv, mla_kv_lat_g)
    k_nope = (ckv @ mla_w_uk).reshape(B, T, MLA_HEADS, MLA_NOPE).transpose(0, 2, 1, 3)
    v = (ckv @ mla_w_uv).reshape(B, T, MLA_HEADS, MLA_V).transpose(0, 2, 1, 3)
    k_rope = jnp.broadcast_to(kr[:, None], (B, MLA_HEADS, T, MLA_ROPE))
    k = jnp.concatenate([k_nope, k_rope], axis=-1)
    cos_m, sin_m = rope_tables(pos, MLA_ROPE)
    q = apply_rope(rmsnorm(q, mla_q_norm_g), cos_m, sin_m, MLA_NOPE, MLA_ROPE)
    k = apply_rope(rmsnorm(k, mla_k_norm_g), cos_m, sin_m, MLA_NOPE, MLA_ROPE)
    o_mla = dense_causal_attention(q, k, v, MLA_QK ** -0.5)
    y_mla = o_mla.transpose(0, 2, 1, 3).reshape(B, T, MLA_HEADS * MLA_V) @ w_proj_mla

    scale = NSA_DK ** -0.5
    cos_n, sin_n = rope_tables(pos, NSA_ROPE)
    qn = q_nsa.reshape(B, T, NSA_HEADS, NSA_DK).transpose(0, 2, 1, 3)
    qn = apply_rope(rmsnorm(qn, nsa_q_norm_g), cos_n, sin_n, 0, NSA_ROPE)
    qn = qn.reshape(B, NSA_GROUPS, NSA_HPG, T, NSA_DK)

    def kv_heads(t, d):
        return t.reshape(B, T, NSA_GROUPS, d).transpose(0, 2, 1, 3)

    k_c = compress_blocks(kv_heads(kc, NSA_DK), cmp_pe_k, cmp_k_w1, cmp_k_b1, cmp_k_w2, cmp_k_b2)
    v_c = compress_blocks(kv_heads(vc, NSA_DV), cmp_pe_v, cmp_v_w1, cmp_v_b1, cmp_v_w2, cmp_v_b2)
    ncmp = k_c.shape[2]
    cmp_end = jnp.arange(ncmp) * CMP_STRIDE + CMP_LEN - 1
    cos_c, sin_c = rope_tables(cmp_end, NSA_ROPE)
    k_c = apply_rope(rmsnorm(k_c, nsa_k_norm_g[0]), cos_c, sin_c, 0, NSA_ROPE)
    s_c = jnp.einsum('bgrtd,bgcd->bgrtc', qn, k_c).astype(jnp.float32) * scale
    p_c = masked_softmax(s_c, cmp_end[None, :] <= pos[:, None])
    o_c = jnp.einsum('bgrtc,bgcd->bgrtd', p_c.astype(v_c.dtype), v_c)

    sel_idx = select_blocks(p_c, T)
    k_s = apply_rope(rmsnorm(kv_heads(ks, NSA_DK), nsa_k_norm_g[1]), cos_n, sin_n, 0, NSA_ROPE)
    o_s = selected_block_attention(qn, k_s, kv_heads(vs, NSA_DV), sel_idx, scale)

    k_w = apply_rope(rmsnorm(kv_heads(kw, NSA_DK), nsa_k_norm_g[2]), cos_n, sin_n, 0, NSA_ROPE)
    o_w = sliding_window_attention(qn, k_w, kv_heads(vw, NSA_DV), scale)

    g = jax.nn.sigmoid(g_nsa.astype(jnp.float32)).astype(x.dtype)
    g = g.reshape(B, T, NSA_GROUPS, NSA_HPG, 3).transpose(0, 2, 3, 1, 4)
    o_nsa = g[..., 0:1] * o_c + g[..., 1:2] * o_s + g[..., 2:3] * o_w
    y_nsa = o_nsa.transpose(0, 3, 1, 2, 4).reshape(B, T, NSA_HEADS * NSA_DV) @ w_proj_nsa

    gm = jax.nn.sigmoid(g_merge.astype(jnp.float32)).astype(x.dtype).reshape(B, T, N_BRANCHES, D_MODEL)
    h = x + (gm[:, :, 0] * y_mla + gm[:, :, 1] * y_nsa) @ w_out

    hn = rmsnorm(h, ffn_norm_g)
    ff = (jax.nn.silu(hn @ w_ffn_gate) * (hn @ w_ffn_up)) @ w_ffn_down
    return h + ff


def setup_inputs(seed: int = 0) -> dict:
    key = jax.random.key(seed)
    keys = iter(jax.random.split(key, 32))

    def w(shape, fan_in):
        return jax.random.normal(next(keys), (DEPTH,) + shape, jnp.float32) * fan_in ** -0.5

    def gain(shape):
        return 1.0 + 0.01 * jax.random.normal(next(keys), (DEPTH,) + shape, jnp.float32)

    def small(shape, s):
        return s * jax.random.normal(next(keys), (DEPTH,) + shape, jnp.float32)

    x = jax.random.normal(next(keys), (BATCH, SEQ, D_MODEL), jnp.float32)
    return {
        "x": x,
        "attn_norm_g": gain((D_MODEL,)),
        "w_in": w((D_MODEL, D_IN), D_MODEL),
        "mla_q_lat_g": gain((MLA_Q_RANK,)),
        "mla_kv_lat_g": gain((MLA_KV_RANK,)),
        "mla_w_uq": w((MLA_Q_RANK, MLA_HEADS * MLA_QK), MLA_Q_RANK),
        "mla_w_uk": w((MLA_KV_RANK, MLA_HEADS * MLA_NOPE), MLA_KV_RANK),
        "mla_w_uv": w((MLA_KV_RANK, MLA_HEADS * MLA_V), MLA_KV_RANK),
        "mla_q_norm_g": gain((MLA_QK,)),
        "mla_k_norm_g": gain((MLA_QK,)),
        "nsa_q_norm_g": gain((NSA_DK,)),
        "nsa_k_norm_g": gain((3, NSA_DK)),
        "cmp_pe_k": small((CMP_LEN, NSA_DK), 0.1),
        "cmp_pe_v": small((CMP_LEN, NSA_DV), 0.1),
        "cmp_k_w1": w((CMP_LEN * NSA_DK, CMP_HIDDEN_K), CMP_LEN * NSA_DK),
        "cmp_k_b1": small((CMP_HIDDEN_K,), 0.01),
        "cmp_k_w2": w((CMP_HIDDEN_K, NSA_DK), CMP_HIDDEN_K),
        "cmp_k_b2": small((NSA_DK,), 0.01),
        "cmp_v_w1": w((CMP_LEN * NSA_DV, CMP_HIDDEN_V), CMP_LEN * NSA_DV),
        "cmp_v_b1": small((CMP_HIDDEN_V,), 0.01),
        "cmp_v_w2": w((CMP_HIDDEN_V, NSA_DV), CMP_HIDDEN_V),
        "cmp_v_b2": small((NSA_DV,), 0.01),
        "w_proj_mla": w((MLA_HEADS * MLA_V, D_MODEL), MLA_HEADS * MLA_V),
        "w_proj_nsa": w((NSA_HEADS * NSA_DV, D_MODEL), NSA_HEADS * NSA_DV),
        "w_out": w((D_MODEL, D_MODEL), D_MODEL),
        "ffn_norm_g": gain((D_MODEL,)),
        "w_ffn_gate": w((D_MODEL, D_FF), D_MODEL),
        "w_ffn_up": w((D_MODEL, D_FF), D_MODEL),
        "w_ffn_down": w((D_FF, D_MODEL), D_FF),
    }


def reference(x, attn_norm_g, w_in, mla_q_lat_g, mla_kv_lat_g, mla_w_uq, mla_w_uk, mla_w_uv,
              mla_q_norm_g, mla_k_norm_g, nsa_q_norm_g, nsa_k_norm_g, cmp_pe_k, cmp_pe_v,
              cmp_k_w1, cmp_k_b1, cmp_k_w2, cmp_k_b2, cmp_v_w1, cmp_v_b1, cmp_v_w2, cmp_v_b2,
              w_proj_mla, w_proj_nsa, w_out, ffn_norm_g, w_ffn_gate, w_ffn_up, w_ffn_down):
    h = x
    for l in range(DEPTH):
        h = hybrid_layer(h, attn_norm_g[l], w_in[l], mla_q_lat_g[l], mla_kv_lat_g[l], mla_w_uq[l],
                         mla_w_uk[l], mla_w_uv[l], mla_q_norm_g[l], mla_k_norm_g[l], nsa_q_norm_g[l],
                         nsa_k_norm_g[l], cmp_pe_k[l], cmp_pe_v[l], cmp_k_w1[l], cmp_k_b1[l], cmp_k_w2[l],
                         cmp_k_b2[l], cmp_v_w1[l], cmp_v_b1[l], cmp_v_w2[l], cmp_v_b2[l], w_proj_mla[l],
                         w_proj_nsa[l], w_out[l], ffn_norm_g[l], w_ffn_gate[l], w_ffn_up[l], w_ffn_down[l])
    return h
```

```python
import functools

import jax
import jax.numpy as jnp
from jax import lax
from jax.experimental import pallas as pl
from jax.experimental.pallas import tpu as pltpu

F32, BF16, I32 = jnp.float32, jnp.bfloat16, jnp.int32

EPS = 1e-6
NEG_INF = -1e30
FORCE = 1e9
ROPE_THETA = 500000.0

D_MODEL = 2048
MLA_HEADS, MLA_NOPE, MLA_ROPE, MLA_V = 16, 128, 64, 128
MLA_QK = MLA_NOPE + MLA_ROPE
MLA_RANK = 512
NSA_HEADS, NSA_GROUPS, NSA_DK, NSA_DV = 16, 4, 192, 128
NSA_HPG = NSA_HEADS // NSA_GROUPS
NSA_ROPE = NSA_DK // 4
CMP_LEN, CMP_STRIDE = 32, 16
SLC_LEN, SLC_TOPK, WINDOW = 64, 16, 512
D_FF = 5632

LANES = 128
HEAD_PAD = 256
SEL_LANE0 = 64
VMEM_LIMIT = 48 * 1024 * 1024

C_QN, C_GM, C_KS, C_KW, C_VS, C_VW = 0, 4096, 8192, 9216, 10240, 10752
C_CQ, C_CKV, C_KR, C_GATE, C_KC, C_VC, C_END = 11264, 11776, 12288, 12544, 13056, 13824, 14336

_NT = (((1,), (1,)), ((), ()))


def _params(sem):
    return pltpu.CompilerParams(dimension_semantics=sem, vmem_limit_bytes=VMEM_LIMIT)


def _norm_matmul_kernel(x_ref, g_ref, w_ref, o_ref, xn_ref):
    @pl.when(pl.program_id(1) == 0)
    def _():
        x = x_ref[...]
        ms = jnp.mean(x * x, axis=-1, keepdims=True)
        xn_ref[...] = (x * lax.rsqrt(ms + EPS) * g_ref[...]).astype(BF16)

    o_ref[...] = jnp.dot(xn_ref[...], w_ref[...], preferred_element_type=F32).astype(o_ref.dtype)


def _norm_matmul(x, xcol, k, g, w, tm, tn):
    n, m = x.shape[0], w.shape[1]
    return pl.pallas_call(
        _norm_matmul_kernel,
        grid=(n // tm, m // tn),
        in_specs=[pl.BlockSpec((tm, k), lambda i, j: (i, xcol)),
                  pl.BlockSpec((1, k), lambda i, j: (0, 0)),
                  pl.BlockSpec((k, tn), lambda i, j: (0, j))],
        out_specs=pl.BlockSpec((tm, tn), lambda i, j: (i, j)),
        out_shape=jax.ShapeDtypeStruct((n, m), F32),
        scratch_shapes=[pltpu.VMEM((tm, k), BF16)],
        compiler_params=_params(("parallel", "arbitrary")),
    )(x, g, w)


def _merge_kernel(om_ref, on_ref, wm_ref, wn_ref, g0_ref, g1_ref, o_ref):
    ym = jnp.dot(om_ref[...], wm_ref[...], preferred_element_type=F32)
    yn = jnp.dot(on_ref[...], wn_ref[...], preferred_element_type=F32)
    o_ref[...] = (jax.nn.sigmoid(g0_ref[...]) * ym + jax.nn.sigmoid(g1_ref[...]) * yn).astype(BF16)


def _merge(o_mla, o_nsa, w_pm, w_pn, proj, tm=512, tn=512):
    n = o_mla.shape[0]
    g0, g1 = C_GM // tn, (C_GM + D_MODEL) // tn
    return pl.pallas_call(
        _merge_kernel,
        grid=(n // tm, D_MODEL // tn),
        in_specs=[pl.BlockSpec((tm, D_MODEL), lambda i, j: (i, 0)),
                  pl.BlockSpec((tm, D_MODEL), lambda i, j: (i, 0)),
                  pl.BlockSpec((D_MODEL, tn), lambda i, j: (0, j)),
                  pl.BlockSpec((D_MODEL, tn), lambda i, j: (0, j)),
                  pl.BlockSpec((tm, tn), lambda i, j: (i, g0 + j)),
                  pl.BlockSpec((tm, tn), lambda i, j: (i, g1 + j))],
        out_specs=pl.BlockSpec((tm, tn), lambda i, j: (i, j)),
        out_shape=jax.ShapeDtypeStruct((n, D_MODEL), BF16),
        compiler_params=_params(("parallel", "arbitrary")),
    )(o_mla, o_nsa, w_pm, w_pn, proj, proj)


def _out_proj_kernel(u_ref, w_ref, x_ref, g_ref, h_ref, hn_ref):
    h = x_ref[...] + jnp.dot(u_ref[...], w_ref[...], preferred_element_type=F32)
    h_ref[...] = h
    ms = jnp.mean(h * h, axis=-1, keepdims=True)
    hn_ref[...] = (h * lax.rsqrt(ms + EPS) * g_ref[...]).astype(BF16)


def _out_proj(u, w_out, x, g, tm=256):
    n = u.shape[0]
    row = pl.BlockSpec((tm, D_MODEL), lambda i: (i, 0))
    return pl.pallas_call(
        _out_proj_kernel,
        grid=(n // tm,),
        in_specs=[row, pl.BlockSpec((D_MODEL, D_MODEL), lambda i: (0, 0)), row,
                  pl.BlockSpec((1, D_MODEL), lambda i: (0, 0))],
        out_specs=[row, row],
        out_shape=[jax.ShapeDtypeStruct((n, D_MODEL), F32), jax.ShapeDtypeStruct((n, D_MODEL), BF16)],
        compiler_params=_params(("parallel",)),
    )(u, w_out, x, g)


def _ffn_up_kernel(hn_ref, wg_ref, wu_ref, o_ref):
    hn = hn_ref[...]
    a = jnp.dot(hn, wg_ref[...], preferred_element_type=F32)
    b = jnp.dot(hn, wu_ref[...], preferred_element_type=F32)
    o_ref[...] = (jax.nn.silu(a) * b).astype(BF16)


def _ffn_up(hn, w_gate, w_up, tm=512, tn=512):
    n = hn.shape[0]
    return pl.pallas_call(
        _ffn_up_kernel,
        grid=(n // tm, D_FF // tn),
        in_specs=[pl.BlockSpec((tm, D_MODEL), lambda i, j: (i, 0)),
                  pl.BlockSpec((D_MODEL, tn), lambda i, j: (0, j)),
                  pl.BlockSpec((D_MODEL, tn), lambda i, j: (0, j))],
        out_specs=pl.BlockSpec((tm, tn), lambda i, j: (i, j)),
        out_shape=jax.ShapeDtypeStruct((n, D_FF), BF16),
        compiler_params=_params(("parallel", "arbitrary")),
    )(hn, w_gate, w_up)


def _ffn_down_kernel(a_ref, w_ref, h_ref, o_ref):
    o_ref[...] = h_ref[...] + jnp.dot(a_ref[...], w_ref[...], preferred_element_type=F32)


def _ffn_down(act, w_down, h, tm=512, tn=512):
    n = act.shape[0]
    return pl.pallas_call(
        _ffn_down_kernel,
        grid=(n // tm, D_MODEL // tn),
        in_specs=[pl.BlockSpec((tm, D_FF), lambda i, j: (i, 0)),
                  pl.BlockSpec((D_FF, tn), lambda i, j: (0, j)),
                  pl.BlockSpec((tm, tn), lambda i, j: (i, j))],
        out_specs=pl.BlockSpec((tm, tn), lambda i, j: (i, j)),
        out_shape=jax.ShapeDtypeStruct((n, D_MODEL), F32),
        compiler_params=_params(("parallel", "arbitrary")),
    )(act, w_down, h)


def _rms_inv(xa, xb, d):
    ss = jnp.sum(xa * xa, axis=-1, keepdims=True) + jnp.sum(xb * xb, axis=-1, keepdims=True)
    return lax.rsqrt(ss * (1.0 / d) + EPS)


def _rope(x, c, s1, s2, half):
    return x * c + pltpu.roll(x, LANES - half, 1) * s1 + pltpu.roll(x, half, 1) * s2


def _flash(q, k_at, v_at, t_rows, n_full, tk, dv):
    rows = q.shape[0]

    def step(j, carry, masked):
        m, l, acc = carry
        off = pl.multiple_of(j * tk, tk)
        s = lax.dot_general(q, k_at(off), _NT, preferred_element_type=F32)
        if masked:
            pk = off + lax.broadcasted_iota(I32, (1, tk), 1)
            s = jnp.where(pk <= t_rows, s, NEG_INF)
        m_new = jnp.maximum(m, jnp.max(s, axis=-1, keepdims=True))
        alpha = jnp.exp(m - m_new)
        p = jnp.exp(s - m_new)
        l = alpha * l + jnp.sum(p, axis=-1, keepdims=True)
        acc = alpha * acc + jnp.dot(p.astype(BF16), v_at(off), preferred_element_type=F32)
        return m_new, l, acc

    init = (jnp.full((rows, 1), NEG_INF, F32), jnp.zeros((rows, 1), F32), jnp.zeros((rows, dv), F32))
    carry = lax.fori_loop(0, n_full, lambda j, c: step(j, c, False), init)
    _, l, acc = step(n_full, carry, True)
    return acc / l


def _mla_kernel(q_ref, kn_ref, kr_ref, v_ref, gq_ref, gk_ref, cq_ref, s1q_ref, s2q_ref,
                ck_ref, s1k_ref, s2k_ref, o_ref, k_s, v_s, *, tq, tk, scale):
    qi = pl.program_id(2)
    half = MLA_ROPE // 2

    @pl.when(qi == 0)
    def _():
        kr = kr_ref[:, :LANES]
        ss_r = jnp.sum(kr * kr, axis=-1, keepdims=True)
        for h in range(2):
            kn = kn_ref[:, h * LANES:(h + 1) * LANES]
            r = lax.rsqrt((jnp.sum(kn * kn, axis=-1, keepdims=True) + ss_r) * (1.0 / MLA_QK) + EPS)
            k_s[h, :, :LANES] = (kn * r * gk_ref[:, :LANES]).astype(BF16)
            k_s[h, :, LANES:] = _rope(kr * r * gk_ref[:, LANES:], ck_ref[...], s1k_ref[...], s2k_ref[...],
                                      half).astype(BF16)
            v_s[h] = v_ref[:, h * LANES:(h + 1) * LANES].astype(BF16)

    t0 = qi * tq
    t_rows = t0 + lax.broadcasted_iota(I32, (tq, 1), 0)
    for h in range(2):
        qa = q_ref[:, h * HEAD_PAD:h * HEAD_PAD + LANES]
        qb = q_ref[:, h * HEAD_PAD + LANES:(h + 1) * HEAD_PAD]
        r = _rms_inv(qa, qb, MLA_QK)
        qa = qa * r * gq_ref[:, :LANES] * scale
        qb = _rope(qb * r * gq_ref[:, LANES:], cq_ref[...], s1q_ref[...], s2q_ref[...], half) * scale
        q = jnp.concatenate([qa.astype(BF16), qb.astype(BF16)], axis=1)
        o = _flash(q, lambda off, h=h: k_s[h, pl.ds(off, tk), :], lambda off, h=h: v_s[h, pl.ds(off, tk), :],
                   t_rows, t0 // tk, tk, MLA_V)
        o_ref[:, h * LANES:(h + 1) * LANES] = o.astype(BF16)


def _mla_attention(q_up, kv_up, proj, gq, gk, tabs, B, T, tq=256, tk=512):
    nq = T // tq
    c, s1, s2 = tabs
    qtab = pl.BlockSpec((tq, LANES), lambda b, h, i: (i, 0))
    ktab = pl.BlockSpec((T, LANES), lambda b, h, i: (0, 0))
    gspec = pl.BlockSpec((1, HEAD_PAD), lambda b, h, i: (0, 0))
    return pl.pallas_call(
        functools.partial(_mla_kernel, tq=tq, tk=tk, scale=MLA_QK ** -0.5),
        grid=(B, MLA_HEADS // 2, nq),
        in_specs=[pl.BlockSpec((tq, 2 * HEAD_PAD), lambda b, h, i: (b * nq + i, h)),
                  pl.BlockSpec((T, 2 * LANES), lambda b, h, i: (b, h)),
                  pl.BlockSpec((T, HEAD_PAD), lambda b, h, i: (b, C_KR // HEAD_PAD)),
                  pl.BlockSpec((T, 2 * LANES), lambda b, h, i: (b, MLA_HEADS // 2 + h)),
                  gspec, gspec, qtab, qtab, qtab, ktab, ktab, ktab],
        out_specs=pl.BlockSpec((tq, 2 * LANES), lambda b, h, i: (b * nq + i, h)),
        out_shape=jax.ShapeDtypeStruct((B * T, MLA_HEADS * MLA_V), BF16),
        scratch_shapes=[pltpu.VMEM((2, T, HEAD_PAD), BF16), pltpu.VMEM((2, T, LANES), BF16)],
        compiler_params=_params(("parallel", "parallel", "arbitrary")),
    )(q_up, kv_up, proj, kv_up, gq, gk, c, s1, s2, c, s1, s2)


def _compress_kernel(x_ref, pea_ref, peb_ref, w1a_ref, w1b_ref, b1_ref, w2_ref, b2_ref,
                     g_ref, c_ref, s1_ref, s2_ref, o_ref, *, is_key, gb, ncp):
    rows = gb * ncp
    x = x_ref[0].reshape(rows, x_ref.shape[-1])
    pa = jnp.dot((x + pea_ref[...]).astype(BF16), w1a_ref[...], preferred_element_type=F32)
    pb = jnp.dot((x + peb_ref[...]).astype(BF16), w1b_ref[...], preferred_element_type=F32)
    h = pa + pltpu.roll(pb, rows - 1, 0) + b1_ref[...]
    y = jnp.dot(jax.nn.gelu(h).astype(BF16), w2_ref[...], preferred_element_type=F32) + b2_ref[...]
    if is_key:
        ya, yb = y[:, :LANES], y[:, LANES:]
        r = _rms_inv(ya, yb, NSA_DK)
        c = jnp.concatenate([c_ref[...]] * gb, axis=0)
        s1 = jnp.concatenate([s1_ref[...]] * gb, axis=0)
        s2 = jnp.concatenate([s2_ref[...]] * gb, axis=0)
        ya = _rope(ya * r * g_ref[:, :LANES], c, s1, s2, NSA_ROPE // 2)
        yb = yb * r * g_ref[:, LANES:]
        y = jnp.concatenate([ya, yb], axis=1)
    o_ref[0] = y.astype(BF16).reshape(gb, ncp, y.shape[-1])


def _compress(xc, pe, w1, b1, w2, b2, g, tabs, is_key, gb=2):
    B, G, ncp, kd = xc.shape
    d = kd // CMP_STRIDE
    dout = w2.shape[1]
    hid = w1.shape[1]
    pea = pe[:CMP_STRIDE].reshape(1, kd)
    peb = pe[CMP_STRIDE:].reshape(1, kd)
    w1a, w1b = w1[:kd].astype(BF16), w1[kd:].astype(BF16)
    c, s1, s2 = tabs
    const = lambda shape: pl.BlockSpec(shape, lambda b, gi: (0,) * len(shape))
    return pl.pallas_call(
        functools.partial(_compress_kernel, is_key=is_key, gb=gb, ncp=ncp),
        grid=(B, G // gb),
        in_specs=[pl.BlockSpec((1, gb, ncp, kd), lambda b, gi: (b, gi, 0, 0)),
                  const((1, kd)), const((1, kd)), const((kd, hid)), const((kd, hid)), const((1, hid)),
                  const((hid, dout)), const((1, dout)), const((1, dout)),
                  const((ncp, LANES)), const((ncp, LANES)), const((ncp, LANES))],
        out_specs=pl.BlockSpec((1, gb, ncp, dout), lambda b, gi: (b, gi, 0, 0)),
        out_shape=jax.ShapeDtypeStruct((B, G, ncp, dout), BF16),
        compiler_params=_params(("parallel", "parallel")),
    )(xc, pea, peb, w1a, w1b, b1.reshape(1, hid), w2.astype(BF16), b2.reshape(1, dout), g, c, s1, s2)


def _nsa_kernel(q_ref, ks_ref, kw_ref, vs_ref, vw_ref, gate_ref, kc_ref, vc_ref, gq_ref, gks_ref, gkw_ref,
                cq_ref, s1q_ref, s2q_ref, ck_ref, s1k_ref, s2k_ref, ov_ref,
                o_ref, ks_s, kw_s, vs_s, vw_s, q_s, *, tq, tk, T, scale):
    qi = pl.program_id(2)
    R = NSA_HPG
    half = NSA_ROPE // 2
    nb = T // SLC_LEN
    topk = min(SLC_TOPK, nb)
    ncp = kc_ref.shape[2]

    @pl.when(qi == 0)
    def _():
        def prep_k(k_ref, g_ref):
            ka, kb = k_ref[:, :LANES], k_ref[:, LANES:]
            r = _rms_inv(ka, kb, NSA_DK)
            ka = _rope(ka * r * g_ref[:, :LANES], ck_ref[...], s1k_ref[...], s2k_ref[...], half)
            return ka, kb * r * g_ref[:, LANES:]

        ka, kb = prep_k(ks_ref, gks_ref)
        blk = lax.broadcasted_iota(I32, (T, LANES), 0) // SLC_LEN
        lane = lax.broadcasted_iota(I32, (T, LANES), 1)
        ks_s[:, :LANES] = ka.astype(BF16)
        ks_s[:, LANES:] = (kb + (lane - SEL_LANE0 == blk).astype(F32)).astype(BF16)
        vs_s[...] = vs_ref[...].astype(BF16)
        ka, kb = prep_k(kw_ref, gkw_ref)
        kw_s[:WINDOW, :] = jnp.zeros((WINDOW, HEAD_PAD), BF16)
        kw_s[WINDOW:, :LANES] = ka.astype(BF16)
        kw_s[WINDOW:, LANES:] = kb.astype(BF16)
        vw_s[:WINDOW, :] = jnp.zeros((WINDOW, NSA_DV), BF16)
        vw_s[WINDOW:, :] = vw_ref[...].astype(BF16)

    t0 = pl.multiple_of(qi * tq, tq)
    for r in range(R):
        qa = q_ref[:, r * HEAD_PAD:r * HEAD_PAD + LANES]
        qb = q_ref[:, r * HEAD_PAD + LANES:(r + 1) * HEAD_PAD]
        rs = _rms_inv(qa, qb, NSA_DK)
        qa = _rope(qa * rs * gq_ref[:, :LANES], cq_ref[...], s1q_ref[...], s2q_ref[...], half) * scale
        qb = qb * rs * gq_ref[:, LANES:] * scale
        q_s[r * tq:(r + 1) * tq, :LANES] = qa.astype(BF16)
        q_s[r * tq:(r + 1) * tq, LANES:] = qb.astype(BF16)

    t_rows = t0 + (lax.broadcasted_iota(I32, (R * tq, 1), 0) & (tq - 1))

    sc = lax.dot_general(q_s[...], kc_ref[0, 0], _NT, preferred_element_type=F32)
    cend = lax.broadcasted_iota(I32, (1, ncp), 1) * CMP_STRIDE + (CMP_LEN - 1)
    mask_c = cend <= t_rows
    sc = jnp.where(mask_c, sc, NEG_INF)
    e = jnp.where(mask_c, jnp.exp(sc - jnp.max(sc, axis=-1, keepdims=True)), 0.0)
    p = e / jnp.maximum(jnp.sum(e, axis=-1, keepdims=True), 1e-30)
    o_c = jnp.dot(p.astype(BF16), vc_ref[0, 0], preferred_element_type=F32)

    ps = p[0:tq] + p[tq:2 * tq] + p[2 * tq:3 * tq] + p[3 * tq:4 * tq]
    ps_hi = ps.astype(BF16)
    ps_lo = (ps - ps_hi.astype(F32)).astype(BF16)
    imp = (jnp.dot(ps_hi, ov_ref[...], preferred_element_type=F32)
           + jnp.dot(ps_lo, ov_ref[...], preferred_element_type=F32))
    jb = lax.broadcasted_iota(I32, (tq, LANES), 1) - SEL_LANE0
    tb = (t0 + lax.broadcasted_iota(I32, (tq, 1), 0)) // SLC_LEN
    forced = (jb == 0) | (jb == tb) | (jb == tb - 1)
    score = jnp.where(forced, FORCE, jnp.where(jb <= tb, imp, -FORCE))
    rank = jnp.zeros((tq, LANES), I32)
    for j in range(nb):
        col = score[:, SEL_LANE0 + j:SEL_LANE0 + j + 1]
        rank = rank + ((col > score) | ((col == score) & (j < jb))).astype(I32)
    in_range = (jb >= 0) & (jb < nb)
    bias = jnp.where(in_range & (rank >= topk), NEG_INF, 0.0)

    for r in range(R):
        qb = q_s[r * tq:(r + 1) * tq, LANES:].astype(F32)
        q_s[r * tq:(r + 1) * tq, LANES:] = (qb + bias).astype(BF16)
    q = q_s[...]
    o_s = _flash(q, lambda off: ks_s[pl.ds(off, tk), :], lambda off: vs_s[pl.ds(off, tk), :],
                 t_rows, t0 // tk, tk, NSA_DV)

    span = tq + WINDOW
    sw = lax.dot_general(q, kw_s[pl.ds(t0, span), :], _NT, preferred_element_type=F32)
    pk = t0 - WINDOW + lax.broadcasted_iota(I32, (1, span), 1)
    d = t_rows - pk
    sw = jnp.where((pk >= 0) & (d >= 0) & (d < WINDOW), sw, NEG_INF)
    ew = jnp.exp(sw - jnp.max(sw, axis=-1, keepdims=True))
    o_w = jnp.dot(ew.astype(BF16), vw_s[pl.ds(t0, span), :], preferred_element_type=F32)
    o_w = o_w / jnp.sum(ew, axis=-1, keepdims=True)

    gs = jax.nn.sigmoid(gate_ref[...])
    for r in range(R):
        rows = slice(r * tq, (r + 1) * tq)
        o = (gs[:, 3 * r:3 * r + 1] * o_c[rows] + gs[:, 3 * r + 1:3 * r + 2] * o_s[rows]
             + gs[:, 3 * r + 2:3 * r + 3] * o_w[rows])
        o_ref[:, r * NSA_DV:(r + 1) * NSA_DV] = o.astype(BF16)


def _nsa_attention(proj, k_c, v_c, gq, gks, gkw, tabs, ov, B, T, tq=128, tk=256):
    nq = T // tq
    G, R = NSA_GROUPS, NSA_HPG
    ncp = k_c.shape[2]
    c, s1, s2 = tabs
    qtab = pl.BlockSpec((tq, LANES), lambda b, g, i: (i, 0))
    ktab = pl.BlockSpec((T, LANES), lambda b, g, i: (0, 0))
    gspec = pl.BlockSpec((1, HEAD_PAD), lambda b, g, i: (0, 0))
    return pl.pallas_call(
        functools.partial(_nsa_kernel, tq=tq, tk=tk, T=T, scale=NSA_DK ** -0.5),
        grid=(B, G, nq),
        in_specs=[pl.BlockSpec((tq, R * HEAD_PAD), lambda b, g, i: (b * nq + i, g)),
                  pl.BlockSpec((T, HEAD_PAD), lambda b, g, i: (b, C_KS // HEAD_PAD + g)),
                  pl.BlockSpec((T, HEAD_PAD), lambda b, g, i: (b, C_KW // HEAD_PAD + g)),
                  pl.BlockSpec((T, NSA_DV), lambda b, g, i: (b, C_VS // NSA_DV + g)),
                  pl.BlockSpec((T, NSA_DV), lambda b, g, i: (b, C_VW // NSA_DV + g)),
                  pl.BlockSpec((tq, LANES), lambda b, g, i: (b * nq + i, C_GATE // LANES + g)),
                  pl.BlockSpec((1, 1, ncp, HEAD_PAD), lambda b, g, i: (b, g, 0, 0)),
                  pl.BlockSpec((1, 1, ncp, NSA_DV), lambda b, g, i: (b, g, 0, 0)),
                  gspec, gspec, gspec, qtab, qtab, qtab, ktab, ktab, ktab,
                  pl.BlockSpec((ncp, LANES), lambda b, g, i: (0, 0))],
        out_specs=pl.BlockSpec((tq, R * NSA_DV), lambda b, g, i: (b * nq + i, g)),
        out_shape=jax.ShapeDtypeStruct((B * T, NSA_HEADS * NSA_DV), BF16),
        scratch_shapes=[pltpu.VMEM((T, HEAD_PAD), BF16), pltpu.VMEM((WINDOW + T, HEAD_PAD), BF16),
                        pltpu.VMEM((T, NSA_DV), BF16), pltpu.VMEM((WINDOW + T, NSA_DV), BF16),
                        pltpu.VMEM((R * tq, HEAD_PAD), BF16)],
        compiler_params=_params(("parallel", "parallel", "arbitrary")),
    )(proj, proj, proj, proj, proj, proj, k_c, v_c, gq, gks, gkw, c, s1, s2, c, s1, s2, ov)


def _pad_cols(w, n):
    return jnp.pad(w, ((0, 0), (0, n - w.shape[1])))


def _pad_heads(w, nh, d, dp):
    k = w.shape[0]
    return jnp.pad(w.reshape(k, nh, d), ((0, 0), (0, 0), (0, dp - d))).reshape(k, nh * dp)


def _layout_w_in(w_in):
    sizes = (MLA_RANK, MLA_RANK, MLA_ROPE, NSA_HEADS * NSA_DK,
             NSA_GROUPS * NSA_DK, NSA_GROUPS * NSA_DV, NSA_GROUPS * NSA_DK, NSA_GROUPS * NSA_DV,
             NSA_GROUPS * NSA_DK, NSA_GROUPS * NSA_DV, NSA_HEADS * 3, 2 * D_MODEL)
    offs, o = [], 0
    for s in sizes:
        offs.append((o, o + s))
        o += s
    cq, ckv, kr, qn, kc, vc, ks, vs, kw, vw, gn, gm = (w_in[:, a:b] for a, b in offs)
    w = jnp.concatenate([
        _pad_heads(qn, NSA_HEADS, NSA_DK, HEAD_PAD), gm,
        _pad_heads(ks, NSA_GROUPS, NSA_DK, HEAD_PAD), _pad_heads(kw, NSA_GROUPS, NSA_DK, HEAD_PAD), vs, vw,
        cq, ckv, _pad_cols(kr, HEAD_PAD), _pad_heads(gn, NSA_GROUPS, NSA_HPG * 3, LANES), kc, vc], axis=1)
    assert w.shape[1] == C_END
    return w.astype(BF16)


def _rope_tables(pos, rot_dim):
    half = rot_dim // 2
    inv = ROPE_THETA ** (-jnp.arange(0, rot_dim, 2, dtype=F32) / rot_dim)
    ang = pos.astype(F32)[:, None] * inv[None, :]
    cos, sin = jnp.cos(ang), jnp.sin(ang)
    n = pos.shape[0]
    c = jnp.ones((n, LANES), F32).at[:, :rot_dim].set(jnp.concatenate([cos, cos], axis=1))
    s1 = jnp.zeros((n, LANES), F32).at[:, :half].set(-sin)
    s2 = jnp.zeros((n, LANES), F32).at[:, half:rot_dim].set(sin)
    return c, s1, s2


def _overlap_table(ncp, nb):
    cs = jnp.arange(ncp)[:, None] * CMP_STRIDE
    ss = jnp.arange(nb)[None, :] * SLC_LEN
    ov = jnp.clip(jnp.minimum(cs + CMP_LEN, ss + SLC_LEN) - jnp.maximum(cs, ss), 0, None).astype(F32) / CMP_LEN
    return jnp.zeros((ncp, LANES), F32).at[:, SEL_LANE0:SEL_LANE0 + nb].set(ov).astype(BF16)


def _gain(g, n):
    return jnp.pad(g, (0, n - g.shape[0])).reshape(1, n)


def _layer(x, attn_norm_g, w_in, mla_q_lat_g, mla_kv_lat_g, mla_w_uq, mla_w_uk, mla_w_uv,
           mla_q_norm_g, mla_k_norm_g, nsa_q_norm_g, nsa_k_norm_g, cmp_pe_k, cmp_pe_v,
           cmp_k_w1, cmp_k_b1, cmp_k_w2, cmp_k_b2, cmp_v_w1, cmp_v_b1, cmp_v_w2, cmp_v_b2,
           w_proj_mla, w_proj_nsa, w_out, ffn_norm_g, w_ffn_gate, w_ffn_up, w_ffn_down):
    B, T, _ = x.shape
    n = B * T
    ncp = T // CMP_STRIDE
    nb = T // SLC_LEN
    assert T % 512 == 0 and ncp % LANES == 0 and SEL_LANE0 + nb <= LANES and T >= WINDOW
    x2 = x.reshape(n, D_MODEL)
    pos = jnp.arange(T)

    proj = _norm_matmul(x2, 0, D_MODEL, attn_norm_g.reshape(1, -1), _layout_w_in(w_in), tm=512, tn=1024)

    q_up = _norm_matmul(proj, C_CQ // MLA_RANK, MLA_RANK, mla_q_lat_g.reshape(1, -1),
                        _pad_heads(mla_w_uq, MLA_HEADS, MLA_QK, HEAD_PAD).astype(BF16), tm=512, tn=1024)
    kv_up = _norm_matmul(proj, C_CKV // MLA_RANK, MLA_RANK, mla_kv_lat_g.reshape(1, -1),
                         jnp.concatenate([mla_w_uk, mla_w_uv], axis=1).astype(BF16), tm=512, tn=1024)
    o_mla = _mla_attention(q_up, kv_up, proj, _gain(mla_q_norm_g, HEAD_PAD), _gain(mla_k_norm_g, HEAD_PAD),
                           _rope_tables(pos, MLA_ROPE), B, T)

    def chunks(c0, d):
        t = proj[:, c0:c0 + NSA_GROUPS * d].reshape(B, ncp, CMP_STRIDE, NSA_GROUPS, d)
        return t.transpose(0, 3, 1, 2, 4).reshape(B, NSA_GROUPS, ncp, CMP_STRIDE * d)

    cmp_tabs = _rope_tables(jnp.arange(ncp) * CMP_STRIDE + CMP_LEN - 1, NSA_ROPE)
    k_c = _compress(chunks(C_KC, NSA_DK), cmp_pe_k, cmp_k_w1, cmp_k_b1, _pad_cols(cmp_k_w2, HEAD_PAD),
                    _gain(cmp_k_b2, HEAD_PAD)[0], _gain(nsa_k_norm_g[0], HEAD_PAD), cmp_tabs, True)
    v_c = _compress(chunks(C_VC, NSA_DV), cmp_pe_v, cmp_v_w1, cmp_v_b1, cmp_v_w2, cmp_v_b2,
                    jnp.zeros((1, NSA_DV), F32), cmp_tabs, False)
    o_nsa = _nsa_attention(proj, k_c, v_c, _gain(nsa_q_norm_g, HEAD_PAD), _gain(nsa_k_norm_g[1], HEAD_PAD),
                           _gain(nsa_k_norm_g[2], HEAD_PAD), _rope_tables(pos, NSA_ROPE),
                           _overlap_table(ncp, nb), B, T)

    u = _merge(o_mla, o_nsa, w_proj_mla.astype(BF16), w_proj_nsa.astype(BF16), proj)
    h, hn = _out_proj(u, w_out.astype(BF16), x2, ffn_norm_g.reshape(1, -1))
    act = _ffn_up(hn, w_ffn_gate.astype(BF16), w_ffn_up.astype(BF16))
    out = _ffn_down(act, w_ffn_down.astype(BF16), h)
    return out.reshape(B, T, D_MODEL)


def kernel(x, attn_norm_g, w_in, mla_q_lat_g, mla_kv_lat_g, mla_w_uq, mla_w_uk, mla_w_uv, mla_q_norm_g, mla_k_norm_g, nsa_q_norm_g, nsa_k_norm_g, cmp_pe_k, cmp_pe_v, cmp_k_w1, cmp_k_b1, cmp_k_w2, cmp_k_b2, cmp_v_w1, cmp_v_b1, cmp_v_w2, cmp_v_b2, w_proj_mla, w_proj_nsa, w_out, ffn_norm_g, w_ffn_gate, w_ffn_up, w_ffn_down):
    params = (attn_norm_g, w_in, mla_q_lat_g, mla_kv_lat_g, mla_w_uq, mla_w_uk, mla_w_uv, mla_q_norm_g,
              mla_k_norm_g, nsa_q_norm_g, nsa_k_norm_g, cmp_pe_k, cmp_pe_v, cmp_k_w1, cmp_k_b1, cmp_k_w2,
              cmp_k_b2, cmp_v_w1, cmp_v_b1, cmp_v_w2, cmp_v_b2, w_proj_mla, w_proj_nsa, w_out, ffn_norm_g,
              w_ffn_gate, w_ffn_up, w_ffn_down)
    h = x
    for l in range(attn_norm_g.shape[0]):
        h = _layer(h, *(p[l] for p in params))
    return h
```

```python
import functools

import jax
import jax.numpy as jnp
from jax import lax
from jax.experimental import pallas as pl
from jax.experimental.pallas import tpu as pltpu

F32, BF16, I32 = jnp.float32, jnp.bfloat16, jnp.int32

EPS = 1e-6
NEG_INF = -1e30
FORCE = 1e9
ROPE_THETA = 500000.0

D_MODEL = 2048
MLA_HEADS, MLA_NOPE, MLA_ROPE, MLA_V = 16, 128, 64, 128
MLA_QK = MLA_NOPE + MLA_ROPE
MLA_RANK = 512
NSA_HEADS, NSA_GROUPS, NSA_DK, NSA_DV = 16, 4, 192, 128
NSA_HPG = NSA_HEADS // NSA_GROUPS
NSA_ROPE = NSA_DK // 4
CMP_LEN, CMP_STRIDE = 32, 16
SLC_LEN, SLC_TOPK, WINDOW = 64, 16, 512
D_FF = 5632

LANES = 128
HALF_LANES = 64
HEAD_PAD = 256
SEL_LANE0 = 64
PAD_LANE = 127
DV = 128
DV_EXT = DV + 16
VMEM_LIMIT = 48 * 1024 * 1024

C_QN, C_GM, C_KS, C_KW, C_VS, C_VW = 0, 4096, 8192, 9216, 10240, 10752
C_CQ, C_CKV, C_KR, C_GATE, C_KC, C_VC, C_END = 11264, 11776, 12288, 12544, 13056, 13824, 14336

_NT = (((1,), (1,)), ((), ()))


def _params(sem):
    return pltpu.CompilerParams(dimension_semantics=sem, vmem_limit_bytes=VMEM_LIMIT)


def _norm_matmul_kernel(x_ref, g_ref, w_ref, o_ref, xn_ref):
    @pl.when(pl.program_id(1) == 0)
    def _():
        x = x_ref[...]
        ms = jnp.mean(x * x, axis=-1, keepdims=True)
        xn_ref[...] = (x * lax.rsqrt(ms + EPS) * g_ref[...]).astype(BF16)

    o_ref[...] = jnp.dot(xn_ref[...], w_ref[...], preferred_element_type=F32).astype(o_ref.dtype)


def _norm_matmul(x, xcol, k, g, w, tm, tn, name):
    n, m = x.shape[0], w.shape[1]
    return pl.pallas_call(
        _norm_matmul_kernel,
        grid=(n // tm, m // tn),
        in_specs=[pl.BlockSpec((tm, k), lambda i, j: (i, xcol)),
                  pl.BlockSpec((1, k), lambda i, j: (0, 0)),
                  pl.BlockSpec((k, tn), lambda i, j: (0, j))],
        out_specs=pl.BlockSpec((tm, tn), lambda i, j: (i, j)),
        out_shape=jax.ShapeDtypeStruct((n, m), F32),
        scratch_shapes=[pltpu.VMEM((tm, k), BF16)],
        compiler_params=_params(("parallel", "arbitrary")),
        name=name,
    )(x, g, w)


def _merge_kernel(om_ref, on_ref, wm_ref, wn_ref, g0_ref, g1_ref, o_ref):
    ym = jnp.dot(om_ref[...], wm_ref[...], preferred_element_type=F32)
    yn = jnp.dot(on_ref[...], wn_ref[...], preferred_element_type=F32)
    o_ref[...] = (jax.nn.sigmoid(g0_ref[...]) * ym + jax.nn.sigmoid(g1_ref[...]) * yn).astype(BF16)


def _merge(o_mla, o_nsa, w_pm, w_pn, proj, tm=512, tn=512):
    n = o_mla.shape[0]
    g0, g1 = C_GM // tn, (C_GM + D_MODEL) // tn
    return pl.pallas_call(
        _merge_kernel,
        grid=(n // tm, D_MODEL // tn),
        in_specs=[pl.BlockSpec((tm, D_MODEL), lambda i, j: (i, 0)),
                  pl.BlockSpec((tm, D_MODEL), lambda i, j: (i, 0)),
                  pl.BlockSpec((D_MODEL, tn), lambda i, j: (0, j)),
                  pl.BlockSpec((D_MODEL, tn), lambda i, j: (0, j)),
                  pl.BlockSpec((tm, tn), lambda i, j: (i, g0 + j)),
                  pl.BlockSpec((tm, tn), lambda i, j: (i, g1 + j))],
        out_specs=pl.BlockSpec((tm, tn), lambda i, j: (i, j)),
        out_shape=jax.ShapeDtypeStruct((n, D_MODEL), BF16),
        compiler_params=_params(("parallel", "arbitrary")),
        name="merge",
    )(o_mla, o_nsa, w_pm, w_pn, proj, proj)


def _out_proj_kernel(u_ref, w_ref, x_ref, g_ref, h_ref, hn_ref):
    h = x_ref[...] + jnp.dot(u_ref[...], w_ref[...], preferred_element_type=F32)
    h_ref[...] = h
    ms = jnp.mean(h * h, axis=-1, keepdims=True)
    hn_ref[...] = (h * lax.rsqrt(ms + EPS) * g_ref[...]).astype(BF16)


def _out_proj(u, w_out, x, g, tm=256):
    n = u.shape[0]
    row = pl.BlockSpec((tm, D_MODEL), lambda i: (i, 0))
    return pl.pallas_call(
        _out_proj_kernel,
        grid=(n // tm,),
        in_specs=[row, pl.BlockSpec((D_MODEL, D_MODEL), lambda i: (0, 0)), row,
                  pl.BlockSpec((1, D_MODEL), lambda i: (0, 0))],
        out_specs=[row, row],
        out_shape=[jax.ShapeDtypeStruct((n, D_MODEL), F32), jax.ShapeDtypeStruct((n, D_MODEL), BF16)],
        compiler_params=_params(("parallel",)),
        name="out_proj",
    )(u, w_out, x, g)


def _ffn_up_kernel(hn_ref, wg_ref, wu_ref, o_ref):
    hn = hn_ref[...]
    a = jnp.dot(hn, wg_ref[...], preferred_element_type=F32)
    b = jnp.dot(hn, wu_ref[...], preferred_element_type=F32)
    o_ref[...] = (jax.nn.silu(a) * b).astype(BF16)


def _ffn_up(hn, w_gate, w_up, tm=512, tn=512):
    n = hn.shape[0]
    return pl.pallas_call(
        _ffn_up_kernel,
        grid=(n // tm, D_FF // tn),
        in_specs=[pl.BlockSpec((tm, D_MODEL), lambda i, j: (i, 0)),
                  pl.BlockSpec((D_MODEL, tn), lambda i, j: (0, j)),
                  pl.BlockSpec((D_MODEL, tn), lambda i, j: (0, j))],
        out_specs=pl.BlockSpec((tm, tn), lambda i, j: (i, j)),
        out_shape=jax.ShapeDtypeStruct((n, D_FF), BF16),
        compiler_params=_params(("parallel", "arbitrary")),
        name="ffn_up",
    )(hn, w_gate, w_up)


def _ffn_down_kernel(a_ref, w_ref, h_ref, o_ref):
    o_ref[...] = h_ref[...] + jnp.dot(a_ref[...], w_ref[...], preferred_element_type=F32)


def _ffn_down(act, w_down, h, tm=512, tn=512):
    n = act.shape[0]
    return pl.pallas_call(
        _ffn_down_kernel,
        grid=(n // tm, D_MODEL // tn),
        in_specs=[pl.BlockSpec((tm, D_FF), lambda i, j: (i, 0)),
                  pl.BlockSpec((D_FF, tn), lambda i, j: (0, j)),
                  pl.BlockSpec((tm, tn), lambda i, j: (i, j))],
        out_specs=pl.BlockSpec((tm, tn), lambda i, j: (i, j)),
        out_shape=jax.ShapeDtypeStruct((n, D_MODEL), F32),
        compiler_params=_params(("parallel", "arbitrary")),
        name="ffn_down",
    )(act, w_down, h)


def _rms_inv(xa, xb, d):
    return lax.rsqrt(jnp.sum(xa * xa + xb * xb, axis=-1, keepdims=True) * (1.0 / d) + EPS)


def _rope(x, c, s):
    return x * c + pltpu.roll(x, HALF_LANES, 1) * s


def _col_max(s, slabs=8):
    rows = s.shape[0]
    if rows % (8 * slabs) == 0:
        s = jnp.max(s.reshape(slabs, rows // slabs, s.shape[1]), axis=0)
    return jnp.max(s, axis=0, keepdims=True)


def _col_sum(s, slabs=8):
    rows = s.shape[0]
    if rows % (8 * slabs) == 0:
        s = jnp.sum(s.reshape(slabs, rows // slabs, s.shape[1]), axis=0)
    return jnp.sum(s, axis=0, keepdims=True)


def _flash_t(chains, n_last, n_max, tk, s_buf0, s_buf1, mt_buf0, mt_buf1, m_ref, acc_ref):
    nc = len(chains)
    s_buf, mt_buf = (s_buf0, s_buf1), (mt_buf0, mt_buf1)

    def stage_a(j):
        off = j * tk
        for c, (q_at, k_at, _, t_cols) in enumerate(chains):
            s = lax.dot_general(k_at(off), q_at(), _NT, preferred_element_type=F32)
            pk = off + lax.broadcasted_iota(I32, (tk, 1), 0)
            s = jnp.where(pk <= t_cols, s, NEG_INF)
            s_buf[j % 2][c] = s
            mt_buf[j % 2][c] = _col_max(s)

    def stage_b(j):
        off = j * tk
        for c, (_, _, vt_at, _) in enumerate(chains):
            m = m_ref[c]
            m_new = jnp.maximum(m, mt_buf[j % 2][c])
            p = jnp.exp(s_buf[j % 2][c] - m_new).astype(BF16)
            acc_ref[c] = jnp.exp(m - m_new) * acc_ref[c] + jnp.dot(vt_at(off), p, preferred_element_type=F32)
            m_ref[c] = m_new

    m_ref[...] = jnp.full(m_ref.shape, NEG_INF, F32)
    acc_ref[...] = jnp.zeros(acc_ref.shape, F32)
    stage_a(0)
    for j in range(n_max):
        if j + 1 < n_max:
            @pl.when(j < n_last)
            def _(j=j):
                stage_a(j + 1)
                stage_b(j)

        @pl.when(j == n_last)
        def _(j=j):
            stage_b(j)

    return [acc_ref[c, :DV] * (1.0 / acc_ref[c, DV:DV + 1]) for c in range(nc)]


def _flash_scratch(nc, tk, nq):
    return [pltpu.VMEM((nc, tk, nq), F32), pltpu.VMEM((nc, tk, nq), F32),
            pltpu.VMEM((nc, 1, nq), F32), pltpu.VMEM((nc, 1, nq), F32),
            pltpu.VMEM((nc, 1, nq), F32), pltpu.VMEM((nc, DV_EXT, nq), F32)]


def _mla_kernel(q_ref, kn_ref, kr_ref, v_ref, gq_ref, gk_ref, cq_ref, sq_ref, ck_ref, sk_ref,
                o_ref, k_s, vt_s, q_s, *flash_scratch, tq, tk, scale):
    qi = pl.program_id(2)
    T = kn_ref.shape[0]

    @pl.when(qi == 0)
    def _():
        kr = kr_ref[:, :LANES]
        ss_r = jnp.sum(kr * kr, axis=-1, keepdims=True)
        for h in range(2):
            kn = kn_ref[:, h * LANES:(h + 1) * LANES]
            r = lax.rsqrt((jnp.sum(kn * kn, axis=-1, keepdims=True) + ss_r) * (1.0 / MLA_QK) + EPS)
            k_s[h, :, :LANES] = (kn * r * gk_ref[:, :LANES]).astype(BF16)
            k_s[h, :, LANES:] = _rope(kr * r * gk_ref[:, LANES:], ck_ref[...], sk_ref[...]).astype(BF16)
            vt_s[h, :DV, :] = v_ref[:, h * LANES:(h + 1) * LANES].T.astype(BF16)
            vt_s[h, DV:, :] = jnp.ones((DV_EXT - DV, T), BF16)

    t0 = qi * tq
    t_cols = t0 + lax.broadcasted_iota(I32, (1, tq), 1)
    chains = []
    for h in range(2):
        qa = q_ref[:, h * HEAD_PAD:h * HEAD_PAD + LANES]
        qb = q_ref[:, h * HEAD_PAD + LANES:(h + 1) * HEAD_PAD]
        r = _rms_inv(qa, qb, MLA_QK) * scale
        qa = qa * r * gq_ref[:, :LANES]
        qb = _rope(qb * r * gq_ref[:, LANES:], cq_ref[...], sq_ref[...])
        q_s[h, :, :LANES] = qa.astype(BF16)
        q_s[h, :, LANES:] = qb.astype(BF16)
        chains.append((lambda h=h: q_s[h], lambda off, h=h: k_s[h, pl.ds(off, tk), :],
                       lambda off, h=h: vt_s[h, :, pl.ds(off, tk)], t_cols))
    for h, o_t in enumerate(_flash_t(chains, t0 // tk, T // tk, tk, *flash_scratch)):
        o_ref[:, h * LANES:(h + 1) * LANES] = o_t.T.astype(BF16)


def _mla_attention(q_up, kv_up, proj, gq, gk, tabs, B, T, tq=256, tk=512):
    nq = T // tq
    c, s = tabs
    qtab = pl.BlockSpec((tq, LANES), lambda b, h, i: (i, 0))
    ktab = pl.BlockSpec((T, LANES), lambda b, h, i: (0, 0))
    gspec = pl.BlockSpec((1, HEAD_PAD), lambda b, h, i: (0, 0))
    return pl.pallas_call(
        functools.partial(_mla_kernel, tq=tq, tk=tk, scale=MLA_QK ** -0.5),
        grid=(B, MLA_HEADS // 2, nq),
        in_specs=[pl.BlockSpec((tq, 2 * HEAD_PAD), lambda b, h, i: (b * nq + i, h)),
                  pl.BlockSpec((T, 2 * LANES), lambda b, h, i: (b, h)),
                  pl.BlockSpec((T, HEAD_PAD), lambda b, h, i: (b, C_KR // HEAD_PAD)),
                  pl.BlockSpec((T, 2 * LANES), lambda b, h, i: (b, MLA_HEADS // 2 + h)),
                  gspec, gspec, qtab, qtab, ktab, ktab],
        out_specs=pl.BlockSpec((tq, 2 * LANES), lambda b, h, i: (b * nq + i, h)),
        out_shape=jax.ShapeDtypeStruct((B * T, MLA_HEADS * MLA_V), BF16),
        scratch_shapes=[pltpu.VMEM((2, T, HEAD_PAD), BF16), pltpu.VMEM((2, DV_EXT, T), BF16),
                        pltpu.VMEM((2, tq, HEAD_PAD), BF16)] + _flash_scratch(2, tk, tq),
        compiler_params=_params(("parallel", "parallel", "arbitrary")),
        name="mla_attention",
    )(q_up, kv_up, proj, kv_up, gq, gk, c, s, c, s)


def _compress_kernel(x_ref, pea_ref, peb_ref, w1a_ref, w1b_ref, b1_ref, w2_ref, b2_ref,
                     g_ref, c_ref, s_ref, o_ref, *, is_key, gb, ncp):
    rows = gb * ncp
    x = x_ref[0].reshape(rows, x_ref.shape[-1])
    pa = jnp.dot((x + pea_ref[...]).astype(BF16), w1a_ref[...], preferred_element_type=F32)
    pb = jnp.dot((x + peb_ref[...]).astype(BF16), w1b_ref[...], preferred_element_type=F32)
    h = pa + pltpu.roll(pb, rows - 1, 0) + b1_ref[...]
    y = jnp.dot(jax.nn.gelu(h).astype(BF16), w2_ref[...], preferred_element_type=F32) + b2_ref[...]
    if is_key:
        ya, yb = y[:, :LANES], y[:, LANES:]
        r = _rms_inv(ya, yb, NSA_DK)
        c = jnp.concatenate([c_ref[...]] * gb, axis=0)
        s = jnp.concatenate([s_ref[...]] * gb, axis=0)
        ya = _rope(ya * r * g_ref[:, :LANES], c, s)
        yb = yb * r * g_ref[:, LANES:]
        y = jnp.concatenate([ya, yb], axis=1)
    o_ref[0] = y.astype(BF16).reshape(gb, ncp, y.shape[-1])


def _compress(xc, pe, w1, b1, w2, b2, g, tabs, is_key, name, gb=2):
    B, G, ncp, kd = xc.shape
    dout = w2.shape[1]
    hid = w1.shape[1]
    pea = pe[:CMP_STRIDE].reshape(1, kd)
    peb = pe[CMP_STRIDE:].reshape(1, kd)
    w1a, w1b = w1[:kd].astype(BF16), w1[kd:].astype(BF16)
    c, s = tabs
    const = lambda shape: pl.BlockSpec(shape, lambda b, gi: (0,) * len(shape))
    return pl.pallas_call(
        functools.partial(_compress_kernel, is_key=is_key, gb=gb, ncp=ncp),
        grid=(B, G // gb),
        in_specs=[pl.BlockSpec((1, gb, ncp, kd), lambda b, gi: (b, gi, 0, 0)),
                  const((1, kd)), const((1, kd)), const((kd, hid)), const((kd, hid)), const((1, hid)),
                  const((hid, dout)), const((1, dout)), const((1, dout)),
                  const((ncp, LANES)), const((ncp, LANES))],
        out_specs=pl.BlockSpec((1, gb, ncp, dout), lambda b, gi: (b, gi, 0, 0)),
        out_shape=jax.ShapeDtypeStruct((B, G, ncp, dout), BF16),
        compiler_params=_params(("parallel", "parallel")),
        name=name,
    )(xc, pea, peb, w1a, w1b, b1.reshape(1, hid), w2.astype(BF16), b2.reshape(1, dout), g, c, s)


def _nsa_kernel(q_ref, ks_ref, kw_ref, vs_ref, vw_ref, gate_ref, kc_ref, vc_ref, gq_ref, gks_ref, gkw_ref,
                cq_ref, sq_ref, ck_ref, sk_ref, ovt_ref,
                o_ref, ks_s, kw_s, vst_s, vwt_s, vct_s, q_s, *flash_scratch, tq, tk, T, scale):
    qi = pl.program_id(2)
    R = NSA_HPG
    nq = R * tq
    nb = T // SLC_LEN
    topk = min(SLC_TOPK, nb)
    ncp = kc_ref.shape[2]

    @pl.when(qi == 0)
    def _():
        def prep_k(k_ref, g_ref):
            ka, kb = k_ref[:, :LANES], k_ref[:, LANES:]
            r = _rms_inv(ka, kb, NSA_DK)
            return _rope(ka * r * g_ref[:, :LANES], ck_ref[...], sk_ref[...]), kb * r * g_ref[:, LANES:]

        ka, kb = prep_k(ks_ref, gks_ref)
        blk = lax.broadcasted_iota(I32, (T, LANES), 0) // SLC_LEN
        lane = lax.broadcasted_iota(I32, (T, LANES), 1)
        ks_s[:, :LANES] = ka.astype(BF16)
        ks_s[:, LANES:] = (kb + (lane - SEL_LANE0 == blk).astype(F32)).astype(BF16)
        ka, kb = prep_k(kw_ref, gkw_ref)
        lane_w = lax.broadcasted_iota(I32, (WINDOW, LANES), 1)
        kw_s[:WINDOW, :LANES] = jnp.zeros((WINDOW, LANES), BF16)
        kw_s[:WINDOW, LANES:] = jnp.where(lane_w == PAD_LANE, NEG_INF, 0.0).astype(BF16)
        kw_s[WINDOW:, :LANES] = ka.astype(BF16)
        kw_s[WINDOW:, LANES:] = kb.astype(BF16)
        vst_s[:DV, :] = vs_ref[...].T.astype(BF16)
        vst_s[DV:, :] = jnp.ones((DV_EXT - DV, T), BF16)
        vwt_s[:DV, :WINDOW] = jnp.zeros((DV, WINDOW), BF16)
        vwt_s[:DV, WINDOW:] = vw_ref[...].T.astype(BF16)
        vwt_s[DV:, :] = jnp.ones((DV_EXT - DV, WINDOW + T), BF16)
        vct_s[...] = vc_ref[0, 0].astype(F32).T.astype(BF16)

    t0 = pl.multiple_of(qi * tq, tq)
    lane_q = lax.broadcasted_iota(I32, (tq, LANES), 1)
    for r in range(R):
        qa = q_ref[:, r * HEAD_PAD:r * HEAD_PAD + LANES]
        qb = q_ref[:, r * HEAD_PAD + LANES:(r + 1) * HEAD_PAD]
        rs = _rms_inv(qa, qb, NSA_DK) * scale
        qa = _rope(qa * rs * gq_ref[:, :LANES], cq_ref[...], sq_ref[...])
        qb = jnp.where(lane_q == PAD_LANE, 1.0, qb * rs * gq_ref[:, LANES:])
        q_s[r * tq:(r + 1) * tq, :LANES] = qa.astype(BF16)
        q_s[r * tq:(r + 1) * tq, LANES:] = qb.astype(BF16)

    c_q = lax.broadcasted_iota(I32, (1, nq), 1) & (tq - 1)
    t_cols = t0 + c_q
    q = q_s[...]

    span = tq + WINDOW
    sw = lax.dot_general(kw_s[pl.ds(t0, span), :], q, _NT, preferred_element_type=F32)
    i_k = lax.broadcasted_iota(I32, (tq, 1), 0)
    s_lo = jnp.where(i_k > c_q, sw[:tq], NEG_INF)
    s_mid = sw[tq:WINDOW]
    s_hi = jnp.where(i_k <= c_q, sw[WINDOW:], NEG_INF)
    m = jnp.maximum(jnp.maximum(_col_max(s_lo), _col_max(s_mid)), _col_max(s_hi))
    pw = jnp.concatenate([jnp.exp(s_lo - m), jnp.exp(s_mid - m), jnp.exp(s_hi - m)], axis=0).astype(BF16)
    acc = jnp.dot(vwt_s[:, pl.ds(t0, span)], pw, preferred_element_type=F32)
    o_w = acc[:DV] * (1.0 / acc[DV:DV + 1])

    sc = lax.dot_general(kc_ref[0, 0], q, _NT, preferred_element_type=F32)
    cend = lax.broadcasted_iota(I32, (ncp, 1), 0) * CMP_STRIDE + (CMP_LEN - 1)
    mask_c = cend <= t_cols
    sc = jnp.where(mask_c, sc, NEG_INF)
    e = jnp.where(mask_c, jnp.exp(sc - _col_max(sc)), 0.0)
    p = e * (1.0 / jnp.maximum(_col_sum(e), 1e-30))
    o_c = jnp.dot(vct_s[...], p.astype(BF16), preferred_element_type=F32)

    ps = p[:, 0:tq] + p[:, tq:2 * tq] + p[:, 2 * tq:3 * tq] + p[:, 3 * tq:4 * tq]
    ps_hi = ps.astype(BF16)
    ps_lo = (ps - ps_hi.astype(F32)).astype(BF16)
    imp = (jnp.dot(ovt_ref[...], ps_hi, preferred_element_type=F32)
           + jnp.dot(ovt_ref[...], ps_lo, preferred_element_type=F32))
    jb = lax.broadcasted_iota(I32, (nb, tq), 0)
    tb = (t0 + lax.broadcasted_iota(I32, (1, tq), 1)) // SLC_LEN
    forced = (jb == 0) | (jb == tb) | (jb == tb - 1)
    score = jnp.where(forced, FORCE, jnp.where(jb <= tb, imp, -FORCE))
    rank = jnp.zeros((nb, tq), I32)
    for j in range(nb):
        row = score[j:j + 1, :]
        rank = rank + ((row > score) | ((row == score) & (j < jb))).astype(I32)
    unsel = (rank >= topk).astype(F32)
    unsel = jnp.concatenate([jnp.zeros((SEL_LANE0, tq), F32), unsel,
                             jnp.zeros((LANES - SEL_LANE0 - nb, tq), F32)], axis=0)
    bias = unsel.T * NEG_INF

    for r in range(R):
        qb = q_s[r * tq:(r + 1) * tq, LANES:].astype(F32)
        q_s[r * tq:(r + 1) * tq, LANES:] = (qb + bias).astype(BF16)
    half = nq // 2
    k_at = lambda off: ks_s[pl.ds(off, tk), :]
    vt_at = lambda off: vst_s[:, pl.ds(off, tk)]
    o_s = jnp.concatenate(_flash_t([(lambda: q_s[:half], k_at, vt_at, t_cols[:, :half]),
                                    (lambda: q_s[half:], k_at, vt_at, t_cols[:, half:])],
                                   t0 // tk, T // tk, tk, *flash_scratch), axis=1)

    gs = jax.nn.sigmoid(gate_ref[...]).T
    for r in range(R):
        cols = slice(r * tq, (r + 1) * tq)
        o = (gs[3 * r:3 * r + 1] * o_c[:, cols] + gs[3 * r + 1:3 * r + 2] * o_s[:, cols]
             + gs[3 * r + 2:3 * r + 3] * o_w[:, cols])
        o_ref[:, r * DV:(r + 1) * DV] = o.T.astype(BF16)


def _nsa_attention(proj, k_c, v_c, gq, gks, gkw, tabs, ovt, B, T, tq=128, tk=512):
    nq = T // tq
    G, R = NSA_GROUPS, NSA_HPG
    ncp = k_c.shape[2]
    c, s = tabs
    qtab = pl.BlockSpec((tq, LANES), lambda b, g, i: (i, 0))
    ktab = pl.BlockSpec((T, LANES), lambda b, g, i: (0, 0))
    gspec = pl.BlockSpec((1, HEAD_PAD), lambda b, g, i: (0, 0))
    return pl.pallas_call(
        functools.partial(_nsa_kernel, tq=tq, tk=tk, T=T, scale=NSA_DK ** -0.5),
        grid=(B, G, nq),
        in_specs=[pl.BlockSpec((tq, R * HEAD_PAD), lambda b, g, i: (b * nq + i, g)),
                  pl.BlockSpec((T, HEAD_PAD), lambda b, g, i: (b, C_KS // HEAD_PAD + g)),
                  pl.BlockSpec((T, HEAD_PAD), lambda b, g, i: (b, C_KW // HEAD_PAD + g)),
                  pl.BlockSpec((T, DV), lambda b, g, i: (b, C_VS // DV + g)),
                  pl.BlockSpec((T, DV), lambda b, g, i: (b, C_VW // DV + g)),
                  pl.BlockSpec((tq, LANES), lambda b, g, i: (b * nq + i, C_GATE // LANES + g)),
                  pl.BlockSpec((1, 1, ncp, HEAD_PAD), lambda b, g, i: (b, g, 0, 0)),
                  pl.BlockSpec((1, 1, ncp, DV), lambda b, g, i: (b, g, 0, 0)),
                  gspec, gspec, gspec, qtab, qtab, ktab, ktab,
                  pl.BlockSpec(ovt.shape, lambda b, g, i: (0, 0))],
        out_specs=pl.BlockSpec((tq, R * DV), lambda b, g, i: (b * nq + i, g)),
        out_shape=jax.ShapeDtypeStruct((B * T, NSA_HEADS * DV), BF16),
        scratch_shapes=[pltpu.VMEM((T, HEAD_PAD), BF16), pltpu.VMEM((WINDOW + T, HEAD_PAD), BF16),
                        pltpu.VMEM((DV_EXT, T), BF16), pltpu.VMEM((DV_EXT, WINDOW + T), BF16),
                        pltpu.VMEM((DV, ncp), BF16), pltpu.VMEM((R * tq, HEAD_PAD), BF16)]
        + _flash_scratch(2, tk, R * tq // 2),
        compiler_params=_params(("parallel", "parallel", "arbitrary")),
        name="nsa_attention",
    )(proj, proj, proj, proj, proj, proj, k_c, v_c, gq, gks, gkw, c, s, c, s, ovt)


def _pad_cols(w, n):
    return jnp.pad(w, ((0, 0), (0, n - w.shape[1])))


def _pad_heads(w, nh, d, dp):
    k = w.shape[0]
    return jnp.pad(w.reshape(k, nh, d), ((0, 0), (0, 0), (0, dp - d))).reshape(k, nh * dp)


def _rope_split(x1, x2, rest_a, rest_b):
    k, nh = x1.shape[:2]
    za = jnp.zeros((k, nh, HALF_LANES - x1.shape[2] - rest_a.shape[2]), x1.dtype)
    zb = jnp.zeros((k, nh, HALF_LANES - x2.shape[2] - rest_b.shape[2]), x1.dtype)
    return jnp.concatenate([x1, rest_a, za, x2, rest_b, zb], axis=2)


def _nsa_head_layout(w, nh):
    k = w.shape[0]
    w = w.reshape(k, nh, NSA_DK)
    h = NSA_ROPE // 2
    fill = HALF_LANES - h
    first = _rope_split(w[..., :h], w[..., h:2 * h], w[..., 2 * h:2 * h + fill], w[..., 2 * h + fill:LANES])
    second = jnp.pad(w[..., LANES:], ((0, 0), (0, 0), (0, HEAD_PAD - NSA_DK)))
    return jnp.concatenate([first, second], axis=2).reshape(k, nh * HEAD_PAD)


def _mla_rope_layout(w, nh):
    k = w.shape[0]
    w = w.reshape(k, nh, MLA_ROPE)
    h = MLA_ROPE // 2
    e = w[..., :0]
    return _rope_split(w[..., :h], w[..., h:], e, e).reshape(k, nh * LANES)


def _mla_head_layout(w, nh):
    k = w.shape[0]
    w = w.reshape(k, nh, MLA_QK)
    rope = _mla_rope_layout(w[..., MLA_NOPE:].reshape(k, nh * MLA_ROPE), nh).reshape(k, nh, LANES)
    return jnp.concatenate([w[..., :MLA_NOPE], rope], axis=2).reshape(k, nh * HEAD_PAD)


def _layout_w_in(w_in):
    sizes = (MLA_RANK, MLA_RANK, MLA_ROPE, NSA_HEADS * NSA_DK,
             NSA_GROUPS * NSA_DK, NSA_GROUPS * NSA_DV, NSA_GROUPS * NSA_DK, NSA_GROUPS * NSA_DV,
             NSA_GROUPS * NSA_DK, NSA_GROUPS * NSA_DV, NSA_HEADS * 3, 2 * D_MODEL)
    offs, o = [], 0
    for s in sizes:
        offs.append((o, o + s))
        o += s
    cq, ckv, kr, qn, kc, vc, ks, vs, kw, vw, gn, gm = (w_in[:, a:b] for a, b in offs)
    w = jnp.concatenate([
        _nsa_head_layout(qn, NSA_HEADS), gm,
        _nsa_head_layout(ks, NSA_GROUPS), _nsa_head_layout(kw, NSA_GROUPS), vs, vw,
        cq, ckv, _pad_cols(_mla_rope_layout(kr, 1), HEAD_PAD),
        _pad_heads(gn, NSA_GROUPS, NSA_HPG * 3, LANES), kc, vc], axis=1)
    assert w.shape[1] == C_END
    return w.astype(BF16)


def _rope_tables(pos, rot_dim):
    half = rot_dim // 2
    inv = ROPE_THETA ** (-jnp.arange(0, rot_dim, 2, dtype=F32) / rot_dim)
    ang = pos.astype(F32)[:, None] * inv[None, :]
    cos, sin = jnp.cos(ang), jnp.sin(ang)
    n = pos.shape[0]
    c = jnp.ones((n, LANES), F32).at[:, :half].set(cos).at[:, HALF_LANES:HALF_LANES + half].set(cos)
    s = jnp.zeros((n, LANES), F32).at[:, :half].set(-sin).at[:, HALF_LANES:HALF_LANES + half].set(sin)
    return c, s


def _overlap_table(ncp, nb):
    cs = jnp.arange(ncp)[None, :] * CMP_STRIDE
    ss = jnp.arange(nb)[:, None] * SLC_LEN
    ov = jnp.clip(jnp.minimum(cs + CMP_LEN, ss + SLC_LEN) - jnp.maximum(cs, ss), 0, None).astype(F32) / CMP_LEN
    return ov.astype(BF16)


def _layer(x, attn_norm_g, w_in, mla_q_lat_g, mla_kv_lat_g, mla_w_uq, mla_w_uk, mla_w_uv,
           mla_q_norm_g, mla_k_norm_g, nsa_q_norm_g, nsa_k_norm_g, cmp_pe_k, cmp_pe_v,
           cmp_k_w1, cmp_k_b1, cmp_k_w2, cmp_k_b2, cmp_v_w1, cmp_v_b1, cmp_v_w2, cmp_v_b2,
           w_proj_mla, w_proj_nsa, w_out, ffn_norm_g, w_ffn_gate, w_ffn_up, w_ffn_down):
    B, T, _ = x.shape
    n = B * T
    ncp = T // CMP_STRIDE
    nb = T // SLC_LEN
    assert T % 512 == 0 and ncp % LANES == 0 and nb % 8 == 0 and SEL_LANE0 + nb <= PAD_LANE and T >= WINDOW
    x2 = x.reshape(n, D_MODEL)
    pos = jnp.arange(T)

    proj = _norm_matmul(x2, 0, D_MODEL, attn_norm_g.reshape(1, -1), _layout_w_in(w_in), 512, 1024, "in_proj")

    q_up = _norm_matmul(proj, C_CQ // MLA_RANK, MLA_RANK, mla_q_lat_g.reshape(1, -1),
                        _mla_head_layout(mla_w_uq, MLA_HEADS).astype(BF16), 512, 1024, "mla_q_up")
    kv_up = _norm_matmul(proj, C_CKV // MLA_RANK, MLA_RANK, mla_kv_lat_g.reshape(1, -1),
                         jnp.concatenate([mla_w_uk, mla_w_uv], axis=1).astype(BF16), 512, 1024, "mla_kv_up")
    o_mla = _mla_attention(q_up, kv_up, proj, _mla_head_layout(mla_q_norm_g[None], 1),
                           _mla_head_layout(mla_k_norm_g[None], 1), _rope_tables(pos, MLA_ROPE), B, T)

    def chunks(c0, d):
        t = proj[:, c0:c0 + NSA_GROUPS * d].reshape(B, ncp, CMP_STRIDE, NSA_GROUPS, d)
        return t.transpose(0, 3, 1, 2, 4).reshape(B, NSA_GROUPS, ncp, CMP_STRIDE * d)

    cmp_tabs = _rope_tables(jnp.arange(ncp) * CMP_STRIDE + CMP_LEN - 1, NSA_ROPE)
    k_c = _compress(chunks(C_KC, NSA_DK), cmp_pe_k, cmp_k_w1, cmp_k_b1, _nsa_head_layout(cmp_k_w2, 1),
                    _nsa_head_layout(cmp_k_b2[None], 1)[0], _nsa_head_layout(nsa_k_norm_g[0:1], 1), cmp_tabs,
                    True, "compress_k")
    v_c = _compress(chunks(C_VC, NSA_DV), cmp_pe_v, cmp_v_w1, cmp_v_b1, cmp_v_w2, cmp_v_b2,
                    jnp.zeros((1, NSA_DV), F32), cmp_tabs, False, "compress_v")
    o_nsa = _nsa_attention(proj, k_c, v_c, _nsa_head_layout(nsa_q_norm_g[None], 1),
                           _nsa_head_layout(nsa_k_norm_g[1:2], 1), _nsa_head_layout(nsa_k_norm_g[2:3], 1),
                           _rope_tables(pos, NSA_ROPE), _overlap_table(ncp, nb), B, T)

    u = _merge(o_mla, o_nsa, w_proj_mla.astype(BF16), w_proj_nsa.astype(BF16), proj)
    h, hn = _out_proj(u, w_out.astype(BF16), x2, ffn_norm_g.reshape(1, -1))
    act = _ffn_up(hn, w_ffn_gate.astype(BF16), w_ffn_up.astype(BF16))
    out = _ffn_down(act, w_ffn_down.astype(BF16), h)
    return out.reshape(B, T, D_MODEL)


def kernel(x, attn_norm_g, w_in, mla_q_lat_g, mla_kv_lat_g, mla_w_uq, mla_w_uk, mla_w_uv, mla_q_norm_g, mla_k_norm_g, nsa_q_norm_g, nsa_k_norm_g, cmp_pe_k, cmp_pe_v, cmp_k_w1, cmp_k_b1, cmp_k_w2, cmp_k_b2, cmp_v_w1, cmp_v_b1, cmp_v_w2, cmp_v_b2, w_proj_mla, w_proj_nsa, w_out, ffn_norm_g, w_ffn_gate, w_ffn_up, w_ffn_down):
    params = (attn_norm_g, w_in, mla_q_lat_g, mla_kv_lat_g, mla_w_uq, mla_w_uk, mla_w_uv, mla_q_norm_g,
              mla_k_norm_g, nsa_q_norm_g, nsa_k_norm_g, cmp_pe_k, cmp_pe_v, cmp_k_w1, cmp_k_b1, cmp_k_w2,
              cmp_k_b2, cmp_v_w1, cmp_v_b1, cmp_v_w2, cmp_v_b2, w_proj_mla, w_proj_nsa, w_out, ffn_norm_g,
              w_ffn_gate, w_ffn_up, w_ffn_down)
    h = x
    for l in range(attn_norm_g.shape[0]):
        h = _layer(h, *(p[l] for p in params))
    return h
```

```python
import functools

import jax
import jax.numpy as jnp
from jax import lax
from jax.experimental import pallas as pl
from jax.experimental.pallas import tpu as pltpu

F32, BF16, I32 = jnp.float32, jnp.bfloat16, jnp.int32

EPS = 1e-6
NEG_INF = -1e30
FORCE = 1e9
ROPE_THETA = 500000.0

D_MODEL = 2048
MLA_HEADS, MLA_NOPE, MLA_ROPE, MLA_V = 16, 128, 64, 128
MLA_QK = MLA_NOPE + MLA_ROPE
MLA_RANK = 512
NSA_HEADS, NSA_GROUPS, NSA_DK, NSA_DV = 16, 4, 192, 128
NSA_HPG = NSA_HEADS // NSA_GROUPS
NSA_ROPE = NSA_DK // 4
CMP_LEN, CMP_STRIDE = 32, 16
SLC_LEN, SLC_TOPK, WINDOW = 64, 16, 512
D_FF = 5632

LANES = 128
HALF_LANES = 64
HEAD_PAD = 256
SEL_LANE0 = 64
PAD_LANE = 127
DV = 128
DV_EXT = DV + 16
VMEM_LIMIT = 48 * 1024 * 1024

C_QN, C_GM, C_KS, C_KW, C_VS, C_VW = 0, 4096, 8192, 9216, 10240, 10752
C_CQ, C_CKV, C_KR, C_GATE, C_KC, C_VC, C_END = 11264, 11776, 12288, 12416, 12544, 13568, 14336
GATE_PITCH = 16

_NT = (((1,), (1,)), ((), ()))


def _params(sem):
    return pltpu.CompilerParams(dimension_semantics=sem, vmem_limit_bytes=VMEM_LIMIT)


def _norm_matmul_kernel(x_ref, g_ref, w_ref, o_ref, xn_ref):
    @pl.when(pl.program_id(1) == 0)
    def _():
        x = x_ref[...]
        ms = jnp.mean(x * x, axis=-1, keepdims=True)
        xn_ref[...] = (x * lax.rsqrt(ms + EPS) * g_ref[...]).astype(BF16)

    o_ref[...] = jnp.dot(xn_ref[...], w_ref[...], preferred_element_type=F32).astype(o_ref.dtype)


def _norm_matmul(x, xcol, k, g, w, tm, tn, name):
    n, m = x.shape[0], w.shape[1]
    return pl.pallas_call(
        _norm_matmul_kernel,
        grid=(n // tm, m // tn),
        in_specs=[pl.BlockSpec((tm, k), lambda i, j: (i, xcol)),
                  pl.BlockSpec((1, k), lambda i, j: (0, 0)),
                  pl.BlockSpec((k, tn), lambda i, j: (0, j))],
        out_specs=pl.BlockSpec((tm, tn), lambda i, j: (i, j)),
        out_shape=jax.ShapeDtypeStruct((n, m), F32),
        scratch_shapes=[pltpu.VMEM((tm, k), BF16)],
        compiler_params=_params(("parallel", "arbitrary")),
        name=name,
    )(x, g, w)


def _merge_kernel(om_ref, on_ref, wm_ref, wn_ref, g0_ref, g1_ref, o_ref):
    ym = jnp.dot(om_ref[...], wm_ref[...], preferred_element_type=F32)
    yn = jnp.dot(on_ref[...], wn_ref[...], preferred_element_type=F32)
    o_ref[...] = (jax.nn.sigmoid(g0_ref[...]) * ym + jax.nn.sigmoid(g1_ref[...]) * yn).astype(BF16)


def _merge(o_mla, o_nsa, w_pm, w_pn, proj, tm=1024, tn=512):
    n = o_mla.shape[0]
    g0, g1 = C_GM // tn, (C_GM + D_MODEL) // tn
    return pl.pallas_call(
        _merge_kernel,
        grid=(n // tm, D_MODEL // tn),
        in_specs=[pl.BlockSpec((tm, D_MODEL), lambda i, j: (i, 0)),
                  pl.BlockSpec((tm, D_MODEL), lambda i, j: (i, 0)),
                  pl.BlockSpec((D_MODEL, tn), lambda i, j: (0, j)),
                  pl.BlockSpec((D_MODEL, tn), lambda i, j: (0, j)),
                  pl.BlockSpec((tm, tn), lambda i, j: (i, g0 + j)),
                  pl.BlockSpec((tm, tn), lambda i, j: (i, g1 + j))],
        out_specs=pl.BlockSpec((tm, tn), lambda i, j: (i, j)),
        out_shape=jax.ShapeDtypeStruct((n, D_MODEL), BF16),
        compiler_params=_params(("parallel", "arbitrary")),
        name="merge",
    )(o_mla, o_nsa, w_pm, w_pn, proj, proj)


def _out_proj_kernel(u_ref, w_ref, x_ref, g_ref, h_ref, hn_ref):
    h = x_ref[...] + jnp.dot(u_ref[...], w_ref[...], preferred_element_type=F32)
    h_ref[...] = h
    ms = jnp.mean(h * h, axis=-1, keepdims=True)
    hn_ref[...] = (h * lax.rsqrt(ms + EPS) * g_ref[...]).astype(BF16)


def _out_proj(u, w_out, x, g, tm=256):
    n = u.shape[0]
    row = pl.BlockSpec((tm, D_MODEL), lambda i: (i, 0))
    return pl.pallas_call(
        _out_proj_kernel,
        grid=(n // tm,),
        in_specs=[row, pl.BlockSpec((D_MODEL, D_MODEL), lambda i: (0, 0)), row,
                  pl.BlockSpec((1, D_MODEL), lambda i: (0, 0))],
        out_specs=[row, row],
        out_shape=[jax.ShapeDtypeStruct((n, D_MODEL), F32), jax.ShapeDtypeStruct((n, D_MODEL), BF16)],
        compiler_params=_params(("parallel",)),
        name="out_proj",
    )(u, w_out, x, g)


def _ffn_up_kernel(hn_ref, wg_ref, wu_ref, o_ref):
    hn = hn_ref[...]
    a = jnp.dot(hn, wg_ref[...], preferred_element_type=F32)
    b = jnp.dot(hn, wu_ref[...], preferred_element_type=F32)
    o_ref[...] = (jax.nn.silu(a) * b).astype(BF16)


def _ffn_up(hn, w_gate, w_up, tm=1024, tn=512):
    n = hn.shape[0]
    return pl.pallas_call(
        _ffn_up_kernel,
        grid=(n // tm, D_FF // tn),
        in_specs=[pl.BlockSpec((tm, D_MODEL), lambda i, j: (i, 0)),
                  pl.BlockSpec((D_MODEL, tn), lambda i, j: (0, j)),
                  pl.BlockSpec((D_MODEL, tn), lambda i, j: (0, j))],
        out_specs=pl.BlockSpec((tm, tn), lambda i, j: (i, j)),
        out_shape=jax.ShapeDtypeStruct((n, D_FF), BF16),
        compiler_params=_params(("parallel", "arbitrary")),
        name="ffn_up",
    )(hn, w_gate, w_up)


def _ffn_down_kernel(a_ref, w_ref, h_ref, o_ref):
    o_ref[...] = h_ref[...] + jnp.dot(a_ref[...], w_ref[...], preferred_element_type=F32)


def _ffn_down(act, w_down, h, tm=1024, tn=256):
    n = act.shape[0]
    return pl.pallas_call(
        _ffn_down_kernel,
        grid=(n // tm, D_MODEL // tn),
        in_specs=[pl.BlockSpec((tm, D_FF), lambda i, j: (i, 0)),
                  pl.BlockSpec((D_FF, tn), lambda i, j: (0, j)),
                  pl.BlockSpec((tm, tn), lambda i, j: (i, j))],
        out_specs=pl.BlockSpec((tm, tn), lambda i, j: (i, j)),
        out_shape=jax.ShapeDtypeStruct((n, D_MODEL), F32),
        compiler_params=_params(("parallel", "arbitrary")),
        name="ffn_down",
    )(act, w_down, h)


def _rms_inv(xa, xb, d):
    return lax.rsqrt(jnp.sum(xa * xa + xb * xb, axis=-1, keepdims=True) * (1.0 / d) + EPS)


def _rope(x, c, s):
    return x * c + pltpu.roll(x, HALF_LANES, 1) * s


def _col_max(s, slabs=8):
    rows = s.shape[0]
    if rows % (8 * slabs) == 0:
        s = jnp.max(s.reshape(slabs, rows // slabs, s.shape[1]), axis=0)
    return jnp.max(s, axis=0, keepdims=True)


def _col_sum(s, slabs=8):
    rows = s.shape[0]
    if rows % (8 * slabs) == 0:
        s = jnp.sum(s.reshape(slabs, rows // slabs, s.shape[1]), axis=0)
    return jnp.sum(s, axis=0, keepdims=True)


def _flash_t(chains, n_last, n_max, tk, s_buf0, s_buf1, mt_buf0, mt_buf1, m_ref, acc_ref):
    nc = len(chains)
    s_buf, mt_buf = (s_buf0, s_buf1), (mt_buf0, mt_buf1)

    def stage_a(j):
        off = j * tk
        for c, (q_at, k_at, _, t_cols) in enumerate(chains):
            s = lax.dot_general(k_at(off), q_at(), _NT, preferred_element_type=F32)
            pk = off + lax.broadcasted_iota(I32, (tk, 1), 0)
            s = jnp.where(pk <= t_cols, s, NEG_INF)
            s_buf[j % 2][c] = s
            mt_buf[j % 2][c] = _col_max(s)

    def stage_b(j):
        off = j * tk
        for c, (_, _, vt_at, _) in enumerate(chains):
            m = m_ref[c]
            m_new = jnp.maximum(m, mt_buf[j % 2][c])
            p = jnp.exp(s_buf[j % 2][c] - m_new).astype(BF16)
            acc_ref[c] = jnp.exp(m - m_new) * acc_ref[c] + jnp.dot(vt_at(off), p, preferred_element_type=F32)
            m_ref[c] = m_new

    m_ref[...] = jnp.full(m_ref.shape, NEG_INF, F32)
    acc_ref[...] = jnp.zeros(acc_ref.shape, F32)
    stage_a(0)
    for j in range(n_max):
        if j + 1 < n_max:
            @pl.when(j < n_last)
            def _(j=j):
                stage_a(j + 1)
                stage_b(j)

        @pl.when(j == n_last)
        def _(j=j):
            stage_b(j)

    return [acc_ref[c, :DV] * (1.0 / acc_ref[c, DV:DV + 1]) for c in range(nc)]


def _flash_scratch(nc, tk, nq):
    return [pltpu.VMEM((nc, tk, nq), F32), pltpu.VMEM((nc, tk, nq), F32),
            pltpu.VMEM((nc, 1, nq), F32), pltpu.VMEM((nc, 1, nq), F32),
            pltpu.VMEM((nc, 1, nq), F32), pltpu.VMEM((nc, DV_EXT, nq), F32)]


def _mla_kernel(q_ref, kn_ref, kr_ref, v_ref, gq_ref, gk_ref, cq_ref, sq_ref, ck_ref, sk_ref,
                o_ref, k_s, vt_s, q_s, *flash_scratch, tq, tk, scale):
    qi = pl.program_id(2)
    T = kn_ref.shape[0]

    @pl.when(qi == 0)
    def _():
        kr = kr_ref[...]
        ss_r = jnp.sum(kr * kr, axis=-1, keepdims=True)
        for h in range(2):
            kn = kn_ref[:, h * LANES:(h + 1) * LANES]
            r = lax.rsqrt((jnp.sum(kn * kn, axis=-1, keepdims=True) + ss_r) * (1.0 / MLA_QK) + EPS)
            k_s[h, :, :LANES] = (kn * r * gk_ref[:, :LANES]).astype(BF16)
            k_s[h, :, LANES:] = _rope(kr * r * gk_ref[:, LANES:], ck_ref[...], sk_ref[...]).astype(BF16)
            vt_s[h, :DV, :] = v_ref[:, h * LANES:(h + 1) * LANES].T.astype(BF16)
            vt_s[h, DV:, :] = jnp.ones((DV_EXT - DV, T), BF16)

    t0 = qi * tq
    t_cols = t0 + lax.broadcasted_iota(I32, (1, tq), 1)
    chains = []
    for h in range(2):
        qa = q_ref[:, h * HEAD_PAD:h * HEAD_PAD + LANES]
        qb = q_ref[:, h * HEAD_PAD + LANES:(h + 1) * HEAD_PAD]
        r = _rms_inv(qa, qb, MLA_QK) * scale
        qa = qa * r * gq_ref[:, :LANES]
        qb = _rope(qb * r * gq_ref[:, LANES:], cq_ref[...], sq_ref[...])
        q_s[h, :, :LANES] = qa.astype(BF16)
        q_s[h, :, LANES:] = qb.astype(BF16)
        chains.append((lambda h=h: q_s[h], lambda off, h=h: k_s[h, pl.ds(off, tk), :],
                       lambda off, h=h: vt_s[h, :, pl.ds(off, tk)], t_cols))
    for h, o_t in enumerate(_flash_t(chains, t0 // tk, T // tk, tk, *flash_scratch)):
        o_ref[:, h * LANES:(h + 1) * LANES] = o_t.T.astype(BF16)


def _mla_attention(q_up, kv_up, proj, gq, gk, tabs, B, T, tq=256, tk=512):
    nq = T // tq
    c, s = tabs
    qtab = pl.BlockSpec((tq, LANES), lambda b, h, i: (i, 0))
    ktab = pl.BlockSpec((T, LANES), lambda b, h, i: (0, 0))
    gspec = pl.BlockSpec((1, HEAD_PAD), lambda b, h, i: (0, 0))
    return pl.pallas_call(
        functools.partial(_mla_kernel, tq=tq, tk=tk, scale=MLA_QK ** -0.5),
        grid=(B, MLA_HEADS // 2, nq),
        in_specs=[pl.BlockSpec((tq, 2 * HEAD_PAD), lambda b, h, i: (b * nq + i, h)),
                  pl.BlockSpec((T, 2 * LANES), lambda b, h, i: (b, h)),
                  pl.BlockSpec((T, LANES), lambda b, h, i: (b, C_KR // LANES)),
                  pl.BlockSpec((T, 2 * LANES), lambda b, h, i: (b, MLA_HEADS // 2 + h)),
                  gspec, gspec, qtab, qtab, ktab, ktab],
        out_specs=pl.BlockSpec((tq, 2 * LANES), lambda b, h, i: (b * nq + i, h)),
        out_shape=jax.ShapeDtypeStruct((B * T, MLA_HEADS * MLA_V), BF16),
        scratch_shapes=[pltpu.VMEM((2, T, HEAD_PAD), BF16), pltpu.VMEM((2, DV_EXT, T), BF16),
                        pltpu.VMEM((2, tq, HEAD_PAD), BF16)] + _flash_scratch(2, tk, tq),
        compiler_params=_params(("parallel", "parallel", "arbitrary")),
        name="mla_attention",
    )(q_up, kv_up, proj, kv_up, gq, gk, c, s, c, s)


def _compress_kernel(*refs, is_key, gb, ncp, nt):
    x_tiles = refs[:gb * nt]
    pe_ref, w1a_ref, w1b_ref, b1_ref, w2_ref, b2_ref, g_ref, c_ref, s_ref, o_ref, xa_s, xb_s = refs[gb * nt:]
    rows = gb * ncp
    for g in range(gb):
        for l in range(CMP_STRIDE):
            for t in range(nt):
                x = x_tiles[g * nt + t][pl.ds(l, ncp, stride=CMP_STRIDE), :]
                lanes = slice((l * nt + t) * LANES, (l * nt + t + 1) * LANES)
                pe_lanes = slice(t * LANES, (t + 1) * LANES)
                xa_s[g * ncp:(g + 1) * ncp, lanes] = (x + pe_ref[l:l + 1, pe_lanes]).astype(BF16)
                xb_s[g * ncp:(g + 1) * ncp, lanes] = (
                    x + pe_ref[CMP_STRIDE + l:CMP_STRIDE + l + 1, pe_lanes]).astype(BF16)
    pa = jnp.dot(xa_s[...], w1a_ref[...], preferred_element_type=F32)
    pb = jnp.dot(xb_s[...], w1b_ref[...], preferred_element_type=F32)
    h = pa + pltpu.roll(pb, rows - 1, 0) + b1_ref[...]
    y = jnp.dot(jax.nn.gelu(h).astype(BF16), w2_ref[...], preferred_element_type=F32) + b2_ref[...]
    if is_key:
        ya, yb = y[:, :LANES], y[:, LANES:]
        r = _rms_inv(ya, yb, NSA_DK)
        c = jnp.concatenate([c_ref[...]] * gb, axis=0)
        s = jnp.concatenate([s_ref[...]] * gb, axis=0)
        ya = _rope(ya * r * g_ref[:, :LANES], c, s)
        yb = yb * r * g_ref[:, LANES:]
        y = jnp.concatenate([ya, yb], axis=1)
    o_ref[0] = y.astype(BF16).reshape(gb, ncp, y.shape[-1])


def _compress(proj, c0, d, B, T, pe, w1, b1, w2, b2, g, tabs, is_key, name, gb=2):
    G = NSA_GROUPS
    ncp = T // CMP_STRIDE
    nt = -(-d // LANES)
    dp = nt * LANES
    kd = CMP_STRIDE * dp
    dout = w2.shape[1]
    hid = w1.shape[1]
    w1 = jnp.pad(w1.reshape(CMP_LEN, d, hid), ((0, 0), (0, dp - d), (0, 0))).reshape(CMP_LEN * dp, hid).astype(BF16)
    pe = _pad_cols(pe, dp)
    c, s = tabs
    const = lambda shape: pl.BlockSpec(shape, lambda b, gi: (0,) * len(shape))
    tile = lambda k: pl.BlockSpec((T, LANES), lambda b, gi: (b, c0 // LANES + gi * gb * nt + k))
    return pl.pallas_call(
        functools.partial(_compress_kernel, is_key=is_key, gb=gb, ncp=ncp, nt=nt),
        grid=(B, G // gb),
        in_specs=[tile(k) for k in range(gb * nt)] + [
                  const((CMP_LEN, dp)), const((kd, hid)), const((kd, hid)), const((1, hid)),
                  const((hid, dout)), const((1, dout)), const((1, dout)),
                  const((ncp, LANES)), const((ncp, LANES))],
        out_specs=pl.BlockSpec((1, gb, ncp, dout), lambda b, gi: (b, gi, 0, 0)),
        out_shape=jax.ShapeDtypeStruct((B, G, ncp, dout), BF16),
        scratch_shapes=[pltpu.VMEM((gb * ncp, kd), BF16), pltpu.VMEM((gb * ncp, kd), BF16)],
        compiler_params=_params(("parallel", "parallel")),
        name=name,
    )(*([proj] * (gb * nt)), pe, w1[:kd], w1[kd:], b1.reshape(1, hid), w2.astype(BF16), b2.reshape(1, dout),
      g, c, s)


def _nsa_kernel(q_ref, ks_ref, kw_ref, vs_ref, vw_ref, gate_ref, kc_ref, vc_ref, gq_ref, gks_ref, gkw_ref,
                cq_ref, sq_ref, ck_ref, sk_ref, ovt_ref,
                o_ref, ks_s, kw_s, vst_s, vwt_s, vct_s, q_s, g_s, *flash_scratch, tq, tk, T, scale):
    qi = pl.program_id(2)
    R = NSA_HPG
    nq = R * tq
    nb = T // SLC_LEN
    topk = min(SLC_TOPK, nb)
    ncp = kc_ref.shape[2]

    @pl.when(qi == 0)
    def _():
        def prep_k(k_ref, g_ref):
            ka, kb = k_ref[:, :LANES], k_ref[:, LANES:]
            r = _rms_inv(ka, kb, NSA_DK)
            return _rope(ka * r * g_ref[:, :LANES], ck_ref[...], sk_ref[...]), kb * r * g_ref[:, LANES:]

        ka, kb = prep_k(ks_ref, gks_ref)
        blk = lax.broadcasted_iota(I32, (T, LANES), 0) // SLC_LEN
        lane = lax.broadcasted_iota(I32, (T, LANES), 1)
        ks_s[:, :LANES] = ka.astype(BF16)
        ks_s[:, LANES:] = (kb + (lane - SEL_LANE0 == blk).astype(F32)).astype(BF16)
        ka, kb = prep_k(kw_ref, gkw_ref)
        lane_w = lax.broadcasted_iota(I32, (WINDOW, LANES), 1)
        kw_s[:WINDOW, :LANES] = jnp.zeros((WINDOW, LANES), BF16)
        kw_s[:WINDOW, LANES:] = jnp.where(lane_w == PAD_LANE, NEG_INF, 0.0).astype(BF16)
        kw_s[WINDOW:, :LANES] = ka.astype(BF16)
        kw_s[WINDOW:, LANES:] = kb.astype(BF16)
        vst_s[:DV, :] = vs_ref[...].T.astype(BF16)
        vst_s[DV:, :] = jnp.ones((DV_EXT - DV, T), BF16)
        vwt_s[:DV, :WINDOW] = jnp.zeros((DV, WINDOW), BF16)
        vwt_s[:DV, WINDOW:] = vw_ref[...].T.astype(BF16)
        vwt_s[DV:, :] = jnp.ones((DV_EXT - DV, WINDOW + T), BF16)
        vct_s[...] = vc_ref[0, 0].astype(F32).T.astype(BF16)

    t0 = pl.multiple_of(qi * tq, tq)
    lane_q = lax.broadcasted_iota(I32, (tq, LANES), 1)
    for r in range(R):
        qa = q_ref[:, r * HEAD_PAD:r * HEAD_PAD + LANES]
        qb = q_ref[:, r * HEAD_PAD + LANES:(r + 1) * HEAD_PAD]
        rs = _rms_inv(qa, qb, NSA_DK) * scale
        qa = _rope(qa * rs * gq_ref[:, :LANES], cq_ref[...], sq_ref[...])
        qb = jnp.where(lane_q == PAD_LANE, 1.0, qb * rs * gq_ref[:, LANES:])
        q_s[r * tq:(r + 1) * tq, :LANES] = qa.astype(BF16)
        q_s[r * tq:(r + 1) * tq, LANES:] = qb.astype(BF16)

    c_q = lax.broadcasted_iota(I32, (1, nq), 1) & (tq - 1)
    t_cols = t0 + c_q
    q = q_s[...]

    span = tq + WINDOW
    sw = lax.dot_general(kw_s[pl.ds(t0, span), :], q, _NT, preferred_element_type=F32)
    i_k = lax.broadcasted_iota(I32, (tq, 1), 0)
    s_lo = jnp.where(i_k > c_q, sw[:tq], NEG_INF)
    s_mid = sw[tq:WINDOW]
    s_hi = jnp.where(i_k <= c_q, sw[WINDOW:], NEG_INF)
    m = jnp.maximum(jnp.maximum(_col_max(s_lo), _col_max(s_mid)), _col_max(s_hi))
    pw = jnp.concatenate([jnp.exp(s_lo - m), jnp.exp(s_mid - m), jnp.exp(s_hi - m)], axis=0).astype(BF16)
    acc = jnp.dot(vwt_s[:, pl.ds(t0, span)], pw, preferred_element_type=F32)
    o_w = acc[:DV] * (1.0 / acc[DV:DV + 1])

    sc = lax.dot_general(kc_ref[0, 0], q, _NT, preferred_element_type=F32)
    cend = lax.broadcasted_iota(I32, (ncp, 1), 0) * CMP_STRIDE + (CMP_LEN - 1)
    mask_c = cend <= t_cols
    sc = jnp.where(mask_c, sc, NEG_INF)
    e = jnp.where(mask_c, jnp.exp(sc - _col_max(sc)), 0.0)
    p = e * (1.0 / jnp.maximum(_col_sum(e), 1e-30))
    o_c = jnp.dot(vct_s[...], p.astype(BF16), preferred_element_type=F32)

    ps = p[:, 0:tq] + p[:, tq:2 * tq] + p[:, 2 * tq:3 * tq] + p[:, 3 * tq:4 * tq]
    ps_hi = ps.astype(BF16)
    ps_lo = (ps - ps_hi.astype(F32)).astype(BF16)
    imp = (jnp.dot(ovt_ref[...], ps_hi, preferred_element_type=F32)
           + jnp.dot(ovt_ref[...], ps_lo, preferred_element_type=F32))
    jb = lax.broadcasted_iota(I32, (nb, tq), 0)
    tb = (t0 + lax.broadcasted_iota(I32, (1, tq), 1)) // SLC_LEN
    forced = (jb == 0) | (jb == tb) | (jb == tb - 1)
    score = jnp.where(forced, FORCE, jnp.where(jb <= tb, imp, -FORCE))
    rank = jnp.zeros((nb, tq), I32)
    for j in range(nb):
        row = score[j:j + 1, :]
        rank = rank + ((row > score) | ((row == score) & (j < jb))).astype(I32)
    unsel = (rank >= topk).astype(F32)
    unsel = jnp.concatenate([jnp.zeros((SEL_LANE0, tq), F32), unsel,
                             jnp.zeros((LANES - SEL_LANE0 - nb, tq), F32)], axis=0)
    bias = unsel.T * NEG_INF

    for r in range(R):
        qb = q_s[r * tq:(r + 1) * tq, LANES:].astype(F32)
        q_s[r * tq:(r + 1) * tq, LANES:] = (qb + bias).astype(BF16)
    half = nq // 2
    k_at = lambda off: ks_s[pl.ds(off, tk), :]
    vt_at = lambda off: vst_s[:, pl.ds(off, tk)]
    o_s = jnp.concatenate(_flash_t([(lambda: q_s[:half], k_at, vt_at, t_cols[:, :half]),
                                    (lambda: q_s[half:], k_at, vt_at, t_cols[:, half:])],
                                   t0 // tk, T // tk, tk, *flash_scratch), axis=1)

    g_s[...] = jax.nn.sigmoid(gate_ref[...]).T
    gs = g_s[pl.ds(pl.multiple_of(pl.program_id(1) * GATE_PITCH, GATE_PITCH), GATE_PITCH), :]
    for r in range(R):
        cols = slice(r * tq, (r + 1) * tq)
        o = (gs[3 * r:3 * r + 1] * o_c[:, cols] + gs[3 * r + 1:3 * r + 2] * o_s[:, cols]
             + gs[3 * r + 2:3 * r + 3] * o_w[:, cols])
        o_ref[:, r * DV:(r + 1) * DV] = o.T.astype(BF16)


def _nsa_attention(proj, k_c, v_c, gq, gks, gkw, tabs, ovt, B, T, tq=128, tk=512):
    nq = T // tq
    G, R = NSA_GROUPS, NSA_HPG
    ncp = k_c.shape[2]
    c, s = tabs
    qtab = pl.BlockSpec((tq, LANES), lambda b, g, i: (i, 0))
    ktab = pl.BlockSpec((T, LANES), lambda b, g, i: (0, 0))
    gspec = pl.BlockSpec((1, HEAD_PAD), lambda b, g, i: (0, 0))
    return pl.pallas_call(
        functools.partial(_nsa_kernel, tq=tq, tk=tk, T=T, scale=NSA_DK ** -0.5),
        grid=(B, G, nq),
        in_specs=[pl.BlockSpec((tq, R * HEAD_PAD), lambda b, g, i: (b * nq + i, g)),
                  pl.BlockSpec((T, HEAD_PAD), lambda b, g, i: (b, C_KS // HEAD_PAD + g)),
                  pl.BlockSpec((T, HEAD_PAD), lambda b, g, i: (b, C_KW // HEAD_PAD + g)),
                  pl.BlockSpec((T, DV), lambda b, g, i: (b, C_VS // DV + g)),
                  pl.BlockSpec((T, DV), lambda b, g, i: (b, C_VW // DV + g)),
                  pl.BlockSpec((tq, LANES), lambda b, g, i: (b * nq + i, C_GATE // LANES)),
                  pl.BlockSpec((1, 1, ncp, HEAD_PAD), lambda b, g, i: (b, g, 0, 0)),
                  pl.BlockSpec((1, 1, ncp, DV), lambda b, g, i: (b, g, 0, 0)),
                  gspec, gspec, gspec, qtab, qtab, ktab, ktab,
                  pl.BlockSpec(ovt.shape, lambda b, g, i: (0, 0))],
        out_specs=pl.BlockSpec((tq, R * DV), lambda b, g, i: (b * nq + i, g)),
        out_shape=jax.ShapeDtypeStruct((B * T, NSA_HEADS * DV), BF16),
        scratch_shapes=[pltpu.VMEM((T, HEAD_PAD), BF16), pltpu.VMEM((WINDOW + T, HEAD_PAD), BF16),
                        pltpu.VMEM((DV_EXT, T), BF16), pltpu.VMEM((DV_EXT, WINDOW + T), BF16),
                        pltpu.VMEM((DV, ncp), BF16), pltpu.VMEM((R * tq, HEAD_PAD), BF16),
                        pltpu.VMEM((LANES, tq), F32)]
        + _flash_scratch(2, tk, R * tq // 2),
        compiler_params=_params(("parallel", "parallel", "arbitrary")),
        name="nsa_attention",
    )(proj, proj, proj, proj, proj, proj, k_c, v_c, gq, gks, gkw, c, s, c, s, ovt)


def _pad_cols(w, n):
    return jnp.pad(w, ((0, 0), (0, n - w.shape[1])))


def _pad_heads(w, nh, d, dp):
    k = w.shape[0]
    return jnp.pad(w.reshape(k, nh, d), ((0, 0), (0, 0), (0, dp - d))).reshape(k, nh * dp)


def _rope_split(x1, x2, rest_a, rest_b):
    k, nh = x1.shape[:2]
    za = jnp.zeros((k, nh, HALF_LANES - x1.shape[2] - rest_a.shape[2]), x1.dtype)
    zb = jnp.zeros((k, nh, HALF_LANES - x2.shape[2] - rest_b.shape[2]), x1.dtype)
    return jnp.concatenate([x1, rest_a, za, x2, rest_b, zb], axis=2)


def _nsa_head_layout(w, nh):
    k = w.shape[0]
    w = w.reshape(k, nh, NSA_DK)
    h = NSA_ROPE // 2
    fill = HALF_LANES - h
    first = _rope_split(w[..., :h], w[..., h:2 * h], w[..., 2 * h:2 * h + fill], w[..., 2 * h + fill:LANES])
    second = jnp.pad(w[..., LANES:], ((0, 0), (0, 0), (0, HEAD_PAD - NSA_DK)))
    return jnp.concatenate([first, second], axis=2).reshape(k, nh * HEAD_PAD)


def _mla_rope_layout(w, nh):
    k = w.shape[0]
    w = w.reshape(k, nh, MLA_ROPE)
    h = MLA_ROPE // 2
    e = w[..., :0]
    return _rope_split(w[..., :h], w[..., h:], e, e).reshape(k, nh * LANES)


def _mla_head_layout(w, nh):
    k = w.shape[0]
    w = w.reshape(k, nh, MLA_QK)
    rope = _mla_rope_layout(w[..., MLA_NOPE:].reshape(k, nh * MLA_ROPE), nh).reshape(k, nh, LANES)
    return jnp.concatenate([w[..., :MLA_NOPE], rope], axis=2).reshape(k, nh * HEAD_PAD)


def _layout_w_in(w_in):
    sizes = (MLA_RANK, MLA_RANK, MLA_ROPE, NSA_HEADS * NSA_DK,
             NSA_GROUPS * NSA_DK, NSA_GROUPS * NSA_DV, NSA_GROUPS * NSA_DK, NSA_GROUPS * NSA_DV,
             NSA_GROUPS * NSA_DK, NSA_GROUPS * NSA_DV, NSA_HEADS * 3, 2 * D_MODEL)
    offs, o = [], 0
    for s in sizes:
        offs.append((o, o + s))
        o += s
    cq, ckv, kr, qn, kc, vc, ks, vs, kw, vw, gn, gm = (w_in[:, a:b] for a, b in offs)
    w = jnp.concatenate([
        _nsa_head_layout(qn, NSA_HEADS), gm,
        _nsa_head_layout(ks, NSA_GROUPS), _nsa_head_layout(kw, NSA_GROUPS), vs, vw,
        cq, ckv, _mla_rope_layout(kr, 1), _pad_cols(_pad_heads(gn, NSA_GROUPS, NSA_HPG * 3, GATE_PITCH), LANES),
        _pad_heads(kc, NSA_GROUPS, NSA_DK, HEAD_PAD), vc], axis=1)
    w = _pad_cols(w, C_END)
    assert w.shape[1] == C_END and C_VC + NSA_GROUPS * NSA_DV <= C_END
    return w.astype(BF16)


def _rope_tables(pos, rot_dim):
    half = rot_dim // 2
    inv = ROPE_THETA ** (-jnp.arange(0, rot_dim, 2, dtype=F32) / rot_dim)
    ang = pos.astype(F32)[:, None] * inv[None, :]
    cos, sin = jnp.cos(ang), jnp.sin(ang)
    n = pos.shape[0]
    c = jnp.ones((n, LANES), F32).at[:, :half].set(cos).at[:, HALF_LANES:HALF_LANES + half].set(cos)
    s = jnp.zeros((n, LANES), F32).at[:, :half].set(-sin).at[:, HALF_LANES:HALF_LANES + half].set(sin)
    return c, s


def _overlap_table(ncp, nb):
    cs = jnp.arange(ncp)[None, :] * CMP_STRIDE
    ss = jnp.arange(nb)[:, None] * SLC_LEN
    ov = jnp.clip(jnp.minimum(cs + CMP_LEN, ss + SLC_LEN) - jnp.maximum(cs, ss), 0, None).astype(F32) / CMP_LEN
    return ov.astype(BF16)


def _layer(x, attn_norm_g, w_in, mla_q_lat_g, mla_kv_lat_g, mla_w_uq, mla_w_uk, mla_w_uv,
           mla_q_norm_g, mla_k_norm_g, nsa_q_norm_g, nsa_k_norm_g, cmp_pe_k, cmp_pe_v,
           cmp_k_w1, cmp_k_b1, cmp_k_w2, cmp_k_b2, cmp_v_w1, cmp_v_b1, cmp_v_w2, cmp_v_b2,
           w_proj_mla, w_proj_nsa, w_out, ffn_norm_g, w_ffn_gate, w_ffn_up, w_ffn_down):
    B, T, _ = x.shape
    n = B * T
    ncp = T // CMP_STRIDE
    nb = T // SLC_LEN
    assert T % 512 == 0 and ncp % LANES == 0 and nb % 8 == 0 and SEL_LANE0 + nb <= PAD_LANE and T >= WINDOW
    x2 = x.reshape(n, D_MODEL)
    pos = jnp.arange(T)

    proj = _norm_matmul(x2, 0, D_MODEL, attn_norm_g.reshape(1, -1), _layout_w_in(w_in), 1024, 1024, "in_proj")

    q_up = _norm_matmul(proj, C_CQ // MLA_RANK, MLA_RANK, mla_q_lat_g.reshape(1, -1),
                        _mla_head_layout(mla_w_uq, MLA_HEADS).astype(BF16), 1024, 2048, "mla_q_up")
    kv_up = _norm_matmul(proj, C_CKV // MLA_RANK, MLA_RANK, mla_kv_lat_g.reshape(1, -1),
                         jnp.concatenate([mla_w_uk, mla_w_uv], axis=1).astype(BF16), 1024, 2048, "mla_kv_up")
    o_mla = _mla_attention(q_up, kv_up, proj, _mla_head_layout(mla_q_norm_g[None], 1),
                           _mla_head_layout(mla_k_norm_g[None], 1), _rope_tables(pos, MLA_ROPE), B, T)

    cmp_tabs = _rope_tables(jnp.arange(ncp) * CMP_STRIDE + CMP_LEN - 1, NSA_ROPE)
    k_c = _compress(proj, C_KC, NSA_DK, B, T, cmp_pe_k, cmp_k_w1, cmp_k_b1, _nsa_head_layout(cmp_k_w2, 1),
                    _nsa_head_layout(cmp_k_b2[None], 1)[0], _nsa_head_layout(nsa_k_norm_g[0:1], 1), cmp_tabs,
                    True, "compress_k")
    v_c = _compress(proj, C_VC, NSA_DV, B, T, cmp_pe_v, cmp_v_w1, cmp_v_b1, cmp_v_w2, cmp_v_b2,
                    jnp.zeros((1, NSA_DV), F32), cmp_tabs, False, "compress_v")
    o_nsa = _nsa_attention(proj, k_c, v_c, _nsa_head_layout(nsa_q_norm_g[None], 1),
                           _nsa_head_layout(nsa_k_norm_g[1:2], 1), _nsa_head_layout(nsa_k_norm_g[2:3], 1),
                           _rope_tables(pos, NSA_ROPE), _overlap_table(ncp, nb), B, T)

    u = _merge(o_mla, o_nsa, w_proj_mla.astype(BF16), w_proj_nsa.astype(BF16), proj)
    h, hn = _out_proj(u, w_out.astype(BF16), x2, ffn_norm_g.reshape(1, -1))
    act = _ffn_up(hn, w_ffn_gate.astype(BF16), w_ffn_up.astype(BF16))
    out = _ffn_down(act, w_ffn_down.astype(BF16), h)
    return out.reshape(B, T, D_MODEL)


def kernel(x, attn_norm_g, w_in, mla_q_lat_g, mla_kv_lat_g, mla_w_uq, mla_w_uk, mla_w_uv, mla_q_norm_g, mla_k_norm_g, nsa_q_norm_g, nsa_k_norm_g, cmp_pe_k, cmp_pe_v, cmp_k_w1, cmp_k_b1, cmp_k_w2, cmp_k_b2, cmp_v_w1, cmp_v_b1, cmp_v_w2, cmp_v_b2, w_proj_mla, w_proj_nsa, w_out, ffn_norm_g, w_ffn_gate, w_ffn_up, w_ffn_down):
    params = (attn_norm_g, w_in, mla_q_lat_g, mla_kv_lat_g, mla_w_uq, mla_w_uk, mla_w_uv, mla_q_norm_g,
              mla_k_norm_g, nsa_q_norm_g, nsa_k_norm_g, cmp_pe_k, cmp_pe_v, cmp_k_w1, cmp_k_b1, cmp_k_w2,
              cmp_k_b2, cmp_v_w1, cmp_v_b1, cmp_v_w2, cmp_v_b2, w_proj_mla, w_proj_nsa, w_out, ffn_norm_g,
              w_ffn_gate, w_ffn_up, w_ffn_down)
    h = x
    for l in range(attn_norm_g.shape[0]):
        h = _layer(h, *(p[l] for p in params))
    return h
```

```python
import functools

import jax
import jax.numpy as jnp
from jax import lax
from jax.experimental import pallas as pl
from jax.experimental.pallas import tpu as pltpu

F32, BF16, I32 = jnp.float32, jnp.bfloat16, jnp.int32

EPS = 1e-6
NEG_INF = -1e30
FORCE = 1e9
ROPE_THETA = 500000.0

D_MODEL = 2048
MLA_HEADS, MLA_NOPE, MLA_ROPE, MLA_V = 16, 128, 64, 128
MLA_QK = MLA_NOPE + MLA_ROPE
MLA_RANK = 512
NSA_HEADS, NSA_GROUPS, NSA_DK, NSA_DV = 16, 4, 192, 128
NSA_HPG = NSA_HEADS // NSA_GROUPS
NSA_ROPE = NSA_DK // 4
CMP_LEN, CMP_STRIDE = 32, 16
SLC_LEN, SLC_TOPK, WINDOW = 64, 16, 512
D_FF = 5632

LANES = 128
HALF_LANES = 64
HEAD_PAD = 256
SEL_LANE0 = 64
PAD_LANE = 127
DV = 128
DV_EXT = DV + 16
VMEM_LIMIT = 48 * 1024 * 1024

C_QN, C_GM, C_KS, C_KW, C_VS, C_VW = 0, 4096, 8192, 9216, 10240, 10752
C_CQ, C_CKV, C_KR, C_GATE, C_KC, C_VC, C_END = 11264, 11776, 12288, 12416, 12544, 13568, 14336
GATE_PITCH = 16

_NT = (((1,), (1,)), ((), ()))


def _params(sem):
    return pltpu.CompilerParams(dimension_semantics=sem, vmem_limit_bytes=VMEM_LIMIT)


def _norm_matmul_kernel(x_ref, g_ref, w_ref, o_ref, xn_ref):
    @pl.when(pl.program_id(1) == 0)
    def _():
        x = x_ref[...]
        ms = jnp.mean(x * x, axis=-1, keepdims=True)
        xn_ref[...] = (x * lax.rsqrt(ms + EPS) * g_ref[...]).astype(BF16)

    o_ref[...] = jnp.dot(xn_ref[...], w_ref[...], preferred_element_type=F32).astype(o_ref.dtype)


def _norm_matmul(x, xcol, k, g, w, tm, tn, name):
    n, m = x.shape[0], w.shape[1]
    return pl.pallas_call(
        _norm_matmul_kernel,
        grid=(n // tm, m // tn),
        in_specs=[pl.BlockSpec((tm, k), lambda i, j: (i, xcol)),
                  pl.BlockSpec((1, k), lambda i, j: (0, 0)),
                  pl.BlockSpec((k, tn), lambda i, j: (0, j))],
        out_specs=pl.BlockSpec((tm, tn), lambda i, j: (i, j)),
        out_shape=jax.ShapeDtypeStruct((n, m), F32),
        scratch_shapes=[pltpu.VMEM((tm, k), BF16)],
        compiler_params=_params(("parallel", "arbitrary")),
        name=name,
    )(x, g, w)


def _merge_kernel(om_ref, on_ref, wm_ref, wn_ref, g0_ref, g1_ref, o_ref):
    ym = jnp.dot(om_ref[...], wm_ref[...], preferred_element_type=F32)
    yn = jnp.dot(on_ref[...], wn_ref[...], preferred_element_type=F32)
    o_ref[...] = (jax.nn.sigmoid(g0_ref[...]) * ym + jax.nn.sigmoid(g1_ref[...]) * yn).astype(BF16)


def _merge(o_mla, o_nsa, w_pm, w_pn, proj, tm=1024, tn=512):
    n = o_mla.shape[0]
    g0, g1 = C_GM // tn, (C_GM + D_MODEL) // tn
    return pl.pallas_call(
        _merge_kernel,
        grid=(n // tm, D_MODEL // tn),
        in_specs=[pl.BlockSpec((tm, D_MODEL), lambda i, j: (i, 0)),
                  pl.BlockSpec((tm, D_MODEL), lambda i, j: (i, 0)),
                  pl.BlockSpec((D_MODEL, tn), lambda i, j: (0, j)),
                  pl.BlockSpec((D_MODEL, tn), lambda i, j: (0, j)),
                  pl.BlockSpec((tm, tn), lambda i, j: (i, g0 + j)),
                  pl.BlockSpec((tm, tn), lambda i, j: (i, g1 + j))],
        out_specs=pl.BlockSpec((tm, tn), lambda i, j: (i, j)),
        out_shape=jax.ShapeDtypeStruct((n, D_MODEL), BF16),
        compiler_params=_params(("parallel", "arbitrary")),
        name="merge",
    )(o_mla, o_nsa, w_pm, w_pn, proj, proj)


def _out_proj_kernel(u_ref, w_ref, x_ref, g_ref, h_ref, hn_ref):
    h = x_ref[...] + jnp.dot(u_ref[...], w_ref[...], preferred_element_type=F32)
    h_ref[...] = h
    ms = jnp.mean(h * h, axis=-1, keepdims=True)
    hn_ref[...] = (h * lax.rsqrt(ms + EPS) * g_ref[...]).astype(BF16)


def _out_proj(u, w_out, x, g, tm=256):
    n = u.shape[0]
    row = pl.BlockSpec((tm, D_MODEL), lambda i: (i, 0))
    return pl.pallas_call(
        _out_proj_kernel,
        grid=(n // tm,),
        in_specs=[row, pl.BlockSpec((D_MODEL, D_MODEL), lambda i: (0, 0)), row,
                  pl.BlockSpec((1, D_MODEL), lambda i: (0, 0))],
        out_specs=[row, row],
        out_shape=[jax.ShapeDtypeStruct((n, D_MODEL), F32), jax.ShapeDtypeStruct((n, D_MODEL), BF16)],
        compiler_params=_params(("parallel",)),
        name="out_proj",
    )(u, w_out, x, g)


def _ffn_up_kernel(hn_ref, wg_ref, wu_ref, o_ref):
    hn = hn_ref[...]
    a = jnp.dot(hn, wg_ref[...], preferred_element_type=F32)
    b = jnp.dot(hn, wu_ref[...], preferred_element_type=F32)
    o_ref[...] = (jax.nn.silu(a) * b).astype(BF16)


def _ffn_up(hn, w_gate, w_up, tm=1024, tn=512):
    n = hn.shape[0]
    return pl.pallas_call(
        _ffn_up_kernel,
        grid=(n // tm, D_FF // tn),
        in_specs=[pl.BlockSpec((tm, D_MODEL), lambda i, j: (i, 0)),
                  pl.BlockSpec((D_MODEL, tn), lambda i, j: (0, j)),
                  pl.BlockSpec((D_MODEL, tn), lambda i, j: (0, j))],
        out_specs=pl.BlockSpec((tm, tn), lambda i, j: (i, j)),
        out_shape=jax.ShapeDtypeStruct((n, D_FF), BF16),
        compiler_params=_params(("parallel", "arbitrary")),
        name="ffn_up",
    )(hn, w_gate, w_up)


def _ffn_down_kernel(a_ref, w_ref, h_ref, o_ref):
    o_ref[...] = h_ref[...] + jnp.dot(a_ref[...], w_ref[...], preferred_element_type=F32)


def _ffn_down(act, w_down, h, tm=1024, tn=256):
    n = act.shape[0]
    return pl.pallas_call(
        _ffn_down_kernel,
        grid=(n // tm, D_MODEL // tn),
        in_specs=[pl.BlockSpec((tm, D_FF), lambda i, j: (i, 0)),
                  pl.BlockSpec((D_FF, tn), lambda i, j: (0, j)),
                  pl.BlockSpec((tm, tn), lambda i, j: (i, j))],
        out_specs=pl.BlockSpec((tm, tn), lambda i, j: (i, j)),
        out_shape=jax.ShapeDtypeStruct((n, D_MODEL), F32),
        compiler_params=_params(("parallel", "arbitrary")),
        name="ffn_down",
    )(act, w_down, h)


def _rms_inv(xa, xb, d):
    return lax.rsqrt(jnp.sum(xa * xa + xb * xb, axis=-1, keepdims=True) * (1.0 / d) + EPS)


def _rope(x, c, s):
    return x * c + pltpu.roll(x, HALF_LANES, 1) * s


def _col_max(s, slabs=8):
    rows = s.shape[0]
    if rows % (8 * slabs) == 0:
        s = jnp.max(s.reshape(slabs, rows // slabs, s.shape[1]), axis=0)
    return jnp.max(s, axis=0, keepdims=True)


def _col_sum(s, slabs=8):
    rows = s.shape[0]
    if rows % (8 * slabs) == 0:
        s = jnp.sum(s.reshape(slabs, rows // slabs, s.shape[1]), axis=0)
    return jnp.sum(s, axis=0, keepdims=True)


def _flash_t(chains, n_last, n_max, tk, s_buf0, s_buf1, mt_buf0, mt_buf1, m_ref, acc_ref):
    nc = len(chains)
    s_buf, mt_buf = (s_buf0, s_buf1), (mt_buf0, mt_buf1)

    def stage_a(j, masked=True):
        off = j * tk
        for c, (q_at, k_at, _, t_cols) in enumerate(chains):
            s = lax.dot_general(k_at(off), q_at(), _NT, preferred_element_type=F32)
            if masked:
                pk = off + lax.broadcasted_iota(I32, (tk, 1), 0)
                s = jnp.where(pk <= t_cols, s, NEG_INF)
            s_buf[j % 2][c] = s
            mt_buf[j % 2][c] = _col_max(s)

    def stage_b(j):
        off = j * tk
        for c, (_, _, vt_at, _) in enumerate(chains):
            m = m_ref[c]
            m_new = jnp.maximum(m, mt_buf[j % 2][c])
            p = jnp.exp(s_buf[j % 2][c] - m_new).astype(BF16)
            acc_ref[c] = jnp.exp(m - m_new) * acc_ref[c] + jnp.dot(vt_at(off), p, preferred_element_type=F32)
            m_ref[c] = m_new

    m_ref[...] = jnp.full(m_ref.shape, NEG_INF, F32)
    acc_ref[...] = jnp.zeros(acc_ref.shape, F32)
    stage_a(0)
    for j in range(n_max):
        if j + 1 < n_max:
            @pl.when(j + 1 < n_last)
            def _(j=j):
                stage_a(j + 1, masked=False)
                stage_b(j)

            @pl.when(j + 1 == n_last)
            def _(j=j):
                stage_a(j + 1)
                stage_b(j)

        @pl.when(j == n_last)
        def _(j=j):
            stage_b(j)

    return [acc_ref[c, :DV] * (1.0 / acc_ref[c, DV:DV + 1]) for c in range(nc)]


def _flash_scratch(nc, tk, nq):
    return [pltpu.VMEM((nc, tk, nq), F32), pltpu.VMEM((nc, tk, nq), F32),
            pltpu.VMEM((nc, 1, nq), F32), pltpu.VMEM((nc, 1, nq), F32),
            pltpu.VMEM((nc, 1, nq), F32), pltpu.VMEM((nc, DV_EXT, nq), F32)]


def _mla_kernel(q_ref, kn_ref, kr_ref, v_ref, gq_ref, gk_ref, cq_ref, sq_ref, ck_ref, sk_ref,
                o_ref, k_s, vt_s, q_s, *flash_scratch, tq, tk, scale, hps):
    qi = pl.program_id(2)
    T = kn_ref.shape[0]

    @pl.when(qi == 0)
    def _():
        kr = kr_ref[...]
        ss_r = jnp.sum(kr * kr, axis=-1, keepdims=True)
        for h in range(hps):
            kn = kn_ref[:, h * LANES:(h + 1) * LANES]
            r = lax.rsqrt((jnp.sum(kn * kn, axis=-1, keepdims=True) + ss_r) * (1.0 / MLA_QK) + EPS)
            k_s[h, :, :LANES] = (kn * r * gk_ref[:, :LANES]).astype(BF16)
            k_s[h, :, LANES:] = _rope(kr * r * gk_ref[:, LANES:], ck_ref[...], sk_ref[...]).astype(BF16)
            vt_s[h, :DV, :] = v_ref[:, h * LANES:(h + 1) * LANES].T.astype(BF16)
            vt_s[h, DV:, :] = jnp.ones((DV_EXT - DV, T), BF16)

    t0 = qi * tq
    t_cols = t0 + lax.broadcasted_iota(I32, (1, tq), 1)
    chains = []
    for h in range(hps):
        qa = q_ref[:, h * HEAD_PAD:h * HEAD_PAD + LANES]
        qb = q_ref[:, h * HEAD_PAD + LANES:(h + 1) * HEAD_PAD]
        r = _rms_inv(qa, qb, MLA_QK) * scale
        qa = qa * r * gq_ref[:, :LANES]
        qb = _rope(qb * r * gq_ref[:, LANES:], cq_ref[...], sq_ref[...])
        q_s[h, :, :LANES] = qa.astype(BF16)
        q_s[h, :, LANES:] = qb.astype(BF16)
        chains.append((lambda h=h: q_s[h], lambda off, h=h: k_s[h, pl.ds(off, tk), :],
                       lambda off, h=h: vt_s[h, :, pl.ds(off, tk)], t_cols))
    for h, o_t in enumerate(_flash_t(chains, t0 // tk, T // tk, tk, *flash_scratch)):
        o_ref[:, h * LANES:(h + 1) * LANES] = o_t.T.astype(BF16)


def _mla_attention(q_up, kv_up, proj, gq, gk, tabs, B, T, tq=256, tk=512, hps=4):
    nq = T // tq
    ng = MLA_HEADS // hps
    c, s = tabs
    qtab = pl.BlockSpec((tq, LANES), lambda b, h, i: (i, 0))
    ktab = pl.BlockSpec((T, LANES), lambda b, h, i: (0, 0))
    gspec = pl.BlockSpec((1, HEAD_PAD), lambda b, h, i: (0, 0))
    return pl.pallas_call(
        functools.partial(_mla_kernel, tq=tq, tk=tk, scale=MLA_QK ** -0.5, hps=hps),
        grid=(B, ng, nq),
        in_specs=[pl.BlockSpec((tq, hps * HEAD_PAD), lambda b, h, i: (b * nq + i, h)),
                  pl.BlockSpec((T, hps * LANES), lambda b, h, i: (b, h)),
                  pl.BlockSpec((T, LANES), lambda b, h, i: (b, C_KR // LANES)),
                  pl.BlockSpec((T, hps * LANES), lambda b, h, i: (b, ng + h)),
                  gspec, gspec, qtab, qtab, ktab, ktab],
        out_specs=pl.BlockSpec((tq, hps * LANES), lambda b, h, i: (b * nq + i, h)),
        out_shape=jax.ShapeDtypeStruct((B * T, MLA_HEADS * MLA_V), BF16),
        scratch_shapes=[pltpu.VMEM((hps, T, HEAD_PAD), BF16), pltpu.VMEM((hps, DV_EXT, T), BF16),
                        pltpu.VMEM((hps, tq, HEAD_PAD), BF16)] + _flash_scratch(hps, tk, tq),
        compiler_params=_params(("parallel", "parallel", "arbitrary")),
        name="mla_attention",
    )(q_up, kv_up, proj, kv_up, gq, gk, c, s, c, s)


def _compress_kernel(*refs, is_key, gb, ncp, nt):
    x_tiles = refs[:gb * nt]
    pe_ref, w1a_ref, w1b_ref, b1_ref, w2_ref, b2_ref, g_ref, c_ref, s_ref, o_ref, xa_s, xb_s = refs[gb * nt:]
    rows = gb * ncp
    for g in range(gb):
        for l in range(CMP_STRIDE):
            for t in range(nt):
                x = x_tiles[g * nt + t][pl.ds(l, ncp, stride=CMP_STRIDE), :]
                lanes = slice((l * nt + t) * LANES, (l * nt + t + 1) * LANES)
                pe_lanes = slice(t * LANES, (t + 1) * LANES)
                xa_s[g * ncp:(g + 1) * ncp, lanes] = (x + pe_ref[l:l + 1, pe_lanes]).astype(BF16)
                xb_s[g * ncp:(g + 1) * ncp, lanes] = (
                    x + pe_ref[CMP_STRIDE + l:CMP_STRIDE + l + 1, pe_lanes]).astype(BF16)
    pa = jnp.dot(xa_s[...], w1a_ref[...], preferred_element_type=F32)
    pb = jnp.dot(xb_s[...], w1b_ref[...], preferred_element_type=F32)
    h = pa + pltpu.roll(pb, rows - 1, 0) + b1_ref[...]
    y = jnp.dot(jax.nn.gelu(h).astype(BF16), w2_ref[...], preferred_element_type=F32) + b2_ref[...]
    if is_key:
        ya, yb = y[:, :LANES], y[:, LANES:]
        r = _rms_inv(ya, yb, NSA_DK)
        c = jnp.concatenate([c_ref[...]] * gb, axis=0)
        s = jnp.concatenate([s_ref[...]] * gb, axis=0)
        ya = _rope(ya * r * g_ref[:, :LANES], c, s)
        yb = yb * r * g_ref[:, LANES:]
        y = jnp.concatenate([ya, yb], axis=1)
    o_ref[0] = y.astype(BF16).reshape(gb, ncp, y.shape[-1])


def _compress(proj, c0, d, B, T, pe, w1, b1, w2, b2, g, tabs, is_key, name, gb=2):
    G = NSA_GROUPS
    ncp = T // CMP_STRIDE
    nt = -(-d // LANES)
    dp = nt * LANES
    kd = CMP_STRIDE * dp
    dout = w2.shape[1]
    hid = w1.shape[1]
    w1 = jnp.pad(w1.astype(BF16).reshape(CMP_LEN, d, hid), ((0, 0), (0, dp - d), (0, 0))).reshape(CMP_LEN * dp, hid)
    pe = _pad_cols(pe, dp)
    c, s = tabs
    const = lambda shape: pl.BlockSpec(shape, lambda b, gi: (0,) * len(shape))
    tile = lambda k: pl.BlockSpec((T, LANES), lambda b, gi: (b, c0 // LANES + gi * gb * nt + k))
    return pl.pallas_call(
        functools.partial(_compress_kernel, is_key=is_key, gb=gb, ncp=ncp, nt=nt),
        grid=(B, G // gb),
        in_specs=[tile(k) for k in range(gb * nt)] + [
                  const((CMP_LEN, dp)), const((kd, hid)), const((kd, hid)), const((1, hid)),
                  const((hid, dout)), const((1, dout)), const((1, dout)),
                  const((ncp, LANES)), const((ncp, LANES))],
        out_specs=pl.BlockSpec((1, gb, ncp, dout), lambda b, gi: (b, gi, 0, 0)),
        out_shape=jax.ShapeDtypeStruct((B, G, ncp, dout), BF16),
        scratch_shapes=[pltpu.VMEM((gb * ncp, kd), BF16), pltpu.VMEM((gb * ncp, kd), BF16)],
        compiler_params=_params(("parallel", "parallel")),
        name=name,
    )(*([proj] * (gb * nt)), pe, w1[:kd], w1[kd:], b1.reshape(1, hid), w2.astype(BF16), b2.reshape(1, dout),
      g, c, s)


def _nsa_kernel(q_ref, ks_ref, kw_ref, vs_ref, vw_ref, gate_ref, kc_ref, vc_ref, gq_ref, gks_ref, gkw_ref,
                cq_ref, sq_ref, ck_ref, sk_ref, ovt_ref,
                o_ref, ks_s, kw_s, vst_s, vwt_s, vct_s, q_s, g_s, *flash_scratch, tq, tk, T, scale):
    qi = pl.program_id(2)
    R = NSA_HPG
    nq = R * tq
    nb = T // SLC_LEN
    topk = min(SLC_TOPK, nb)
    ncp = kc_ref.shape[2]

    @pl.when(qi == 0)
    def _():
        def prep_k(k_ref, g_ref):
            ka, kb = k_ref[:, :LANES], k_ref[:, LANES:]
            r = _rms_inv(ka, kb, NSA_DK)
            return _rope(ka * r * g_ref[:, :LANES], ck_ref[...], sk_ref[...]), kb * r * g_ref[:, LANES:]

        ka, kb = prep_k(ks_ref, gks_ref)
        blk = lax.broadcasted_iota(I32, (T, LANES), 0) // SLC_LEN
        lane = lax.broadcasted_iota(I32, (T, LANES), 1)
        ks_s[:, :LANES] = ka.astype(BF16)
        ks_s[:, LANES:] = (kb + (lane - SEL_LANE0 == blk).astype(F32)).astype(BF16)
        ka, kb = prep_k(kw_ref, gkw_ref)
        lane_w = lax.broadcasted_iota(I32, (WINDOW, LANES), 1)
        kw_s[:WINDOW, :LANES] = jnp.zeros((WINDOW, LANES), BF16)
        kw_s[:WINDOW, LANES:] = jnp.where(lane_w == PAD_LANE, NEG_INF, 0.0).astype(BF16)
        kw_s[WINDOW:, :LANES] = ka.astype(BF16)
        kw_s[WINDOW:, LANES:] = kb.astype(BF16)
        vst_s[:DV, :] = vs_ref[...].T.astype(BF16)
        vst_s[DV:, :] = jnp.ones((DV_EXT - DV, T), BF16)
        vwt_s[:DV, :WINDOW] = jnp.zeros((DV, WINDOW), BF16)
        vwt_s[:DV, WINDOW:] = vw_ref[...].T.astype(BF16)
        vwt_s[DV:, :] = jnp.ones((DV_EXT - DV, WINDOW + T), BF16)
        vct_s[...] = vc_ref[0, 0].astype(F32).T.astype(BF16)

    t0 = pl.multiple_of(qi * tq, tq)
    lane_q = lax.broadcasted_iota(I32, (tq, LANES), 1)
    for r in range(R):
        qa = q_ref[:, r * HEAD_PAD:r * HEAD_PAD + LANES]
        qb = q_ref[:, r * HEAD_PAD + LANES:(r + 1) * HEAD_PAD]
        rs = _rms_inv(qa, qb, NSA_DK) * scale
        qa = _rope(qa * rs * gq_ref[:, :LANES], cq_ref[...], sq_ref[...])
        qb = jnp.where(lane_q == PAD_LANE, 1.0, qb * rs * gq_ref[:, LANES:])
        q_s[r * tq:(r + 1) * tq, :LANES] = qa.astype(BF16)
        q_s[r * tq:(r + 1) * tq, LANES:] = qb.astype(BF16)

    c_q = lax.broadcasted_iota(I32, (1, nq), 1) & (tq - 1)
    t_cols = t0 + c_q
    q = q_s[...]

    g_s[...] = jax.nn.sigmoid(gate_ref[...]).T

    span = tq + WINDOW
    sw = lax.dot_general(kw_s[pl.ds(t0, span), :], q, _NT, preferred_element_type=F32)
    sc = lax.dot_general(kc_ref[0, 0], q, _NT, preferred_element_type=F32)

    cend = lax.broadcasted_iota(I32, (ncp, 1), 0) * CMP_STRIDE + (CMP_LEN - 1)
    mask_c = cend <= t_cols
    sc = jnp.where(mask_c, sc, NEG_INF)
    e = jnp.where(mask_c, jnp.exp(sc - _col_max(sc)), 0.0)
    p = e * (1.0 / jnp.maximum(_col_sum(e), 1e-30))
    o_c = jnp.dot(vct_s[...], p.astype(BF16), preferred_element_type=F32)

    ps = p[:, :tq]
    for r in range(1, R):
        ps = ps + p[:, r * tq:(r + 1) * tq]
    ps_hi = ps.astype(BF16)
    ps_lo = (ps - ps_hi.astype(F32)).astype(BF16)
    imp = (jnp.dot(ovt_ref[...], ps_hi, preferred_element_type=F32)
           + jnp.dot(ovt_ref[...], ps_lo, preferred_element_type=F32))
    jb = lax.broadcasted_iota(I32, (nb, tq), 0)
    tb = (t0 + lax.broadcasted_iota(I32, (1, tq), 1)) // SLC_LEN
    forced = (jb == 0) | (jb == tb) | (jb == tb - 1)
    score = jnp.where(forced, FORCE, jnp.where(jb <= tb, imp, -FORCE))
    rank = jnp.zeros((nb, tq), I32)
    for j in range(nb):
        row = score[j:j + 1, :]
        rank = rank + ((row > score) | ((row == score) & (j < jb))).astype(I32)
    unsel = (rank >= topk).astype(F32)
    unsel = jnp.concatenate([jnp.zeros((SEL_LANE0, tq), F32), unsel,
                             jnp.zeros((LANES - SEL_LANE0 - nb, tq), F32)], axis=0)
    bias = unsel.T * NEG_INF

    for r in range(R):
        qb = q_s[r * tq:(r + 1) * tq, LANES:].astype(F32)
        q_s[r * tq:(r + 1) * tq, LANES:] = (qb + bias).astype(BF16)

    i_k = lax.broadcasted_iota(I32, (tq, 1), 0)
    s_lo = jnp.where(i_k > c_q, sw[:tq], NEG_INF)
    s_mid = sw[tq:WINDOW]
    s_hi = jnp.where(i_k <= c_q, sw[WINDOW:], NEG_INF)
    m = jnp.maximum(jnp.maximum(_col_max(s_lo), _col_max(s_mid)), _col_max(s_hi))
    pw = jnp.concatenate([jnp.exp(s_lo - m), jnp.exp(s_mid - m), jnp.exp(s_hi - m)], axis=0).astype(BF16)
    acc = jnp.dot(vwt_s[:, pl.ds(t0, span)], pw, preferred_element_type=F32)
    o_w = acc[:DV] * (1.0 / acc[DV:DV + 1])

    k_at = lambda off: ks_s[pl.ds(off, tk), :]
    vt_at = lambda off: vst_s[:, pl.ds(off, tk)]
    chains = [(lambda r=r: q_s[r * tq:(r + 1) * tq], k_at, vt_at, t_cols[:, r * tq:(r + 1) * tq])
              for r in range(R)]
    o_s = jnp.concatenate(_flash_t(chains, t0 // tk, T // tk, tk, *flash_scratch), axis=1)

    gs = g_s[pl.ds(pl.multiple_of(pl.program_id(1) * GATE_PITCH, GATE_PITCH), GATE_PITCH), :]
    for r in range(R):
        cols = slice(r * tq, (r + 1) * tq)
        o = (gs[3 * r:3 * r + 1] * o_c[:, cols] + gs[3 * r + 1:3 * r + 2] * o_s[:, cols]
             + gs[3 * r + 2:3 * r + 3] * o_w[:, cols])
        o_ref[:, r * DV:(r + 1) * DV] = o.T.astype(BF16)


def _nsa_attention(proj, k_c, v_c, gq, gks, gkw, tabs, ovt, B, T, tq=256, tk=512):
    nq = T // tq
    G, R = NSA_GROUPS, NSA_HPG
    ncp = k_c.shape[2]
    c, s = tabs
    qtab = pl.BlockSpec((tq, LANES), lambda b, g, i: (i, 0))
    ktab = pl.BlockSpec((T, LANES), lambda b, g, i: (0, 0))
    gspec = pl.BlockSpec((1, HEAD_PAD), lambda b, g, i: (0, 0))
    return pl.pallas_call(
        functools.partial(_nsa_kernel, tq=tq, tk=tk, T=T, scale=NSA_DK ** -0.5),
        grid=(B, G, nq),
        in_specs=[pl.BlockSpec((tq, R * HEAD_PAD), lambda b, g, i: (b * nq + i, g)),
                  pl.BlockSpec((T, HEAD_PAD), lambda b, g, i: (b, C_KS // HEAD_PAD + g)),
                  pl.BlockSpec((T, HEAD_PAD), lambda b, g, i: (b, C_KW // HEAD_PAD + g)),
                  pl.BlockSpec((T, DV), lambda b, g, i: (b, C_VS // DV + g)),
                  pl.BlockSpec((T, DV), lambda b, g, i: (b, C_VW // DV + g)),
                  pl.BlockSpec((tq, LANES), lambda b, g, i: (b * nq + i, C_GATE // LANES)),
                  pl.BlockSpec((1, 1, ncp, HEAD_PAD), lambda b, g, i: (b, g, 0, 0)),
                  pl.BlockSpec((1, 1, ncp, DV), lambda b, g, i: (b, g, 0, 0)),
                  gspec, gspec, gspec, qtab, qtab, ktab, ktab,
                  pl.BlockSpec(ovt.shape, lambda b, g, i: (0, 0))],
        out_specs=pl.BlockSpec((tq, R * DV), lambda b, g, i: (b * nq + i, g)),
        out_shape=jax.ShapeDtypeStruct((B * T, NSA_HEADS * DV), BF16),
        scratch_shapes=[pltpu.VMEM((T, HEAD_PAD), BF16), pltpu.VMEM((WINDOW + T, HEAD_PAD), BF16),
                        pltpu.VMEM((DV_EXT, T), BF16), pltpu.VMEM((DV_EXT, WINDOW + T), BF16),
                        pltpu.VMEM((DV, ncp), BF16), pltpu.VMEM((R * tq, HEAD_PAD), BF16),
                        pltpu.VMEM((LANES, tq), F32)]
        + _flash_scratch(R, tk, tq),
        compiler_params=_params(("parallel", "parallel", "arbitrary")),
        name="nsa_attention",
    )(proj, proj, proj, proj, proj, proj, k_c, v_c, gq, gks, gkw, c, s, c, s, ovt)


def _pad_cols(w, n):
    return jnp.pad(w, ((0, 0), (0, n - w.shape[1])))


def _pad_heads(w, nh, d, dp):
    k = w.shape[0]
    return jnp.pad(w.reshape(k, nh, d), ((0, 0), (0, 0), (0, dp - d))).reshape(k, nh * dp)


def _rope_split(x1, x2, rest_a, rest_b):
    k, nh = x1.shape[:2]
    za = jnp.zeros((k, nh, HALF_LANES - x1.shape[2] - rest_a.shape[2]), x1.dtype)
    zb = jnp.zeros((k, nh, HALF_LANES - x2.shape[2] - rest_b.shape[2]), x1.dtype)
    return jnp.concatenate([x1, rest_a, za, x2, rest_b, zb], axis=2)


def _nsa_head_layout(w, nh):
    k = w.shape[0]
    w = w.reshape(k, nh, NSA_DK)
    h = NSA_ROPE // 2
    fill = HALF_LANES - h
    first = _rope_split(w[..., :h], w[..., h:2 * h], w[..., 2 * h:2 * h + fill], w[..., 2 * h + fill:LANES])
    second = jnp.pad(w[..., LANES:], ((0, 0), (0, 0), (0, HEAD_PAD - NSA_DK)))
    return jnp.concatenate([first, second], axis=2).reshape(k, nh * HEAD_PAD)


def _mla_rope_layout(w, nh):
    k = w.shape[0]
    w = w.reshape(k, nh, MLA_ROPE)
    h = MLA_ROPE // 2
    e = w[..., :0]
    return _rope_split(w[..., :h], w[..., h:], e, e).reshape(k, nh * LANES)


def _mla_head_layout(w, nh):
    k = w.shape[0]
    w = w.reshape(k, nh, MLA_QK)
    rope = _mla_rope_layout(w[..., MLA_NOPE:].reshape(k, nh * MLA_ROPE), nh).reshape(k, nh, LANES)
    return jnp.concatenate([w[..., :MLA_NOPE], rope], axis=2).reshape(k, nh * HEAD_PAD)


def _layout_w_in(w_in):
    sizes = (MLA_RANK, MLA_RANK, MLA_ROPE, NSA_HEADS * NSA_DK,
             NSA_GROUPS * NSA_DK, NSA_GROUPS * NSA_DV, NSA_GROUPS * NSA_DK, NSA_GROUPS * NSA_DV,
             NSA_GROUPS * NSA_DK, NSA_GROUPS * NSA_DV, NSA_HEADS * 3, 2 * D_MODEL)
    offs, o = [], 0
    for s in sizes:
        offs.append((o, o + s))
        o += s
    w_in = w_in.astype(BF16)
    cq, ckv, kr, qn, kc, vc, ks, vs, kw, vw, gn, gm = (w_in[:, a:b] for a, b in offs)
    w = jnp.concatenate([
        _nsa_head_layout(qn, NSA_HEADS), gm,
        _nsa_head_layout(ks, NSA_GROUPS), _nsa_head_layout(kw, NSA_GROUPS), vs, vw,
        cq, ckv, _mla_rope_layout(kr, 1), _pad_cols(_pad_heads(gn, NSA_GROUPS, NSA_HPG * 3, GATE_PITCH), LANES),
        _pad_heads(kc, NSA_GROUPS, NSA_DK, HEAD_PAD), vc], axis=1)
    w = _pad_cols(w, C_END)
    assert w.shape[1] == C_END and C_VC + NSA_GROUPS * NSA_DV <= C_END
    return w


def _rope_tables(pos, rot_dim):
    half = rot_dim // 2
    inv = ROPE_THETA ** (-jnp.arange(0, rot_dim, 2, dtype=F32) / rot_dim)
    ang = pos.astype(F32)[:, None] * inv[None, :]
    cos, sin = jnp.cos(ang), jnp.sin(ang)
    n = pos.shape[0]
    c = jnp.ones((n, LANES), F32).at[:, :half].set(cos).at[:, HALF_LANES:HALF_LANES + half].set(cos)
    s = jnp.zeros((n, LANES), F32).at[:, :half].set(-sin).at[:, HALF_LANES:HALF_LANES + half].set(sin)
    return c, s


def _overlap_table(ncp, nb):
    cs = jnp.arange(ncp)[None, :] * CMP_STRIDE
    ss = jnp.arange(nb)[:, None] * SLC_LEN
    ov = jnp.clip(jnp.minimum(cs + CMP_LEN, ss + SLC_LEN) - jnp.maximum(cs, ss), 0, None).astype(F32) / CMP_LEN
    return ov.astype(BF16)


def _layer(x, attn_norm_g, w_in, mla_q_lat_g, mla_kv_lat_g, mla_w_uq, mla_w_uk, mla_w_uv,
           mla_q_norm_g, mla_k_norm_g, nsa_q_norm_g, nsa_k_norm_g, cmp_pe_k, cmp_pe_v,
           cmp_k_w1, cmp_k_b1, cmp_k_w2, cmp_k_b2, cmp_v_w1, cmp_v_b1, cmp_v_w2, cmp_v_b2,
           w_proj_mla, w_proj_nsa, w_out, ffn_norm_g, w_ffn_gate, w_ffn_up, w_ffn_down):
    B, T, _ = x.shape
    n = B * T
    ncp = T // CMP_STRIDE
    nb = T // SLC_LEN
    assert T % 512 == 0 and ncp % LANES == 0 and nb % 8 == 0 and SEL_LANE0 + nb <= PAD_LANE and T >= WINDOW
    x2 = x.reshape(n, D_MODEL)
    pos = jnp.arange(T)

    proj = _norm_matmul(x2, 0, D_MODEL, attn_norm_g.reshape(1, -1), _layout_w_in(w_in), 1024, 1024, "in_proj")

    q_up = _norm_matmul(proj, C_CQ // MLA_RANK, MLA_RANK, mla_q_lat_g.reshape(1, -1),
                        _mla_head_layout(mla_w_uq, MLA_HEADS).astype(BF16), 1024, 2048, "mla_q_up")
    kv_up = _norm_matmul(proj, C_CKV // MLA_RANK, MLA_RANK, mla_kv_lat_g.reshape(1, -1),
                         jnp.concatenate([mla_w_uk, mla_w_uv], axis=1).astype(BF16), 1024, 2048, "mla_kv_up")
    o_mla = _mla_attention(q_up, kv_up, proj, _mla_head_layout(mla_q_norm_g[None], 1),
                           _mla_head_layout(mla_k_norm_g[None], 1), _rope_tables(pos, MLA_ROPE), B, T)

    cmp_tabs = _rope_tables(jnp.arange(ncp) * CMP_STRIDE + CMP_LEN - 1, NSA_ROPE)
    k_c = _compress(proj, C_KC, NSA_DK, B, T, cmp_pe_k, cmp_k_w1, cmp_k_b1, _nsa_head_layout(cmp_k_w2, 1),
                    _nsa_head_layout(cmp_k_b2[None], 1)[0], _nsa_head_layout(nsa_k_norm_g[0:1], 1), cmp_tabs,
                    True, "compress_k")
    v_c = _compress(proj, C_VC, NSA_DV, B, T, cmp_pe_v, cmp_v_w1, cmp_v_b1, cmp_v_w2, cmp_v_b2,
                    jnp.zeros((1, NSA_DV), F32), cmp_tabs, False, "compress_v")
    o_nsa = _nsa_attention(proj, k_c, v_c, _nsa_head_layout(nsa_q_norm_g[None], 1),
                           _nsa_head_layout(nsa_k_norm_g[1:2], 1), _nsa_head_layout(nsa_k_norm_g[2:3], 1),
                           _rope_tables(pos, NSA_ROPE), _overlap_table(ncp, nb), B, T)

    u = _merge(o_mla, o_nsa, w_proj_mla.astype(BF16), w_proj_nsa.astype(BF16), proj)
    h, hn = _out_proj(u, w_out.astype(BF16), x2, ffn_norm_g.reshape(1, -1))
    act = _ffn_up(hn, w_ffn_gate.astype(BF16), w_ffn_up.astype(BF16))
    out = _ffn_down(act, w_ffn_down.astype(BF16), h)
    return out.reshape(B, T, D_MODEL)


def kernel(x, attn_norm_g, w_in, mla_q_lat_g, mla_kv_lat_g, mla_w_uq, mla_w_uk, mla_w_uv, mla_q_norm_g, mla_k_norm_g, nsa_q_norm_g, nsa_k_norm_g, cmp_pe_k, cmp_pe_v, cmp_k_w1, cmp_k_b1, cmp_k_w2, cmp_k_b2, cmp_v_w1, cmp_v_b1, cmp_v_w2, cmp_v_b2, w_proj_mla, w_proj_nsa, w_out, ffn_norm_g, w_ffn_gate, w_ffn_up, w_ffn_down):
    params = (attn_norm_g, w_in, mla_q_lat_g, mla_kv_lat_g, mla_w_uq, mla_w_uk, mla_w_uv, mla_q_norm_g,
              mla_k_norm_g, nsa_q_norm_g, nsa_k_norm_g, cmp_pe_k, cmp_pe_v, cmp_k_w1, cmp_k_b1, cmp_k_w2,
              cmp_k_b2, cmp_v_w1, cmp_v_b1, cmp_v_w2, cmp_v_b2, w_proj_mla, w_proj_nsa, w_out, ffn_norm_g,
              w_ffn_gate, w_ffn_up, w_ffn_down)
    h = x
    for l in range(attn_norm_g.shape[0]):
        h = _layer(h, *(p[l] for p in params))
    return h
```

```python
import functools

import numpy as np
import jax
import jax.numpy as jnp
from jax import lax
from jax.experimental import pallas as pl
from jax.experimental.pallas import tpu as pltpu

F32, BF16, I32 = jnp.float32, jnp.bfloat16, jnp.int32

EPS = 1e-6
NEG_INF = -1e30
FORCE = 1e9
ROPE_THETA = 500000.0

D_MODEL = 2048
MLA_HEADS, MLA_NOPE, MLA_ROPE, MLA_V = 16, 128, 64, 128
MLA_QK = MLA_NOPE + MLA_ROPE
MLA_RANK = 512
NSA_HEADS, NSA_GROUPS, NSA_DK, NSA_DV = 16, 4, 192, 128
NSA_HPG = NSA_HEADS // NSA_GROUPS
NSA_ROPE = NSA_DK // 4
CMP_LEN, CMP_STRIDE = 32, 16
SLC_LEN, SLC_TOPK, WINDOW = 64, 16, 512
D_FF = 5632

LANES = 128
HALF_LANES = 64
HEAD_PAD = 256
SEL_LANE0 = 64
PAD_LANE = 127
DV = 128
DV_EXT = DV + 16
VMEM_LIMIT = 48 * 1024 * 1024

C_QN, C_GM, C_KS, C_KW, C_VS, C_VW = 0, 4096, 8192, 9216, 10240, 10752
C_CQ, C_CKV, C_KR, C_GATE, C_KC, C_VC, C_END = 11264, 11776, 12288, 12416, 12544, 13568, 14336
GATE_PITCH = 16

_NT = (((1,), (1,)), ((), ()))


def _params(sem):
    return pltpu.CompilerParams(dimension_semantics=sem, vmem_limit_bytes=VMEM_LIMIT)


def _norm_matmul_kernel(x_ref, g_ref, w_ref, o_ref, xn_ref):
    @pl.when(pl.program_id(1) == 0)
    def _():
        x = x_ref[...]
        ms = jnp.mean(x * x, axis=-1, keepdims=True)
        xn_ref[...] = (x * lax.rsqrt(ms + EPS) * g_ref[...]).astype(BF16)

    o_ref[...] = jnp.dot(xn_ref[...], w_ref[...], preferred_element_type=F32).astype(o_ref.dtype)


def _norm_matmul(x, xcol, k, g, w, tm, tn, name):
    n, m = x.shape[0], w.shape[1]
    return pl.pallas_call(
        _norm_matmul_kernel,
        grid=(n // tm, m // tn),
        in_specs=[pl.BlockSpec((tm, k), lambda i, j: (i, xcol)),
                  pl.BlockSpec((1, k), lambda i, j: (0, 0)),
                  pl.BlockSpec((k, tn), lambda i, j: (0, j))],
        out_specs=pl.BlockSpec((tm, tn), lambda i, j: (i, j)),
        out_shape=jax.ShapeDtypeStruct((n, m), F32),
        scratch_shapes=[pltpu.VMEM((tm, k), BF16)],
        compiler_params=_params(("parallel", "arbitrary")),
        name=name,
    )(x, g, w)


def _merge_kernel(om_ref, on_ref, wm_ref, wn_ref, g0_ref, g1_ref, o_ref):
    ym = jnp.dot(om_ref[...], wm_ref[...], preferred_element_type=F32)
    yn = jnp.dot(on_ref[...], wn_ref[...], preferred_element_type=F32)
    o_ref[...] = (jax.nn.sigmoid(g0_ref[...]) * ym + jax.nn.sigmoid(g1_ref[...]) * yn).astype(BF16)


def _merge(o_mla, o_nsa, w_pm, w_pn, proj, tm=1024, tn=512):
    n = o_mla.shape[0]
    g0, g1 = C_GM // tn, (C_GM + D_MODEL) // tn
    return pl.pallas_call(
        _merge_kernel,
        grid=(n // tm, D_MODEL // tn),
        in_specs=[pl.BlockSpec((tm, D_MODEL), lambda i, j: (i, 0)),
                  pl.BlockSpec((tm, D_MODEL), lambda i, j: (i, 0)),
                  pl.BlockSpec((D_MODEL, tn), lambda i, j: (0, j)),
                  pl.BlockSpec((D_MODEL, tn), lambda i, j: (0, j)),
                  pl.BlockSpec((tm, tn), lambda i, j: (i, g0 + j)),
                  pl.BlockSpec((tm, tn), lambda i, j: (i, g1 + j))],
        out_specs=pl.BlockSpec((tm, tn), lambda i, j: (i, j)),
        out_shape=jax.ShapeDtypeStruct((n, D_MODEL), BF16),
        compiler_params=_params(("parallel", "arbitrary")),
        name="merge",
    )(o_mla, o_nsa, w_pm, w_pn, proj, proj)


def _out_proj_kernel(u_ref, w_ref, x_ref, g_ref, h_ref, hn_ref):
    h = x_ref[...] + jnp.dot(u_ref[...], w_ref[...], preferred_element_type=F32)
    h_ref[...] = h
    ms = jnp.mean(h * h, axis=-1, keepdims=True)
    hn_ref[...] = (h * lax.rsqrt(ms + EPS) * g_ref[...]).astype(BF16)


def _out_proj(u, w_out, x, g, tm=512):
    n = u.shape[0]
    row = pl.BlockSpec((tm, D_MODEL), lambda i: (i, 0))
    return pl.pallas_call(
        _out_proj_kernel,
        grid=(n // tm,),
        in_specs=[row, pl.BlockSpec((D_MODEL, D_MODEL), lambda i: (0, 0)), row,
                  pl.BlockSpec((1, D_MODEL), lambda i: (0, 0))],
        out_specs=[row, row],
        out_shape=[jax.ShapeDtypeStruct((n, D_MODEL), F32), jax.ShapeDtypeStruct((n, D_MODEL), BF16)],
        compiler_params=_params(("parallel",)),
        name="out_proj",
    )(u, w_out, x, g)


def _ffn_up_kernel(hn_ref, wg_ref, wu_ref, o_ref):
    hn = hn_ref[...]
    a = jnp.dot(hn, wg_ref[...], preferred_element_type=F32)
    b = jnp.dot(hn, wu_ref[...], preferred_element_type=F32)
    o_ref[...] = (jax.nn.silu(a) * b).astype(BF16)


def _ffn_up(hn, w_gate, w_up, tm=1024, tn=512):
    n = hn.shape[0]
    return pl.pallas_call(
        _ffn_up_kernel,
        grid=(n // tm, D_FF // tn),
        in_specs=[pl.BlockSpec((tm, D_MODEL), lambda i, j: (i, 0)),
                  pl.BlockSpec((D_MODEL, tn), lambda i, j: (0, j)),
                  pl.BlockSpec((D_MODEL, tn), lambda i, j: (0, j))],
        out_specs=pl.BlockSpec((tm, tn), lambda i, j: (i, j)),
        out_shape=jax.ShapeDtypeStruct((n, D_FF), BF16),
        compiler_params=_params(("parallel", "arbitrary")),
        name="ffn_up",
    )(hn, w_gate, w_up)


def _ffn_down_kernel(a_ref, w_ref, h_ref, o_ref):
    o_ref[...] = h_ref[...] + jnp.dot(a_ref[...], w_ref[...], preferred_element_type=F32)


def _ffn_down(act, w_down, h, tm=1024, tn=512):
    n = act.shape[0]
    return pl.pallas_call(
        _ffn_down_kernel,
        grid=(n // tm, D_MODEL // tn),
        in_specs=[pl.BlockSpec((tm, D_FF), lambda i, j: (i, 0)),
                  pl.BlockSpec((D_FF, tn), lambda i, j: (0, j)),
                  pl.BlockSpec((tm, tn), lambda i, j: (i, j))],
        out_specs=pl.BlockSpec((tm, tn), lambda i, j: (i, j)),
        out_shape=jax.ShapeDtypeStruct((n, D_MODEL), F32),
        compiler_params=_params(("parallel", "arbitrary")),
        name="ffn_down",
    )(act, w_down, h)


def _rms_inv(xa, xb, d):
    return lax.rsqrt(jnp.sum(xa * xa + xb * xb, axis=-1, keepdims=True) * (1.0 / d) + EPS)


def _rope(x, c, s):
    return x * c + pltpu.roll(x, HALF_LANES, 1) * s


def _col_max(s, slabs=8):
    rows = s.shape[0]
    if rows % (8 * slabs) == 0:
        s = jnp.max(s.reshape(slabs, rows // slabs, s.shape[1]), axis=0)
    return jnp.max(s, axis=0, keepdims=True)


def _col_sum(s, slabs=8):
    rows = s.shape[0]
    if rows % (8 * slabs) == 0:
        s = jnp.sum(s.reshape(slabs, rows // slabs, s.shape[1]), axis=0)
    return jnp.sum(s, axis=0, keepdims=True)


def _flash_t(chains, n_last, n_max, tk, s_buf0, s_buf1, mt_buf0, mt_buf1, m_ref, acc_ref):
    nc = len(chains)
    s_buf, mt_buf = (s_buf0, s_buf1), (mt_buf0, mt_buf1)

    def stage_a(j, masked=True):
        off = j * tk
        for c, (q_at, k_at, _, t_cols) in enumerate(chains):
            s = lax.dot_general(k_at(off), q_at(), _NT, preferred_element_type=F32)
            if masked:
                pk = off + lax.broadcasted_iota(I32, (tk, 1), 0)
                s = jnp.where(pk <= t_cols, s, NEG_INF)
            s_buf[j % 2][c] = s
            mt_buf[j % 2][c] = _col_max(s)

    def stage_b(j):
        off = j * tk
        for c, (_, _, vt_at, _) in enumerate(chains):
            m = m_ref[c]
            m_new = jnp.maximum(m, mt_buf[j % 2][c])
            p = jnp.exp(s_buf[j % 2][c] - m_new).astype(BF16)
            acc_ref[c] = jnp.exp(m - m_new) * acc_ref[c] + jnp.dot(vt_at(off), p, preferred_element_type=F32)
            m_ref[c] = m_new

    m_ref[...] = jnp.full(m_ref.shape, NEG_INF, F32)
    acc_ref[...] = jnp.zeros(acc_ref.shape, F32)
    stage_a(0)
    for j in range(n_max):
        if j + 1 < n_max:
            @pl.when(j + 1 < n_last)
            def _(j=j):
                stage_a(j + 1, masked=False)
                stage_b(j)

            @pl.when(j + 1 == n_last)
            def _(j=j):
                stage_a(j + 1)
                stage_b(j)

        @pl.when(j == n_last)
        def _(j=j):
            stage_b(j)

    return [acc_ref[c, :DV] * (1.0 / acc_ref[c, DV:DV + 1]) for c in range(nc)]


def _flash_scratch(nc, tk, nq):
    return [pltpu.VMEM((nc, tk, nq), F32), pltpu.VMEM((nc, tk, nq), F32),
            pltpu.VMEM((nc, 1, nq), F32), pltpu.VMEM((nc, 1, nq), F32),
            pltpu.VMEM((nc, 1, nq), F32), pltpu.VMEM((nc, DV_EXT, nq), F32)]


def _mla_kernel(q_ref, kn_ref, kr_ref, v_ref, gq_ref, gk_ref, cq_ref, sq_ref, ck_ref, sk_ref,
                o_ref, k_s, vt_s, q_s, *flash_scratch, tq, tk, scale, hps):
    qi = pl.program_id(2)
    T = kn_ref.shape[0]

    @pl.when(qi == 0)
    def _():
        kr = kr_ref[...]
        ss_r = jnp.sum(kr * kr, axis=-1, keepdims=True)
        kr_rot = _rope(kr * gk_ref[:, LANES:], ck_ref[...], sk_ref[...])
        for h in range(hps):
            kn = kn_ref[:, h * LANES:(h + 1) * LANES]
            r = lax.rsqrt((jnp.sum(kn * kn, axis=-1, keepdims=True) + ss_r) * (1.0 / MLA_QK) + EPS)
            k_s[h, :, :LANES] = (kn * r * gk_ref[:, :LANES]).astype(BF16)
            k_s[h, :, LANES:] = (kr_rot * r).astype(BF16)
            vt_s[h, :DV, :] = v_ref[:, h * LANES:(h + 1) * LANES].T.astype(BF16)
            vt_s[h, DV:, :] = jnp.ones((DV_EXT - DV, T), BF16)

    t0 = qi * tq
    t_cols = t0 + lax.broadcasted_iota(I32, (1, tq), 1)
    chains = []
    for h in range(hps):
        qa = q_ref[:, h * HEAD_PAD:h * HEAD_PAD + LANES]
        qb = q_ref[:, h * HEAD_PAD + LANES:(h + 1) * HEAD_PAD]
        r = _rms_inv(qa, qb, MLA_QK) * scale
        qa = qa * r * gq_ref[:, :LANES]
        qb = _rope(qb * r * gq_ref[:, LANES:], cq_ref[...], sq_ref[...])
        q_s[h, :, :LANES] = qa.astype(BF16)
        q_s[h, :, LANES:] = qb.astype(BF16)
        chains.append((lambda h=h: q_s[h], lambda off, h=h: k_s[h, pl.ds(off, tk), :],
                       lambda off, h=h: vt_s[h, :, pl.ds(off, tk)], t_cols))
    for h, o_t in enumerate(_flash_t(chains, t0 // tk, T // tk, tk, *flash_scratch)):
        o_ref[:, h * LANES:(h + 1) * LANES] = o_t.T.astype(BF16)


def _mla_attention(q_up, kv_up, proj, gq, gk, tabs, B, T, tq=256, tk=512, hps=4):
    nq = T // tq
    ng = MLA_HEADS // hps
    c, s = tabs
    qtab = pl.BlockSpec((tq, LANES), lambda b, h, i: (i, 0))
    ktab = pl.BlockSpec((T, LANES), lambda b, h, i: (0, 0))
    gspec = pl.BlockSpec((1, HEAD_PAD), lambda b, h, i: (0, 0))
    return pl.pallas_call(
        functools.partial(_mla_kernel, tq=tq, tk=tk, scale=MLA_QK ** -0.5, hps=hps),
        grid=(B, ng, nq),
        in_specs=[pl.BlockSpec((tq, hps * HEAD_PAD), lambda b, h, i: (b * nq + i, h)),
                  pl.BlockSpec((T, hps * LANES), lambda b, h, i: (b, h)),
                  pl.BlockSpec((T, LANES), lambda b, h, i: (b, C_KR // LANES)),
                  pl.BlockSpec((T, hps * LANES), lambda b, h, i: (b, ng + h)),
                  gspec, gspec, qtab, qtab, ktab, ktab],
        out_specs=pl.BlockSpec((tq, hps * LANES), lambda b, h, i: (b * nq + i, h)),
        out_shape=jax.ShapeDtypeStruct((B * T, MLA_HEADS * MLA_V), BF16),
        scratch_shapes=[pltpu.VMEM((hps, T, HEAD_PAD), BF16), pltpu.VMEM((hps, DV_EXT, T), BF16),
                        pltpu.VMEM((hps, tq, HEAD_PAD), BF16)] + _flash_scratch(hps, tk, tq),
        compiler_params=_params(("parallel", "parallel", "arbitrary")),
        name="mla_attention",
    )(q_up, kv_up, proj, kv_up, gq, gk, c, s, c, s)


def _compress_kernel(*refs, is_key, gb, ncp, nt):
    x_tiles = refs[:gb * nt]
    pe_ref, w1a_ref, w1b_ref, b1_ref, w2_ref, b2_ref, g_ref, c_ref, s_ref, o_ref, xa_s, xb_s = refs[gb * nt:]
    rows = gb * ncp
    for g in range(gb):
        for l in range(CMP_STRIDE):
            for t in range(nt):
                x = x_tiles[g * nt + t][pl.ds(l, ncp, stride=CMP_STRIDE), :]
                lanes = slice((l * nt + t) * LANES, (l * nt + t + 1) * LANES)
                pe_lanes = slice(t * LANES, (t + 1) * LANES)
                xa_s[g * ncp:(g + 1) * ncp, lanes] = (x + pe_ref[l:l + 1, pe_lanes]).astype(BF16)
                xb_s[g * ncp:(g + 1) * ncp, lanes] = (
                    x + pe_ref[CMP_STRIDE + l:CMP_STRIDE + l + 1, pe_lanes]).astype(BF16)
    pa = jnp.dot(xa_s[...], w1a_ref[...], preferred_element_type=F32)
    pb = jnp.dot(xb_s[...], w1b_ref[...], preferred_element_type=F32)
    h = pa + pltpu.roll(pb, rows - 1, 0) + b1_ref[...]
    y = jnp.dot(jax.nn.gelu(h).astype(BF16), w2_ref[...], preferred_element_type=F32) + b2_ref[...]
    if is_key:
        ya, yb = y[:, :LANES], y[:, LANES:]
        r = _rms_inv(ya, yb, NSA_DK)
        c = jnp.concatenate([c_ref[...]] * gb, axis=0)
        s = jnp.concatenate([s_ref[...]] * gb, axis=0)
        ya = _rope(ya * r * g_ref[:, :LANES], c, s)
        yb = yb * r * g_ref[:, LANES:]
        y = jnp.concatenate([ya, yb], axis=1)
    o_ref[0] = y.astype(BF16).reshape(gb, ncp, y.shape[-1])


def _compress(proj, c0, d, B, T, pe, w1, b1, w2, b2, g, tabs, is_key, name, gb=2):
    G = NSA_GROUPS
    ncp = T // CMP_STRIDE
    nt = -(-d // LANES)
    dp = nt * LANES
    kd = CMP_STRIDE * dp
    dout = w2.shape[1]
    hid = w1.shape[1]
    w1 = jnp.pad(w1.astype(BF16).reshape(CMP_LEN, d, hid), ((0, 0), (0, dp - d), (0, 0))).reshape(CMP_LEN * dp, hid)
    pe = _pad_cols(pe, dp)
    c, s = tabs
    const = lambda shape: pl.BlockSpec(shape, lambda b, gi: (0,) * len(shape))
    tile = lambda k: pl.BlockSpec((T, LANES), lambda b, gi: (b, c0 // LANES + gi * gb * nt + k))
    return pl.pallas_call(
        functools.partial(_compress_kernel, is_key=is_key, gb=gb, ncp=ncp, nt=nt),
        grid=(B, G // gb),
        in_specs=[tile(k) for k in range(gb * nt)] + [
                  const((CMP_LEN, dp)), const((kd, hid)), const((kd, hid)), const((1, hid)),
                  const((hid, dout)), const((1, dout)), const((1, dout)),
                  const((ncp, LANES)), const((ncp, LANES))],
        out_specs=pl.BlockSpec((1, gb, ncp, dout), lambda b, gi: (b, gi, 0, 0)),
        out_shape=jax.ShapeDtypeStruct((B, G, ncp, dout), BF16),
        scratch_shapes=[pltpu.VMEM((gb * ncp, kd), BF16), pltpu.VMEM((gb * ncp, kd), BF16)],
        compiler_params=_params(("parallel", "parallel")),
        name=name,
    )(*([proj] * (gb * nt)), pe, w1[:kd], w1[kd:], b1.reshape(1, hid), w2.astype(BF16), b2.reshape(1, dout),
      g, c, s)


def _nsa_kernel(q_ref, ks_ref, kw_ref, vs_ref, vw_ref, gate_ref, kc_ref, vc_ref, gq_ref, gks_ref, gkw_ref,
                cq_ref, sq_ref, ck_ref, sk_ref, ovt_ref,
                o_ref, ks_s, kw_s, vst_s, vwt_s, vct_s, q_s, g_s, *flash_scratch, tq, tk, T, scale):
    qi = pl.program_id(2)
    R = NSA_HPG
    nq = R * tq
    nb = T // SLC_LEN
    topk = min(SLC_TOPK, nb)
    ncp = kc_ref.shape[2]

    @pl.when(qi == 0)
    def _():
        def prep_k(k_ref, g_ref):
            ka, kb = k_ref[:, :LANES], k_ref[:, LANES:]
            r = _rms_inv(ka, kb, NSA_DK)
            return _rope(ka * r * g_ref[:, :LANES], ck_ref[...], sk_ref[...]), kb * r * g_ref[:, LANES:]

        ka, kb = prep_k(ks_ref, gks_ref)
        blk = lax.broadcasted_iota(I32, (T, LANES), 0) // SLC_LEN
        lane = lax.broadcasted_iota(I32, (T, LANES), 1)
        ks_s[:, :LANES] = ka.astype(BF16)
        ks_s[:, LANES:] = (kb + (lane - SEL_LANE0 == blk).astype(F32)).astype(BF16)
        ka, kb = prep_k(kw_ref, gkw_ref)
        lane_w = lax.broadcasted_iota(I32, (WINDOW, LANES), 1)
        kw_s[:WINDOW, :LANES] = jnp.zeros((WINDOW, LANES), BF16)
        kw_s[:WINDOW, LANES:] = jnp.where(lane_w == PAD_LANE, NEG_INF, 0.0).astype(BF16)
        kw_s[WINDOW:, :LANES] = ka.astype(BF16)
        kw_s[WINDOW:, LANES:] = kb.astype(BF16)
        vst_s[:DV, :] = vs_ref[...].T.astype(BF16)
        vst_s[DV:, :] = jnp.ones((DV_EXT - DV, T), BF16)
        vwt_s[:DV, :WINDOW] = jnp.zeros((DV, WINDOW), BF16)
        vwt_s[:DV, WINDOW:] = vw_ref[...].T.astype(BF16)
        vwt_s[DV:, :] = jnp.ones((DV_EXT - DV, WINDOW + T), BF16)
        vct_s[...] = vc_ref[0, 0].astype(F32).T.astype(BF16)

    t0 = pl.multiple_of(qi * tq, tq)
    lane_q = lax.broadcasted_iota(I32, (tq, LANES), 1)
    for r in range(R):
        qa = q_ref[:, r * HEAD_PAD:r * HEAD_PAD + LANES]
        qb = q_ref[:, r * HEAD_PAD + LANES:(r + 1) * HEAD_PAD]
        rs = _rms_inv(qa, qb, NSA_DK) * scale
        qa = _rope(qa * rs * gq_ref[:, :LANES], cq_ref[...], sq_ref[...])
        qb = jnp.where(lane_q == PAD_LANE, 1.0, qb * rs * gq_ref[:, LANES:])
        q_s[r * tq:(r + 1) * tq, :LANES] = qa.astype(BF16)
        q_s[r * tq:(r + 1) * tq, LANES:] = qb.astype(BF16)

    c_q = lax.broadcasted_iota(I32, (1, nq), 1) & (tq - 1)
    t_cols = t0 + c_q
    q = q_s[...]

    g_s[...] = jax.nn.sigmoid(gate_ref[...]).T

    span = tq + WINDOW
    sw = lax.dot_general(kw_s[pl.ds(t0, span), :], q, _NT, preferred_element_type=F32)
    sc = lax.dot_general(kc_ref[0, 0], q, _NT, preferred_element_type=F32)

    cend = lax.broadcasted_iota(I32, (ncp, 1), 0) * CMP_STRIDE + (CMP_LEN - 1)
    mask_c = cend <= t_cols
    sc = jnp.where(mask_c, sc, NEG_INF)
    e = jnp.where(mask_c, jnp.exp(sc - _col_max(sc)), 0.0)
    p = e * (1.0 / jnp.maximum(_col_sum(e), 1e-30))
    o_c = jnp.dot(vct_s[...], p.astype(BF16), preferred_element_type=F32)

    ps = p[:, :tq]
    for r in range(1, R):
        ps = ps + p[:, r * tq:(r + 1) * tq]
    ps_hi = ps.astype(BF16)
    ps_lo = (ps - ps_hi.astype(F32)).astype(BF16)
    imp = (jnp.dot(ovt_ref[...], ps_hi, preferred_element_type=F32)
           + jnp.dot(ovt_ref[...], ps_lo, preferred_element_type=F32))
    jb = lax.broadcasted_iota(I32, (nb, tq), 0)
    tb = (t0 + lax.broadcasted_iota(I32, (1, tq), 1)) // SLC_LEN
    forced = (jb == 0) | (jb == tb) | (jb == tb - 1)
    score = jnp.where(forced, FORCE, jnp.where(jb <= tb, imp, -FORCE))
    rank = jnp.zeros((nb, tq), I32)
    for j in range(nb):
        row = score[j:j + 1, :]
        rank = rank + ((row > score) | ((row == score) & (j < jb))).astype(I32)
    unsel = (rank >= topk).astype(F32)
    unsel = jnp.concatenate([jnp.zeros((SEL_LANE0, tq), F32), unsel,
                             jnp.zeros((LANES - SEL_LANE0 - nb, tq), F32)], axis=0)
    bias = unsel.T * NEG_INF

    for r in range(R):
        qb = q_s[r * tq:(r + 1) * tq, LANES:].astype(F32)
        q_s[r * tq:(r + 1) * tq, LANES:] = (qb + bias).astype(BF16)

    i_k = lax.broadcasted_iota(I32, (tq, 1), 0)
    s_lo = jnp.where(i_k > c_q, sw[:tq], NEG_INF)
    s_mid = sw[tq:WINDOW]
    s_hi = jnp.where(i_k <= c_q, sw[WINDOW:], NEG_INF)
    m = jnp.maximum(jnp.maximum(_col_max(s_lo), _col_max(s_mid)), _col_max(s_hi))
    pw = jnp.concatenate([jnp.exp(s_lo - m), jnp.exp(s_mid - m), jnp.exp(s_hi - m)], axis=0).astype(BF16)
    acc = jnp.dot(vwt_s[:, pl.ds(t0, span)], pw, preferred_element_type=F32)
    o_w = acc[:DV] * (1.0 / acc[DV:DV + 1])

    k_at = lambda off: ks_s[pl.ds(off, tk), :]
    vt_at = lambda off: vst_s[:, pl.ds(off, tk)]
    chains = [(lambda r=r: q_s[r * tq:(r + 1) * tq], k_at, vt_at, t_cols[:, r * tq:(r + 1) * tq])
              for r in range(R)]
    o_s = jnp.concatenate(_flash_t(chains, t0 // tk, T // tk, tk, *flash_scratch), axis=1)

    gs = g_s[pl.ds(pl.multiple_of(pl.program_id(1) * GATE_PITCH, GATE_PITCH), GATE_PITCH), :]
    for r in range(R):
        cols = slice(r * tq, (r + 1) * tq)
        o = (gs[3 * r:3 * r + 1] * o_c[:, cols] + gs[3 * r + 1:3 * r + 2] * o_s[:, cols]
             + gs[3 * r + 2:3 * r + 3] * o_w[:, cols])
        o_ref[:, r * DV:(r + 1) * DV] = o.T.astype(BF16)


def _nsa_attention(proj, k_c, v_c, gq, gks, gkw, tabs, ovt, B, T, tq=256, tk=512):
    nq = T // tq
    G, R = NSA_GROUPS, NSA_HPG
    ncp = k_c.shape[2]
    c, s = tabs
    qtab = pl.BlockSpec((tq, LANES), lambda b, g, i: (i, 0))
    ktab = pl.BlockSpec((T, LANES), lambda b, g, i: (0, 0))
    gspec = pl.BlockSpec((1, HEAD_PAD), lambda b, g, i: (0, 0))
    return pl.pallas_call(
        functools.partial(_nsa_kernel, tq=tq, tk=tk, T=T, scale=NSA_DK ** -0.5),
        grid=(B, G, nq),
        in_specs=[pl.BlockSpec((tq, R * HEAD_PAD), lambda b, g, i: (b * nq + i, g)),
                  pl.BlockSpec((T, HEAD_PAD), lambda b, g, i: (b, C_KS // HEAD_PAD + g)),
                  pl.BlockSpec((T, HEAD_PAD), lambda b, g, i: (b, C_KW // HEAD_PAD + g)),
                  pl.BlockSpec((T, DV), lambda b, g, i: (b, C_VS // DV + g)),
                  pl.BlockSpec((T, DV), lambda b, g, i: (b, C_VW // DV + g)),
                  pl.BlockSpec((tq, LANES), lambda b, g, i: (b * nq + i, C_GATE // LANES)),
                  pl.BlockSpec((1, 1, ncp, HEAD_PAD), lambda b, g, i: (b, g, 0, 0)),
                  pl.BlockSpec((1, 1, ncp, DV), lambda b, g, i: (b, g, 0, 0)),
                  gspec, gspec, gspec, qtab, qtab, ktab, ktab,
                  pl.BlockSpec(ovt.shape, lambda b, g, i: (0, 0))],
        out_specs=pl.BlockSpec((tq, R * DV), lambda b, g, i: (b * nq + i, g)),
        out_shape=jax.ShapeDtypeStruct((B * T, NSA_HEADS * DV), BF16),
        scratch_shapes=[pltpu.VMEM((T, HEAD_PAD), BF16), pltpu.VMEM((WINDOW + T, HEAD_PAD), BF16),
                        pltpu.VMEM((DV_EXT, T), BF16), pltpu.VMEM((DV_EXT, WINDOW + T), BF16),
                        pltpu.VMEM((DV, ncp), BF16), pltpu.VMEM((R * tq, HEAD_PAD), BF16),
                        pltpu.VMEM((LANES, tq), F32)]
        + _flash_scratch(R, tk, tq),
        compiler_params=_params(("parallel", "parallel", "arbitrary")),
        name="nsa_attention",
    )(proj, proj, proj, proj, proj, proj, k_c, v_c, gq, gks, gkw, c, s, c, s, ovt)


def _pad_cols(w, n):
    return jnp.pad(w, ((0, 0), (0, n - w.shape[1])))


def _take_cols(w, plan):
    zeros = lambda n: jnp.zeros((w.shape[0], n), w.dtype)
    return jnp.concatenate([zeros(p) if isinstance(p, int) else w[:, p[0]:p[1]] for p in plan], axis=1)


def _nsa_head_plan(c0):
    h = NSA_ROPE // 2
    fill = HALF_LANES - h
    return [(c0, c0 + h), (c0 + 2 * h, c0 + 2 * h + fill), (c0 + h, c0 + 2 * h),
            (c0 + 2 * h + fill, c0 + NSA_DK), HEAD_PAD - NSA_DK]


def _mla_rope_plan(c0):
    h = MLA_ROPE // 2
    return [(c0, c0 + h), HALF_LANES - h, (c0 + h, c0 + 2 * h), HALF_LANES - h]


def _mla_head_plan(c0):
    return [(c0, c0 + MLA_NOPE)] + _mla_rope_plan(c0 + MLA_NOPE)


def _heads(plan_fn, c0, nh, d):
    return [piece for h in range(nh) for piece in plan_fn(c0 + h * d)]


def _nsa_head_layout(w, nh):
    return _take_cols(w, _heads(_nsa_head_plan, 0, nh, NSA_DK))


def _mla_head_layout(w, nh):
    return _take_cols(w, _heads(_mla_head_plan, 0, nh, MLA_QK))


def _layout_w_in(w_in):
    sizes = (MLA_RANK, MLA_RANK, MLA_ROPE, NSA_HEADS * NSA_DK,
             NSA_GROUPS * NSA_DK, NSA_GROUPS * NSA_DV, NSA_GROUPS * NSA_DK, NSA_GROUPS * NSA_DV,
             NSA_GROUPS * NSA_DK, NSA_GROUPS * NSA_DV, NSA_HEADS * 3, 2 * D_MODEL)
    offs, o = [], 0
    for s in sizes:
        offs.append(o)
        o += s
    cq, ckv, kr, qn, kc, vc, ks, vs, kw, vw, gn, gm = offs
    whole = lambda c0, n: [(c0, c0 + n)]
    ngate = NSA_HPG * 3
    plan = (_heads(_nsa_head_plan, qn, NSA_HEADS, NSA_DK) + whole(gm, 2 * D_MODEL)
            + _heads(_nsa_head_plan, ks, NSA_GROUPS, NSA_DK) + _heads(_nsa_head_plan, kw, NSA_GROUPS, NSA_DK)
            + whole(vs, NSA_GROUPS * NSA_DV) + whole(vw, NSA_GROUPS * NSA_DV)
            + whole(cq, MLA_RANK) + whole(ckv, MLA_RANK) + _mla_rope_plan(kr)
            + _heads(lambda c: [(c, c + ngate), GATE_PITCH - ngate], gn, NSA_GROUPS, ngate)
            + [LANES - NSA_GROUPS * GATE_PITCH]
            + _heads(lambda c: [(c, c + NSA_DK), HEAD_PAD - NSA_DK], kc, NSA_GROUPS, NSA_DK)
            + whole(vc, NSA_GROUPS * NSA_DV))
    width = sum(p if isinstance(p, int) else p[1] - p[0] for p in plan)
    assert width == C_VC + NSA_GROUPS * NSA_DV <= C_END
    return _take_cols(w_in.astype(BF16), plan + [C_END - width])


def _rope_tables(pos, rot_dim):
    half = rot_dim // 2
    inv = ROPE_THETA ** (-np.arange(0, rot_dim, 2, dtype=np.float64) / rot_dim)
    ang = np.asarray(pos, np.float64)[:, None] * inv[None, :]
    c = np.ones((len(pos), LANES))
    s = np.zeros((len(pos), LANES))
    c[:, :half] = c[:, HALF_LANES:HALF_LANES + half] = np.cos(ang)
    s[:, :half] = -np.sin(ang)
    s[:, HALF_LANES:HALF_LANES + half] = np.sin(ang)
    return jnp.asarray(c, F32), jnp.asarray(s, F32)


def _overlap_table(ncp, nb):
    cs = np.arange(ncp)[None, :] * CMP_STRIDE
    ss = np.arange(nb)[:, None] * SLC_LEN
    ov = np.clip(np.minimum(cs + CMP_LEN, ss + SLC_LEN) - np.maximum(cs, ss), 0, None) / CMP_LEN
    return jnp.asarray(ov, BF16)


def _layer(x, attn_norm_g, w_in, mla_q_lat_g, mla_kv_lat_g, mla_w_uq, mla_w_uk, mla_w_uv,
           mla_q_norm_g, mla_k_norm_g, nsa_q_norm_g, nsa_k_norm_g, cmp_pe_k, cmp_pe_v,
           cmp_k_w1, cmp_k_b1, cmp_k_w2, cmp_k_b2, cmp_v_w1, cmp_v_b1, cmp_v_w2, cmp_v_b2,
           w_proj_mla, w_proj_nsa, w_out, ffn_norm_g, w_ffn_gate, w_ffn_up, w_ffn_down):
    B, T, _ = x.shape
    n = B * T
    ncp = T // CMP_STRIDE
    nb = T // SLC_LEN
    assert T % 512 == 0 and ncp % LANES == 0 and nb % 8 == 0 and SEL_LANE0 + nb <= PAD_LANE and T >= WINDOW
    x2 = x.reshape(n, D_MODEL)
    pos = np.arange(T)

    proj = _norm_matmul(x2, 0, D_MODEL, attn_norm_g.reshape(1, -1), _layout_w_in(w_in), 1024, 1024, "in_proj")

    q_up = _norm_matmul(proj, C_CQ // MLA_RANK, MLA_RANK, mla_q_lat_g.reshape(1, -1),
                        _mla_head_layout(mla_w_uq.astype(BF16), MLA_HEADS), 1024, 2048, "mla_q_up")
    kv_up = _norm_matmul(proj, C_CKV // MLA_RANK, MLA_RANK, mla_kv_lat_g.reshape(1, -1),
                         jnp.concatenate([mla_w_uk, mla_w_uv], axis=1).astype(BF16), 1024, 2048, "mla_kv_up")
    o_mla = _mla_attention(q_up, kv_up, proj, _mla_head_layout(mla_q_norm_g[None], 1),
                           _mla_head_layout(mla_k_norm_g[None], 1), _rope_tables(pos, MLA_ROPE), B, T)

    cmp_tabs = _rope_tables(np.arange(ncp) * CMP_STRIDE + CMP_LEN - 1, NSA_ROPE)
    k_c = _compress(proj, C_KC, NSA_DK, B, T, cmp_pe_k, cmp_k_w1, cmp_k_b1, _nsa_head_layout(cmp_k_w2, 1),
                    _nsa_head_layout(cmp_k_b2[None], 1)[0], _nsa_head_layout(nsa_k_norm_g[0:1], 1), cmp_tabs,
                    True, "compress_k")
    v_c = _compress(proj, C_VC, NSA_DV, B, T, cmp_pe_v, cmp_v_w1, cmp_v_b1, cmp_v_w2, cmp_v_b2,
                    jnp.zeros((1, NSA_DV), F32), cmp_tabs, False, "compress_v")
    o_nsa = _nsa_attention(proj, k_c, v_c, _nsa_head_layout(nsa_q_norm_g[None], 1),
                           _nsa_head_layout(nsa_k_norm_g[1:2], 1), _nsa_head_layout(nsa_k_norm_g[2:3], 1),
                           _rope_tables(pos, NSA_ROPE), _overlap_table(ncp, nb), B, T)

    u = _merge(o_mla, o_nsa, w_proj_mla.astype(BF16), w_proj_nsa.astype(BF16), proj)
    h, hn = _out_proj(u, w_out.astype(BF16), x2, ffn_norm_g.reshape(1, -1))
    act = _ffn_up(hn, w_ffn_gate.astype(BF16), w_ffn_up.astype(BF16))
    out = _ffn_down(act, w_ffn_down.astype(BF16), h)
    return out.reshape(B, T, D_MODEL)


def kernel(x, attn_norm_g, w_in, mla_q_lat_g, mla_kv_lat_g, mla_w_uq, mla_w_uk, mla_w_uv, mla_q_norm_g, mla_k_norm_g, nsa_q_norm_g, nsa_k_norm_g, cmp_pe_k, cmp_pe_v, cmp_k_w1, cmp_k_b1, cmp_k_w2, cmp_k_b2, cmp_v_w1, cmp_v_b1, cmp_v_w2, cmp_v_b2, w_proj_mla, w_proj_nsa, w_out, ffn_norm_g, w_ffn_gate, w_ffn_up, w_ffn_down):
    params = (attn_norm_g, w_in, mla_q_lat_g, mla_kv_lat_g, mla_w_uq, mla_w_uk, mla_w_uv, mla_q_norm_g,
              mla_k_norm_g, nsa_q_norm_g, nsa_k_norm_g, cmp_pe_k, cmp_pe_v, cmp_k_w1, cmp_k_b1, cmp_k_w2,
              cmp_k_b2, cmp_v_w1, cmp_v_b1, cmp_v_w2, cmp_v_b2, w_proj_mla, w_proj_nsa, w_out, ffn_norm_g,
              w_ffn_gate, w_ffn_up, w_ffn_down)
    h = x
    for l in range(attn_norm_g.shape[0]):
        h = _layer(h, *(p[l] for p in params))
    return h
```

```python
import functools

import numpy as np
import jax
import jax.numpy as jnp
from jax import lax
from jax.experimental import pallas as pl
from jax.experimental.pallas import tpu as pltpu

F32, BF16, I32 = jnp.float32, jnp.bfloat16, jnp.int32

EPS = 1e-6
NEG_INF = -1e30
FORCE = 1e9
ROPE_THETA = 500000.0

D_MODEL = 2048
MLA_HEADS, MLA_NOPE, MLA_ROPE, MLA_V = 16, 128, 64, 128
MLA_QK = MLA_NOPE + MLA_ROPE
MLA_RANK = 512
NSA_HEADS, NSA_GROUPS, NSA_DK, NSA_DV = 16, 4, 192, 128
NSA_HPG = NSA_HEADS // NSA_GROUPS
NSA_ROPE = NSA_DK // 4
CMP_LEN, CMP_STRIDE = 32, 16
SLC_LEN, SLC_TOPK, WINDOW = 64, 16, 512
D_FF = 5632

LANES = 128
HALF_LANES = 64
HEAD_PAD = 256
SEL_LANE0 = 64
PAD_LANE = 127
DV = 128
DV_EXT = DV + 16
VMEM_LIMIT = 48 * 1024 * 1024

C_QN, C_GM, C_KS, C_KW, C_VS, C_VW = 0, 4096, 8192, 9216, 10240, 10752
C_CQ, C_CKV, C_KR, C_GATE, C_KC, C_VC, C_END = 11264, 11776, 12288, 12416, 12544, 13568, 14336
GATE_PITCH = 16

_NT = (((1,), (1,)), ((), ()))


def _params(sem):
    return pltpu.CompilerParams(dimension_semantics=sem, vmem_limit_bytes=VMEM_LIMIT)


def _norm_matmul_kernel(x_ref, g_ref, w_ref, o_ref, xn_ref, *, w_t):
    @pl.when(pl.program_id(1) == 0)
    def _():
        x = x_ref[...]
        ms = jnp.mean(x * x, axis=-1, keepdims=True)
        xn_ref[...] = (x * lax.rsqrt(ms + EPS) * g_ref[...]).astype(BF16)

    if w_t:
        y = lax.dot_general(xn_ref[...], w_ref[...], _NT, preferred_element_type=F32)
    else:
        y = jnp.dot(xn_ref[...], w_ref[...], preferred_element_type=F32)
    o_ref[...] = y.astype(o_ref.dtype)


def _norm_matmul(x, xcol, k, g, w, tm, tn, name, w_t=False):
    n, m = x.shape[0], w.shape[0 if w_t else 1]
    return pl.pallas_call(
        functools.partial(_norm_matmul_kernel, w_t=w_t),
        grid=(n // tm, m // tn),
        in_specs=[pl.BlockSpec((tm, k), lambda i, j: (i, xcol)),
                  pl.BlockSpec((1, k), lambda i, j: (0, 0)),
                  pl.BlockSpec((tn, k), lambda i, j: (j, 0)) if w_t else pl.BlockSpec((k, tn), lambda i, j: (0, j))],
        out_specs=pl.BlockSpec((tm, tn), lambda i, j: (i, j)),
        out_shape=jax.ShapeDtypeStruct((n, m), F32),
        scratch_shapes=[pltpu.VMEM((tm, k), BF16)],
        compiler_params=_params(("parallel", "arbitrary")),
        name=name,
    )(x, g, w)


def _merge_kernel(om_ref, on_ref, wm_ref, wn_ref, g0_ref, g1_ref, o_ref):
    ym = jnp.dot(om_ref[...], wm_ref[...], preferred_element_type=F32)
    yn = jnp.dot(on_ref[...], wn_ref[...], preferred_element_type=F32)
    o_ref[...] = (jax.nn.sigmoid(g0_ref[...]) * ym + jax.nn.sigmoid(g1_ref[...]) * yn).astype(BF16)


def _merge(o_mla, o_nsa, w_pm, w_pn, proj, tm=1024, tn=512):
    n = o_mla.shape[0]
    g0, g1 = C_GM // tn, (C_GM + D_MODEL) // tn
    return pl.pallas_call(
        _merge_kernel,
        grid=(n // tm, D_MODEL // tn),
        in_specs=[pl.BlockSpec((tm, D_MODEL), lambda i, j: (i, 0)),
                  pl.BlockSpec((tm, D_MODEL), lambda i, j: (i, 0)),
                  pl.BlockSpec((D_MODEL, tn), lambda i, j: (0, j)),
                  pl.BlockSpec((D_MODEL, tn), lambda i, j: (0, j)),
                  pl.BlockSpec((tm, tn), lambda i, j: (i, g0 + j)),
                  pl.BlockSpec((tm, tn), lambda i, j: (i, g1 + j))],
        out_specs=pl.BlockSpec((tm, tn), lambda i, j: (i, j)),
        out_shape=jax.ShapeDtypeStruct((n, D_MODEL), BF16),
        compiler_params=_params(("parallel", "arbitrary")),
        name="merge",
    )(o_mla, o_nsa, w_pm, w_pn, proj, proj)


def _out_proj_kernel(u_ref, w_ref, x_ref, g_ref, h_ref, hn_ref):
    h = x_ref[...] + jnp.dot(u_ref[...], w_ref[...], preferred_element_type=F32)
    h_ref[...] = h
    ms = jnp.mean(h * h, axis=-1, keepdims=True)
    hn_ref[...] = (h * lax.rsqrt(ms + EPS) * g_ref[...]).astype(BF16)


def _out_proj(u, w_out, x, g, tm=512):
    n = u.shape[0]
    row = pl.BlockSpec((tm, D_MODEL), lambda i: (i, 0))
    return pl.pallas_call(
        _out_proj_kernel,
        grid=(n // tm,),
        in_specs=[row, pl.BlockSpec((D_MODEL, D_MODEL), lambda i: (0, 0)), row,
                  pl.BlockSpec((1, D_MODEL), lambda i: (0, 0))],
        out_specs=[row, row],
        out_shape=[jax.ShapeDtypeStruct((n, D_MODEL), F32), jax.ShapeDtypeStruct((n, D_MODEL), BF16)],
        compiler_params=_params(("parallel",)),
        name="out_proj",
    )(u, w_out, x, g)


def _ffn_up_kernel(hn_ref, wg_ref, wu_ref, o_ref):
    hn = hn_ref[...]
    a = jnp.dot(hn, wg_ref[...], preferred_element_type=F32)
    b = jnp.dot(hn, wu_ref[...], preferred_element_type=F32)
    o_ref[...] = (jax.nn.silu(a) * b).astype(BF16)


def _ffn_up(hn, w_gate, w_up, tm=1024, tn=512):
    n = hn.shape[0]
    return pl.pallas_call(
        _ffn_up_kernel,
        grid=(n // tm, D_FF // tn),
        in_specs=[pl.BlockSpec((tm, D_MODEL), lambda i, j: (i, 0)),
                  pl.BlockSpec((D_MODEL, tn), lambda i, j: (0, j)),
                  pl.BlockSpec((D_MODEL, tn), lambda i, j: (0, j))],
        out_specs=pl.BlockSpec((tm, tn), lambda i, j: (i, j)),
        out_shape=jax.ShapeDtypeStruct((n, D_FF), BF16),
        compiler_params=_params(("parallel", "arbitrary")),
        name="ffn_up",
    )(hn, w_gate, w_up)


def _ffn_down_kernel(a_ref, w_ref, h_ref, o_ref):
    o_ref[...] = h_ref[...] + jnp.dot(a_ref[...], w_ref[...], preferred_element_type=F32)


def _ffn_down(act, w_down, h, tm=1024, tn=512):
    n = act.shape[0]
    return pl.pallas_call(
        _ffn_down_kernel,
        grid=(n // tm, D_MODEL // tn),
        in_specs=[pl.BlockSpec((tm, D_FF), lambda i, j: (i, 0)),
                  pl.BlockSpec((D_FF, tn), lambda i, j: (0, j)),
                  pl.BlockSpec((tm, tn), lambda i, j: (i, j))],
        out_specs=pl.BlockSpec((tm, tn), lambda i, j: (i, j)),
        out_shape=jax.ShapeDtypeStruct((n, D_MODEL), F32),
        compiler_params=_params(("parallel", "arbitrary")),
        name="ffn_down",
    )(act, w_down, h)


def _rms_inv(xa, xb, d):
    return lax.rsqrt(jnp.sum(xa * xa + xb * xb, axis=-1, keepdims=True) * (1.0 / d) + EPS)


def _rope(x, c, s):
    return x * c + pltpu.roll(x, HALF_LANES, 1) * s


def _col_max(s, slabs=8):
    rows = s.shape[0]
    if rows % (8 * slabs) == 0:
        s = jnp.max(s.reshape(slabs, rows // slabs, s.shape[1]), axis=0)
    return jnp.max(s, axis=0, keepdims=True)


def _col_sum(s, slabs=8):
    rows = s.shape[0]
    if rows % (8 * slabs) == 0:
        s = jnp.sum(s.reshape(slabs, rows // slabs, s.shape[1]), axis=0)
    return jnp.sum(s, axis=0, keepdims=True)


def _flash_t(chains, n_last, n_max, tk, s_buf0, s_buf1, mt_buf0, mt_buf1, m_ref, acc_ref):
    nc = len(chains)
    s_buf, mt_buf = (s_buf0, s_buf1), (mt_buf0, mt_buf1)

    def stage_a(j, masked=True):
        off = j * tk
        for c, (q_at, k_at, _, t_cols) in enumerate(chains):
            s = lax.dot_general(k_at(off), q_at(), _NT, preferred_element_type=F32)
            if masked:
                pk = off + lax.broadcasted_iota(I32, (tk, 1), 0)
                s = jnp.where(pk <= t_cols, s, NEG_INF)
            s_buf[j % 2][c] = s
            mt_buf[j % 2][c] = _col_max(s)

    def stage_b(j):
        off = j * tk
        for c, (_, _, vt_at, _) in enumerate(chains):
            m = m_ref[c]
            m_new = jnp.maximum(m, mt_buf[j % 2][c])
            p = jnp.exp(s_buf[j % 2][c] - m_new).astype(BF16)
            acc_ref[c] = jnp.exp(m - m_new) * acc_ref[c] + jnp.dot(vt_at(off), p, preferred_element_type=F32)
            m_ref[c] = m_new

    m_ref[...] = jnp.full(m_ref.shape, NEG_INF, F32)
    acc_ref[...] = jnp.zeros(acc_ref.shape, F32)
    stage_a(0)
    for j in range(n_max):
        if j + 1 < n_max:
            @pl.when(j + 1 < n_last)
            def _(j=j):
                stage_a(j + 1, masked=False)
                stage_b(j)

            @pl.when(j + 1 == n_last)
            def _(j=j):
                stage_a(j + 1)
                stage_b(j)

        @pl.when(j == n_last)
        def _(j=j):
            stage_b(j)

    return [acc_ref[c, :DV] * (1.0 / acc_ref[c, DV:DV + 1]) for c in range(nc)]


def _flash_scratch(nc, tk, nq):
    return [pltpu.VMEM((nc, tk, nq), F32), pltpu.VMEM((nc, tk, nq), F32),
            pltpu.VMEM((nc, 1, nq), F32), pltpu.VMEM((nc, 1, nq), F32),
            pltpu.VMEM((nc, 1, nq), F32), pltpu.VMEM((nc, DV_EXT, nq), F32)]


def _mla_kernel(q_ref, kn_ref, kr_ref, v_ref, gq_ref, gk_ref, cq_ref, sq_ref, ck_ref, sk_ref,
                o_ref, k_s, vt_s, q_s, *flash_scratch, tq, tk, scale, hps):
    qi = pl.program_id(2)
    T = kn_ref.shape[0]

    @pl.when(qi == 0)
    def _():
        kr = kr_ref[...]
        ss_r = jnp.sum(kr * kr, axis=-1, keepdims=True)
        kr_rot = _rope(kr * gk_ref[:, LANES:], ck_ref[...], sk_ref[...])
        for h in range(hps):
            kn = kn_ref[:, h * LANES:(h + 1) * LANES]
            r = lax.rsqrt((jnp.sum(kn * kn, axis=-1, keepdims=True) + ss_r) * (1.0 / MLA_QK) + EPS)
            k_s[h, :, :LANES] = (kn * r * gk_ref[:, :LANES]).astype(BF16)
            k_s[h, :, LANES:] = (kr_rot * r).astype(BF16)
            vt_s[h, :DV, :] = v_ref[:, h * LANES:(h + 1) * LANES].T.astype(BF16)
            vt_s[h, DV:, :] = jnp.ones((DV_EXT - DV, T), BF16)

    t0 = qi * tq
    t_cols = t0 + lax.broadcasted_iota(I32, (1, tq), 1)
    chains = []
    for h in range(hps):
        qa = q_ref[:, h * HEAD_PAD:h * HEAD_PAD + LANES]
        qb = q_ref[:, h * HEAD_PAD + LANES:(h + 1) * HEAD_PAD]
        r = _rms_inv(qa, qb, MLA_QK) * scale
        qa = qa * r * gq_ref[:, :LANES]
        qb = _rope(qb * r * gq_ref[:, LANES:], cq_ref[...], sq_ref[...])
        q_s[h, :, :LANES] = qa.astype(BF16)
        q_s[h, :, LANES:] = qb.astype(BF16)
        chains.append((lambda h=h: q_s[h], lambda off, h=h: k_s[h, pl.ds(off, tk), :],
                       lambda off, h=h: vt_s[h, :, pl.ds(off, tk)], t_cols))
    for h, o_t in enumerate(_flash_t(chains, t0 // tk, T // tk, tk, *flash_scratch)):
        o_ref[:, h * LANES:(h + 1) * LANES] = o_t.T.astype(BF16)


def _mla_attention(q_up, kv_up, proj, gq, gk, tabs, B, T, tq=256, tk=512, hps=4):
    nq = T // tq
    ng = MLA_HEADS // hps
    c, s = tabs
    qtab = pl.BlockSpec((tq, LANES), lambda b, h, i: (i, 0))
    ktab = pl.BlockSpec((T, LANES), lambda b, h, i: (0, 0))
    gspec = pl.BlockSpec((1, HEAD_PAD), lambda b, h, i: (0, 0))
    return pl.pallas_call(
        functools.partial(_mla_kernel, tq=tq, tk=tk, scale=MLA_QK ** -0.5, hps=hps),
        grid=(B, ng, nq),
        in_specs=[pl.BlockSpec((tq, hps * HEAD_PAD), lambda b, h, i: (b * nq + i, h)),
                  pl.BlockSpec((T, hps * LANES), lambda b, h, i: (b, h)),
                  pl.BlockSpec((T, LANES), lambda b, h, i: (b, C_KR // LANES)),
                  pl.BlockSpec((T, hps * LANES), lambda b, h, i: (b, ng + h)),
                  gspec, gspec, qtab, qtab, ktab, ktab],
        out_specs=pl.BlockSpec((tq, hps * LANES), lambda b, h, i: (b * nq + i, h)),
        out_shape=jax.ShapeDtypeStruct((B * T, MLA_HEADS * MLA_V), BF16),
        scratch_shapes=[pltpu.VMEM((hps, T, HEAD_PAD), BF16), pltpu.VMEM((hps, DV_EXT, T), BF16),
                        pltpu.VMEM((hps, tq, HEAD_PAD), BF16)] + _flash_scratch(hps, tk, tq),
        compiler_params=_params(("parallel", "parallel", "arbitrary")),
        name="mla_attention",
    )(q_up, kv_up, proj, kv_up, gq, gk, c, s, c, s)


def _compress_kernel(*refs, is_key, gb, ncp, nt):
    x_tiles = refs[:gb * nt]
    pe_ref, w1a_ref, w1b_ref, b1_ref, w2_ref, b2_ref, g_ref, c_ref, s_ref, o_ref, xa_s, xb_s = refs[gb * nt:]
    rows = gb * ncp
    for g in range(gb):
        for l in range(CMP_STRIDE):
            for t in range(nt):
                x = x_tiles[g * nt + t][pl.ds(l, ncp, stride=CMP_STRIDE), :]
                lanes = slice((l * nt + t) * LANES, (l * nt + t + 1) * LANES)
                pe_lanes = slice(t * LANES, (t + 1) * LANES)
                xa_s[g * ncp:(g + 1) * ncp, lanes] = (x + pe_ref[l:l + 1, pe_lanes]).astype(BF16)
                xb_s[g * ncp:(g + 1) * ncp, lanes] = (
                    x + pe_ref[CMP_STRIDE + l:CMP_STRIDE + l + 1, pe_lanes]).astype(BF16)
    pa = jnp.dot(xa_s[...], w1a_ref[...], preferred_element_type=F32)
    pb = jnp.dot(xb_s[...], w1b_ref[...], preferred_element_type=F32)
    h = pa + pltpu.roll(pb, rows - 1, 0) + b1_ref[...]
    y = jnp.dot(jax.nn.gelu(h).astype(BF16), w2_ref[...], preferred_element_type=F32) + b2_ref[...]
    if is_key:
        ya, yb = y[:, :LANES], y[:, LANES:]
        r = _rms_inv(ya, yb, NSA_DK)
        c = jnp.concatenate([c_ref[...]] * gb, axis=0)
        s = jnp.concatenate([s_ref[...]] * gb, axis=0)
        ya = _rope(ya * r * g_ref[:, :LANES], c, s)
        yb = yb * r * g_ref[:, LANES:]
        y = jnp.concatenate([ya, yb], axis=1)
    o_ref[0] = y.astype(BF16).reshape(gb, ncp, y.shape[-1])


def _compress(proj, c0, d, B, T, pe, w1, b1, w2, b2, g, tabs, is_key, name, gb=2):
    G = NSA_GROUPS
    ncp = T // CMP_STRIDE
    nt = -(-d // LANES)
    dp = nt * LANES
    kd = CMP_STRIDE * dp
    dout = w2.shape[1]
    hid = w1.shape[1]
    w1 = jnp.pad(w1.astype(BF16).reshape(CMP_LEN, d, hid), ((0, 0), (0, dp - d), (0, 0))).reshape(CMP_LEN * dp, hid)
    pe = _pad_cols(pe, dp)
    c, s = tabs
    const = lambda shape: pl.BlockSpec(shape, lambda b, gi: (0,) * len(shape))
    tile = lambda k: pl.BlockSpec((T, LANES), lambda b, gi: (b, c0 // LANES + gi * gb * nt + k))
    return pl.pallas_call(
        functools.partial(_compress_kernel, is_key=is_key, gb=gb, ncp=ncp, nt=nt),
        grid=(B, G // gb),
        in_specs=[tile(k) for k in range(gb * nt)] + [
                  const((CMP_LEN, dp)), const((kd, hid)), const((kd, hid)), const((1, hid)),
                  const((hid, dout)), const((1, dout)), const((1, dout)),
                  const((ncp, LANES)), const((ncp, LANES))],
        out_specs=pl.BlockSpec((1, gb, ncp, dout), lambda b, gi: (b, gi, 0, 0)),
        out_shape=jax.ShapeDtypeStruct((B, G, ncp, dout), BF16),
        scratch_shapes=[pltpu.VMEM((gb * ncp, kd), BF16), pltpu.VMEM((gb * ncp, kd), BF16)],
        compiler_params=_params(("parallel", "parallel")),
        name=name,
    )(*([proj] * (gb * nt)), pe, w1[:kd], w1[kd:], b1.reshape(1, hid), w2.astype(BF16), b2.reshape(1, dout),
      g, c, s)


def _nsa_kernel(q_ref, ks_ref, kw_ref, vs_ref, vw_ref, gate_ref, kc_ref, vc_ref, gq_ref, gks_ref, gkw_ref,
                cq_ref, sq_ref, ck_ref, sk_ref, ovt_ref,
                o_ref, ks_s, kw_s, vst_s, vwt_s, vct_s, q_s, g_s, *flash_scratch, tq, tk, T, scale):
    qi = pl.program_id(2)
    R = NSA_HPG
    nq = R * tq
    nb = T // SLC_LEN
    topk = min(SLC_TOPK, nb)
    ncp = kc_ref.shape[2]

    @pl.when(qi == 0)
    def _():
        def prep_k(k_ref, g_ref):
            ka, kb = k_ref[:, :LANES], k_ref[:, LANES:]
            r = _rms_inv(ka, kb, NSA_DK)
            return _rope(ka * r * g_ref[:, :LANES], ck_ref[...], sk_ref[...]), kb * r * g_ref[:, LANES:]

        ka, kb = prep_k(ks_ref, gks_ref)
        blk = lax.broadcasted_iota(I32, (T, LANES), 0) // SLC_LEN
        lane = lax.broadcasted_iota(I32, (T, LANES), 1)
        ks_s[:, :LANES] = ka.astype(BF16)
        ks_s[:, LANES:] = (kb + (lane - SEL_LANE0 == blk).astype(F32)).astype(BF16)
        ka, kb = prep_k(kw_ref, gkw_ref)
        lane_w = lax.broadcasted_iota(I32, (WINDOW, LANES), 1)
        kw_s[:WINDOW, :LANES] = jnp.zeros((WINDOW, LANES), BF16)
        kw_s[:WINDOW, LANES:] = jnp.where(lane_w == PAD_LANE, NEG_INF, 0.0).astype(BF16)
        kw_s[WINDOW:, :LANES] = ka.astype(BF16)
        kw_s[WINDOW:, LANES:] = kb.astype(BF16)
        vst_s[:DV, :] = vs_ref[...].T.astype(BF16)
        vst_s[DV:, :] = jnp.ones((DV_EXT - DV, T), BF16)
        vwt_s[:DV, :WINDOW] = jnp.zeros((DV, WINDOW), BF16)
        vwt_s[:DV, WINDOW:] = vw_ref[...].T.astype(BF16)
        vwt_s[DV:, :] = jnp.ones((DV_EXT - DV, WINDOW + T), BF16)
        vct_s[...] = vc_ref[0, 0].astype(F32).T.astype(BF16)

    t0 = pl.multiple_of(qi * tq, tq)
    lane_q = lax.broadcasted_iota(I32, (tq, LANES), 1)
    for r in range(R):
        qa = q_ref[:, r * HEAD_PAD:r * HEAD_PAD + LANES]
        qb = q_ref[:, r * HEAD_PAD + LANES:(r + 1) * HEAD_PAD]
        rs = _rms_inv(qa, qb, NSA_DK) * scale
        qa = _rope(qa * rs * gq_ref[:, :LANES], cq_ref[...], sq_ref[...])
        qb = jnp.where(lane_q == PAD_LANE, 1.0, qb * rs * gq_ref[:, LANES:])
        q_s[r * tq:(r + 1) * tq, :LANES] = qa.astype(BF16)
        q_s[r * tq:(r + 1) * tq, LANES:] = qb.astype(BF16)

    c_q = lax.broadcasted_iota(I32, (1, nq), 1) & (tq - 1)
    t_cols = t0 + c_q
    q = q_s[...]

    g_s[...] = jax.nn.sigmoid(gate_ref[...]).T

    span = tq + WINDOW
    sw = lax.dot_general(kw_s[pl.ds(t0, span), :], q, _NT, preferred_element_type=F32)
    sc = lax.dot_general(kc_ref[0, 0], q, _NT, preferred_element_type=F32)

    cend = lax.broadcasted_iota(I32, (ncp, 1), 0) * CMP_STRIDE + (CMP_LEN - 1)
    mask_c = cend <= t_cols
    sc = jnp.where(mask_c, sc, NEG_INF)
    e = jnp.where(mask_c, jnp.exp(sc - _col_max(sc)), 0.0)
    p = e * (1.0 / jnp.maximum(_col_sum(e), 1e-30))
    o_c = jnp.dot(vct_s[...], p.astype(BF16), preferred_element_type=F32)

    ps = p[:, :tq]
    for r in range(1, R):
        ps = ps + p[:, r * tq:(r + 1) * tq]
    ps_hi = ps.astype(BF16)
    ps_lo = (ps - ps_hi.astype(F32)).astype(BF16)
    imp = (jnp.dot(ovt_ref[...], ps_hi, preferred_element_type=F32)
           + jnp.dot(ovt_ref[...], ps_lo, preferred_element_type=F32))
    jb = lax.broadcasted_iota(I32, (nb, tq), 0)
    tb = (t0 + lax.broadcasted_iota(I32, (1, tq), 1)) // SLC_LEN
    forced = (jb == 0) | (jb == tb) | (jb == tb - 1)
    score = jnp.where(forced, FORCE, jnp.where(jb <= tb, imp, -FORCE))
    rank = jnp.zeros((nb, tq), I32)
    for j in range(nb):
        row = score[j:j + 1, :]
        rank = rank + ((row > score) | ((row == score) & (j < jb))).astype(I32)
    unsel = (rank >= topk).astype(F32)
    unsel = jnp.concatenate([jnp.zeros((SEL_LANE0, tq), F32), unsel,
                             jnp.zeros((LANES - SEL_LANE0 - nb, tq), F32)], axis=0)
    bias = unsel.T * NEG_INF

    for r in range(R):
        qb = q_s[r * tq:(r + 1) * tq, LANES:].astype(F32)
        q_s[r * tq:(r + 1) * tq, LANES:] = (qb + bias).astype(BF16)

    i_k = lax.broadcasted_iota(I32, (tq, 1), 0)
    s_lo = jnp.where(i_k > c_q, sw[:tq], NEG_INF)
    s_mid = sw[tq:WINDOW]
    s_hi = jnp.where(i_k <= c_q, sw[WINDOW:], NEG_INF)
    m = jnp.maximum(jnp.maximum(_col_max(s_lo), _col_max(s_mid)), _col_max(s_hi))
    pw = jnp.concatenate([jnp.exp(s_lo - m), jnp.exp(s_mid - m), jnp.exp(s_hi - m)], axis=0).astype(BF16)
    acc = jnp.dot(vwt_s[:, pl.ds(t0, span)], pw, preferred_element_type=F32)
    o_w = acc[:DV] * (1.0 / acc[DV:DV + 1])

    k_at = lambda off: ks_s[pl.ds(off, tk), :]
    vt_at = lambda off: vst_s[:, pl.ds(off, tk)]
    chains = [(lambda r=r: q_s[r * tq:(r + 1) * tq], k_at, vt_at, t_cols[:, r * tq:(r + 1) * tq])
              for r in range(R)]
    o_s = jnp.concatenate(_flash_t(chains, t0 // tk, T // tk, tk, *flash_scratch), axis=1)

    gs = g_s[pl.ds(pl.multiple_of(pl.program_id(1) * GATE_PITCH, GATE_PITCH), GATE_PITCH), :]
    for r in range(R):
        cols = slice(r * tq, (r + 1) * tq)
        o = (gs[3 * r:3 * r + 1] * o_c[:, cols] + gs[3 * r + 1:3 * r + 2] * o_s[:, cols]
             + gs[3 * r + 2:3 * r + 3] * o_w[:, cols])
        o_ref[:, r * DV:(r + 1) * DV] = o.T.astype(BF16)


def _nsa_attention(proj, k_c, v_c, gq, gks, gkw, tabs, ovt, B, T, tq=256, tk=512):
    nq = T // tq
    G, R = NSA_GROUPS, NSA_HPG
    ncp = k_c.shape[2]
    c, s = tabs
    qtab = pl.BlockSpec((tq, LANES), lambda b, g, i: (i, 0))
    ktab = pl.BlockSpec((T, LANES), lambda b, g, i: (0, 0))
    gspec = pl.BlockSpec((1, HEAD_PAD), lambda b, g, i: (0, 0))
    return pl.pallas_call(
        functools.partial(_nsa_kernel, tq=tq, tk=tk, T=T, scale=NSA_DK ** -0.5),
        grid=(B, G, nq),
        in_specs=[pl.BlockSpec((tq, R * HEAD_PAD), lambda b, g, i: (b * nq + i, g)),
                  pl.BlockSpec((T, HEAD_PAD), lambda b, g, i: (b, C_KS // HEAD_PAD + g)),
                  pl.BlockSpec((T, HEAD_PAD), lambda b, g, i: (b, C_KW // HEAD_PAD + g)),
                  pl.BlockSpec((T, DV), lambda b, g, i: (b, C_VS // DV + g)),
                  pl.BlockSpec((T, DV), lambda b, g, i: (b, C_VW // DV + g)),
                  pl.BlockSpec((tq, LANES), lambda b, g, i: (b * nq + i, C_GATE // LANES)),
                  pl.BlockSpec((1, 1, ncp, HEAD_PAD), lambda b, g, i: (b, g, 0, 0)),
                  pl.BlockSpec((1, 1, ncp, DV), lambda b, g, i: (b, g, 0, 0)),
                  gspec, gspec, gspec, qtab, qtab, ktab, ktab,
                  pl.BlockSpec(ovt.shape, lambda b, g, i: (0, 0))],
        out_specs=pl.BlockSpec((tq, R * DV), lambda b, g, i: (b * nq + i, g)),
        out_shape=jax.ShapeDtypeStruct((B * T, NSA_HEADS * DV), BF16),
        scratch_shapes=[pltpu.VMEM((T, HEAD_PAD), BF16), pltpu.VMEM((WINDOW + T, HEAD_PAD), BF16),
                        pltpu.VMEM((DV_EXT, T), BF16), pltpu.VMEM((DV_EXT, WINDOW + T), BF16),
                        pltpu.VMEM((DV, ncp), BF16), pltpu.VMEM((R * tq, HEAD_PAD), BF16),
                        pltpu.VMEM((LANES, tq), F32)]
        + _flash_scratch(R, tk, tq),
        compiler_params=_params(("parallel", "parallel", "arbitrary")),
        name="nsa_attention",
    )(proj, proj, proj, proj, proj, proj, k_c, v_c, gq, gks, gkw, c, s, c, s, ovt)


def _pad_cols(w, n):
    return jnp.pad(w, ((0, 0), (0, n - w.shape[1])))


def _take_cols(w, plan, axis=1):
    other = w.shape[1 - axis]
    zeros = lambda n: jnp.zeros((other, n) if axis == 1 else (n, other), w.dtype)
    take = lambda a, b: w[:, a:b] if axis == 1 else w[a:b]
    return jnp.concatenate([zeros(p) if isinstance(p, int) else take(*p) for p in plan], axis=axis)


def _nsa_head_plan(c0):
    h = NSA_ROPE // 2
    fill = HALF_LANES - h
    return [(c0, c0 + h), (c0 + 2 * h, c0 + 2 * h + fill), (c0 + h, c0 + 2 * h),
            (c0 + 2 * h + fill, c0 + NSA_DK), HEAD_PAD - NSA_DK]


def _mla_rope_plan(c0):
    h = MLA_ROPE // 2
    return [(c0, c0 + h), HALF_LANES - h, (c0 + h, c0 + 2 * h), HALF_LANES - h]


def _mla_head_plan(c0):
    return [(c0, c0 + MLA_NOPE)] + _mla_rope_plan(c0 + MLA_NOPE)


def _heads(plan_fn, c0, nh, d):
    return [piece for h in range(nh) for piece in plan_fn(c0 + h * d)]


def _nsa_head_layout(w, nh):
    return _take_cols(w, _heads(_nsa_head_plan, 0, nh, NSA_DK))


def _mla_head_layout(w, nh):
    return _take_cols(w, _heads(_mla_head_plan, 0, nh, MLA_QK))


def _layout_w_in(w_in):
    sizes = (MLA_RANK, MLA_RANK, MLA_ROPE, NSA_HEADS * NSA_DK,
             NSA_GROUPS * NSA_DK, NSA_GROUPS * NSA_DV, NSA_GROUPS * NSA_DK, NSA_GROUPS * NSA_DV,
             NSA_GROUPS * NSA_DK, NSA_GROUPS * NSA_DV, NSA_HEADS * 3, 2 * D_MODEL)
    offs, o = [], 0
    for s in sizes:
        offs.append(o)
        o += s
    cq, ckv, kr, qn, kc, vc, ks, vs, kw, vw, gn, gm = offs
    whole = lambda c0, n: [(c0, c0 + n)]
    ngate = NSA_HPG * 3
    plan = (_heads(_nsa_head_plan, qn, NSA_HEADS, NSA_DK) + whole(gm, 2 * D_MODEL)
            + _heads(_nsa_head_plan, ks, NSA_GROUPS, NSA_DK) + _heads(_nsa_head_plan, kw, NSA_GROUPS, NSA_DK)
            + whole(vs, NSA_GROUPS * NSA_DV) + whole(vw, NSA_GROUPS * NSA_DV)
            + whole(cq, MLA_RANK) + whole(ckv, MLA_RANK) + _mla_rope_plan(kr)
            + _heads(lambda c: [(c, c + ngate), GATE_PITCH - ngate], gn, NSA_GROUPS, ngate)
            + [LANES - NSA_GROUPS * GATE_PITCH]
            + _heads(lambda c: [(c, c + NSA_DK), HEAD_PAD - NSA_DK], kc, NSA_GROUPS, NSA_DK)
            + whole(vc, NSA_GROUPS * NSA_DV))
    width = sum(p if isinstance(p, int) else p[1] - p[0] for p in plan)
    assert width == C_VC + NSA_GROUPS * NSA_DV <= C_END
    return _take_cols(w_in.astype(BF16).T, plan + [C_END - width], axis=0)


def _rope_tables(pos, rot_dim):
    half = rot_dim // 2
    inv = ROPE_THETA ** (-np.arange(0, rot_dim, 2, dtype=np.float64) / rot_dim)
    ang = np.asarray(pos, np.float64)[:, None] * inv[None, :]
    c = np.ones((len(pos), LANES))
    s = np.zeros((len(pos), LANES))
    c[:, :half] = c[:, HALF_LANES:HALF_LANES + half] = np.cos(ang)
    s[:, :half] = -np.sin(ang)
    s[:, HALF_LANES:HALF_LANES + half] = np.sin(ang)
    return jnp.asarray(c, F32), jnp.asarray(s, F32)


def _overlap_table(ncp, nb):
    cs = np.arange(ncp)[None, :] * CMP_STRIDE
    ss = np.arange(nb)[:, None] * SLC_LEN
    ov = np.clip(np.minimum(cs + CMP_LEN, ss + SLC_LEN) - np.maximum(cs, ss), 0, None) / CMP_LEN
    return jnp.asarray(ov, BF16)


def _layer(x, attn_norm_g, w_in, mla_q_lat_g, mla_kv_lat_g, mla_w_uq, mla_w_uk, mla_w_uv,
           mla_q_norm_g, mla_k_norm_g, nsa_q_norm_g, nsa_k_norm_g, cmp_pe_k, cmp_pe_v,
           cmp_k_w1, cmp_k_b1, cmp_k_w2, cmp_k_b2, cmp_v_w1, cmp_v_b1, cmp_v_w2, cmp_v_b2,
           w_proj_mla, w_proj_nsa, w_out, ffn_norm_g, w_ffn_gate, w_ffn_up, w_ffn_down):
    B, T, _ = x.shape
    n = B * T
    ncp = T // CMP_STRIDE
    nb = T // SLC_LEN
    assert T % 512 == 0 and ncp % LANES == 0 and nb % 8 == 0 and SEL_LANE0 + nb <= PAD_LANE and T >= WINDOW
    x2 = x.reshape(n, D_MODEL)
    pos = np.arange(T)

    proj = _norm_matmul(x2, 0, D_MODEL, attn_norm_g.reshape(1, -1), _layout_w_in(w_in), 1024, 1024, "in_proj",
                        w_t=True)

    q_up = _norm_matmul(proj, C_CQ // MLA_RANK, MLA_RANK, mla_q_lat_g.reshape(1, -1),
                        _mla_head_layout(mla_w_uq.astype(BF16), MLA_HEADS), 1024, 2048, "mla_q_up")
    kv_up = _norm_matmul(proj, C_CKV // MLA_RANK, MLA_RANK, mla_kv_lat_g.reshape(1, -1),
                         jnp.concatenate([mla_w_uk, mla_w_uv], axis=1).astype(BF16), 1024, 2048, "mla_kv_up")
    o_mla = _mla_attention(q_up, kv_up, proj, _mla_head_layout(mla_q_norm_g[None], 1),
                           _mla_head_layout(mla_k_norm_g[None], 1), _rope_tables(pos, MLA_ROPE), B, T)

    cmp_tabs = _rope_tables(np.arange(ncp) * CMP_STRIDE + CMP_LEN - 1, NSA_ROPE)
    k_c = _compress(proj, C_KC, NSA_DK, B, T, cmp_pe_k, cmp_k_w1, cmp_k_b1, _nsa_head_layout(cmp_k_w2, 1),
                    _nsa_head_layout(cmp_k_b2[None], 1)[0], _nsa_head_layout(nsa_k_norm_g[0:1], 1), cmp_tabs,
                    True, "compress_k")
    v_c = _compress(proj, C_VC, NSA_DV, B, T, cmp_pe_v, cmp_v_w1, cmp_v_b1, cmp_v_w2, cmp_v_b2,
                    jnp.zeros((1, NSA_DV), F32), cmp_tabs, False, "compress_v")
    o_nsa = _nsa_attention(proj, k_c, v_c, _nsa_head_layout(nsa_q_norm_g[None], 1),
                           _nsa_head_layout(nsa_k_norm_g[1:2], 1), _nsa_head_layout(nsa_k_norm_g[2:3], 1),
                           _rope_tables(pos, NSA_ROPE), _overlap_table(ncp, nb), B, T)

    u = _merge(o_mla, o_nsa, w_proj_mla.astype(BF16), w_proj_nsa.astype(BF16), proj)
    h, hn = _out_proj(u, w_out.astype(BF16), x2, ffn_norm_g.reshape(1, -1))
    act = _ffn_up(hn, w_ffn_gate.astype(BF16), w_ffn_up.astype(BF16))
    out = _ffn_down(act, w_ffn_down.astype(BF16), h)
    return out.reshape(B, T, D_MODEL)


def kernel(x, attn_norm_g, w_in, mla_q_lat_g, mla_kv_lat_g, mla_w_uq, mla_w_uk, mla_w_uv, mla_q_norm_g, mla_k_norm_g, nsa_q_norm_g, nsa_k_norm_g, cmp_pe_k, cmp_pe_v, cmp_k_w1, cmp_k_b1, cmp_k_w2, cmp_k_b2, cmp_v_w1, cmp_v_b1, cmp_v_w2, cmp_v_b2, w_proj_mla, w_proj_nsa, w_out, ffn_norm_g, w_ffn_gate, w_ffn_up, w_ffn_down):
    params = (attn_norm_g, w_in, mla_q_lat_g, mla_kv_lat_g, mla_w_uq, mla_w_uk, mla_w_uv, mla_q_norm_g,
              mla_k_norm_g, nsa_q_norm_g, nsa_k_norm_g, cmp_pe_k, cmp_pe_v, cmp_k_w1, cmp_k_b1, cmp_k_w2,
              cmp_k_b2, cmp_v_w1, cmp_v_b1, cmp_v_w2, cmp_v_b2, w_proj_mla, w_proj_nsa, w_out, ffn_norm_g,
              w_ffn_gate, w_ffn_up, w_ffn_down)
    h = x
    for l in range(attn_norm_g.shape[0]):
        h = _layer(h, *(p[l] for p in params))
    return h
```

```python
import functools

import numpy as np
import jax
import jax.numpy as jnp
from jax import lax
from jax.experimental import pallas as pl
from jax.experimental.pallas import tpu as pltpu

F32, BF16, I32 = jnp.float32, jnp.bfloat16, jnp.int32

EPS = 1e-6
NEG_INF = -1e30
FORCE = 1e9
ROPE_THETA = 500000.0

D_MODEL = 2048
MLA_HEADS, MLA_NOPE, MLA_ROPE, MLA_V = 16, 128, 64, 128
MLA_QK = MLA_NOPE + MLA_ROPE
MLA_RANK = 512
NSA_HEADS, NSA_GROUPS, NSA_DK, NSA_DV = 16, 4, 192, 128
NSA_HPG = NSA_HEADS // NSA_GROUPS
NSA_ROPE = NSA_DK // 4
CMP_LEN, CMP_STRIDE = 32, 16
SLC_LEN, SLC_TOPK, WINDOW = 64, 16, 512
D_FF = 5632

LANES = 128
HALF_LANES = 64
HEAD_PAD = 256
SEL_LANE0 = 64
PAD_LANE = 127
DV = 128
DV_EXT = DV + 16
VMEM_LIMIT = 48 * 1024 * 1024

C_QN, C_GM, C_KS, C_KW, C_VS, C_VW = 0, 4096, 8192, 9216, 10240, 10752
C_CQ, C_CKV, C_KR, C_GATE, C_KC, C_VC, C_END = 11264, 11776, 12288, 12416, 12544, 13568, 14336
GATE_PITCH = 16

_NT = (((1,), (1,)), ((), ()))


def _params(sem):
    return pltpu.CompilerParams(dimension_semantics=sem, vmem_limit_bytes=VMEM_LIMIT)


def _norm_matmul_kernel(x_ref, g_ref, w_ref, o_ref, xn_ref):
    @pl.when(pl.program_id(1) == 0)
    def _():
        x = x_ref[...]
        ms = jnp.mean(x * x, axis=-1, keepdims=True)
        xn_ref[...] = (x * lax.rsqrt(ms + EPS) * g_ref[...]).astype(BF16)

    o_ref[...] = jnp.dot(xn_ref[...], w_ref[...], preferred_element_type=F32).astype(o_ref.dtype)


def _norm_matmul(x, xcol, k, g, w, tm, tn, name):
    n, m = x.shape[0], w.shape[1]
    return pl.pallas_call(
        _norm_matmul_kernel,
        grid=(n // tm, m // tn),
        in_specs=[pl.BlockSpec((tm, k), lambda i, j: (i, xcol)),
                  pl.BlockSpec((1, k), lambda i, j: (0, 0)),
                  pl.BlockSpec((k, tn), lambda i, j: (0, j))],
        out_specs=pl.BlockSpec((tm, tn), lambda i, j: (i, j)),
        out_shape=jax.ShapeDtypeStruct((n, m), F32),
        scratch_shapes=[pltpu.VMEM((tm, k), BF16)],
        compiler_params=_params(("parallel", "arbitrary")),
        name=name,
    )(x, g, w)


def _merge_kernel(om_ref, on_ref, wm_ref, wn_ref, g0_ref, g1_ref, o_ref):
    ym = jnp.dot(om_ref[...], wm_ref[...], preferred_element_type=F32)
    yn = jnp.dot(on_ref[...], wn_ref[...], preferred_element_type=F32)
    o_ref[...] = (jax.nn.sigmoid(g0_ref[...]) * ym + jax.nn.sigmoid(g1_ref[...]) * yn).astype(BF16)


def _merge(o_mla, o_nsa, w_pm, w_pn, proj, tm=1024, tn=512):
    n = o_mla.shape[0]
    g0, g1 = C_GM // tn, (C_GM + D_MODEL) // tn
    return pl.pallas_call(
        _merge_kernel,
        grid=(n // tm, D_MODEL // tn),
        in_specs=[pl.BlockSpec((tm, D_MODEL), lambda i, j: (i, 0)),
                  pl.BlockSpec((tm, D_MODEL), lambda i, j: (i, 0)),
                  pl.BlockSpec((D_MODEL, tn), lambda i, j: (0, j)),
                  pl.BlockSpec((D_MODEL, tn), lambda i, j: (0, j)),
                  pl.BlockSpec((tm, tn), lambda i, j: (i, g0 + j)),
                  pl.BlockSpec((tm, tn), lambda i, j: (i, g1 + j))],
        out_specs=pl.BlockSpec((tm, tn), lambda i, j: (i, j)),
        out_shape=jax.ShapeDtypeStruct((n, D_MODEL), BF16),
        compiler_params=_params(("parallel", "arbitrary")),
        name="merge",
    )(o_mla, o_nsa, w_pm, w_pn, proj, proj)


def _out_proj_kernel(u_ref, w_ref, x_ref, g_ref, h_ref, hn_ref):
    h = x_ref[...] + jnp.dot(u_ref[...], w_ref[...], preferred_element_type=F32)
    h_ref[...] = h
    ms = jnp.mean(h * h, axis=-1, keepdims=True)
    hn_ref[...] = (h * lax.rsqrt(ms + EPS) * g_ref[...]).astype(BF16)


def _out_proj(u, w_out, x, g, tm=512):
    n = u.shape[0]
    row = pl.BlockSpec((tm, D_MODEL), lambda i: (i, 0))
    return pl.pallas_call(
        _out_proj_kernel,
        grid=(n // tm,),
        in_specs=[row, pl.BlockSpec((D_MODEL, D_MODEL), lambda i: (0, 0)), row,
                  pl.BlockSpec((1, D_MODEL), lambda i: (0, 0))],
        out_specs=[row, row],
        out_shape=[jax.ShapeDtypeStruct((n, D_MODEL), F32), jax.ShapeDtypeStruct((n, D_MODEL), BF16)],
        compiler_params=_params(("parallel",)),
        name="out_proj",
    )(u, w_out, x, g)


def _ffn_up_kernel(hn_ref, wg_ref, wu_ref, o_ref):
    hn = hn_ref[...]
    a = jnp.dot(hn, wg_ref[...], preferred_element_type=F32)
    b = jnp.dot(hn, wu_ref[...], preferred_element_type=F32)
    o_ref[...] = (jax.nn.silu(a) * b).astype(BF16)


def _ffn_up(hn, w_gate, w_up, tm=1024, tn=512):
    n = hn.shape[0]
    return pl.pallas_call(
        _ffn_up_kernel,
        grid=(n // tm, D_FF // tn),
        in_specs=[pl.BlockSpec((tm, D_MODEL), lambda i, j: (i, 0)),
                  pl.BlockSpec((D_MODEL, tn), lambda i, j: (0, j)),
                  pl.BlockSpec((D_MODEL, tn), lambda i, j: (0, j))],
        out_specs=pl.BlockSpec((tm, tn), lambda i, j: (i, j)),
        out_shape=jax.ShapeDtypeStruct((n, D_FF), BF16),
        compiler_params=_params(("parallel", "arbitrary")),
        name="ffn_up",
    )(hn, w_gate, w_up)


def _ffn_down_kernel(a_ref, w_ref, h_ref, o_ref):
    o_ref[...] = h_ref[...] + jnp.dot(a_ref[...], w_ref[...], preferred_element_type=F32)


def _ffn_down(act, w_down, h, tm=1024, tn=512):
    n = act.shape[0]
    return pl.pallas_call(
        _ffn_down_kernel,
        grid=(n // tm, D_MODEL // tn),
        in_specs=[pl.BlockSpec((tm, D_FF), lambda i, j: (i, 0)),
                  pl.BlockSpec((D_FF, tn), lambda i, j: (0, j)),
                  pl.BlockSpec((tm, tn), lambda i, j: (i, j))],
        out_specs=pl.BlockSpec((tm, tn), lambda i, j: (i, j)),
        out_shape=jax.ShapeDtypeStruct((n, D_MODEL), F32),
        compiler_params=_params(("parallel", "arbitrary")),
        name="ffn_down",
    )(act, w_down, h)


def _rms_inv(xa, xb, d):
    return lax.rsqrt(jnp.sum(xa * xa + xb * xb, axis=-1, keepdims=True) * (1.0 / d) + EPS)


def _rope(x, c, s):
    return x * c + pltpu.roll(x, HALF_LANES, 1) * s


def _col_max(s, slabs=8):
    rows = s.shape[0]
    if rows % (8 * slabs) == 0:
        s = jnp.max(s.reshape(slabs, rows // slabs, s.shape[1]), axis=0)
    return jnp.max(s, axis=0, keepdims=True)


def _col_sum(s, slabs=8):
    rows = s.shape[0]
    if rows % (8 * slabs) == 0:
        s = jnp.sum(s.reshape(slabs, rows // slabs, s.shape[1]), axis=0)
    return jnp.sum(s, axis=0, keepdims=True)


def _flash_t(chains, n_last, n_max, tk, s_buf0, s_buf1, mt_buf0, mt_buf1, m_ref, acc_ref):
    nc = len(chains)
    s_buf, mt_buf = (s_buf0, s_buf1), (mt_buf0, mt_buf1)

    def stage_a(j, masked=True):
        off = j * tk
        for c, (q_at, k_at, _, t_cols) in enumerate(chains):
            s = lax.dot_general(k_at(off), q_at(), _NT, preferred_element_type=F32)
            if masked:
                pk = off + lax.broadcasted_iota(I32, (tk, 1), 0)
                s = jnp.where(pk <= t_cols, s, NEG_INF)
            s_buf[j % 2][c] = s
            mt_buf[j % 2][c] = _col_max(s)

    def stage_b(j):
        off = j * tk
        for c, (_, _, vt_at, _) in enumerate(chains):
            m = m_ref[c]
            m_new = jnp.maximum(m, mt_buf[j % 2][c])
            p = jnp.exp(s_buf[j % 2][c] - m_new).astype(BF16)
            acc_ref[c] = jnp.exp(m - m_new) * acc_ref[c] + jnp.dot(vt_at(off), p, preferred_element_type=F32)
            m_ref[c] = m_new

    m_ref[...] = jnp.full(m_ref.shape, NEG_INF, F32)
    acc_ref[...] = jnp.zeros(acc_ref.shape, F32)
    stage_a(0)
    for j in range(n_max):
        if j + 1 < n_max:
            @pl.when(j + 1 < n_last)
            def _(j=j):
                stage_a(j + 1, masked=False)
                stage_b(j)

            @pl.when(j + 1 == n_last)
            def _(j=j):
                stage_a(j + 1)
                stage_b(j)

        @pl.when(j == n_last)
        def _(j=j):
            stage_b(j)

    return [acc_ref[c, :DV] * (1.0 / acc_ref[c, DV:DV + 1]) for c in range(nc)]


def _flash_scratch(nc, tk, nq):
    return [pltpu.VMEM((nc, tk, nq), F32), pltpu.VMEM((nc, tk, nq), F32),
            pltpu.VMEM((nc, 1, nq), F32), pltpu.VMEM((nc, 1, nq), F32),
            pltpu.VMEM((nc, 1, nq), F32), pltpu.VMEM((nc, DV_EXT, nq), F32)]


def _mla_kernel(q_ref, kn_ref, kr_ref, v_ref, gq_ref, gk_ref, cq_ref, sq_ref, ck_ref, sk_ref,
                o_ref, k_s, vt_s, q_s, *flash_scratch, tq, tk, scale, hps):
    qi = pl.program_id(2)
    T = kn_ref.shape[0]

    @pl.when(qi == 0)
    def _():
        kr = kr_ref[...]
        ss_r = jnp.sum(kr * kr, axis=-1, keepdims=True)
        kr_rot = _rope(kr * gk_ref[:, LANES:], ck_ref[...], sk_ref[...])
        for h in range(hps):
            kn = kn_ref[:, h * LANES:(h + 1) * LANES]
            r = lax.rsqrt((jnp.sum(kn * kn, axis=-1, keepdims=True) + ss_r) * (1.0 / MLA_QK) + EPS)
            k_s[h, :, :LANES] = (kn * r * gk_ref[:, :LANES]).astype(BF16)
            k_s[h, :, LANES:] = (kr_rot * r).astype(BF16)
            vt_s[h, :DV, :] = v_ref[:, h * LANES:(h + 1) * LANES].T.astype(BF16)
            vt_s[h, DV:, :] = jnp.ones((DV_EXT - DV, T), BF16)

    t0 = qi * tq
    t_cols = t0 + lax.broadcasted_iota(I32, (1, tq), 1)
    chains = []
    for h in range(hps):
        qa = q_ref[:, h * HEAD_PAD:h * HEAD_PAD + LANES]
        qb = q_ref[:, h * HEAD_PAD + LANES:(h + 1) * HEAD_PAD]
        r = _rms_inv(qa, qb, MLA_QK) * scale
        qa = qa * r * gq_ref[:, :LANES]
        qb = _rope(qb * r * gq_ref[:, LANES:], cq_ref[...], sq_ref[...])
        q_s[h, :, :LANES] = qa.astype(BF16)
        q_s[h, :, LANES:] = qb.astype(BF16)
        chains.append((lambda h=h: q_s[h], lambda off, h=h: k_s[h, pl.ds(off, tk), :],
                       lambda off, h=h: vt_s[h, :, pl.ds(off, tk)], t_cols))
    for h, o_t in enumerate(_flash_t(chains, t0 // tk, T // tk, tk, *flash_scratch)):
        o_ref[:, h * LANES:(h + 1) * LANES] = o_t.T.astype(BF16)


def _mla_attention(q_up, kv_up, proj, gq, gk, tabs, B, T, tq=256, tk=512, hps=4):
    nq = T // tq
    ng = MLA_HEADS // hps
    c, s = tabs
    qtab = pl.BlockSpec((tq, LANES), lambda b, h, i: (i, 0))
    ktab = pl.BlockSpec((T, LANES), lambda b, h, i: (0, 0))
    gspec = pl.BlockSpec((1, HEAD_PAD), lambda b, h, i: (0, 0))
    return pl.pallas_call(
        functools.partial(_mla_kernel, tq=tq, tk=tk, scale=MLA_QK ** -0.5, hps=hps),
        grid=(B, ng, nq),
        in_specs=[pl.BlockSpec((tq, hps * HEAD_PAD), lambda b, h, i: (b * nq + i, h)),
                  pl.BlockSpec((T, hps * LANES), lambda b, h, i: (b, h)),
                  pl.BlockSpec((T, LANES), lambda b, h, i: (b, C_KR // LANES)),
                  pl.BlockSpec((T, hps * LANES), lambda b, h, i: (b, ng + h)),
                  gspec, gspec, qtab, qtab, ktab, ktab],
        out_specs=pl.BlockSpec((tq, hps * LANES), lambda b, h, i: (b * nq + i, h)),
        out_shape=jax.ShapeDtypeStruct((B * T, MLA_HEADS * MLA_V), BF16),
        scratch_shapes=[pltpu.VMEM((hps, T, HEAD_PAD), BF16), pltpu.VMEM((hps, DV_EXT, T), BF16),
                        pltpu.VMEM((hps, tq, HEAD_PAD), BF16)] + _flash_scratch(hps, tk, tq),
        compiler_params=_params(("parallel", "parallel", "arbitrary")),
        name="mla_attention",
    )(q_up, kv_up, proj, kv_up, gq, gk, c, s, c, s)


def _compress_kernel(*refs, is_key, gb, ncp, nt):
    x_tiles = refs[:gb * nt]
    pe_ref, w1a_ref, w1b_ref, b1_ref, w2_ref, b2_ref, g_ref, c_ref, s_ref, o_ref, xa_s, xb_s = refs[gb * nt:]
    rows = gb * ncp
    for g in range(gb):
        for l in range(CMP_STRIDE):
            for t in range(nt):
                x = x_tiles[g * nt + t][pl.ds(l, ncp, stride=CMP_STRIDE), :]
                lanes = slice((l * nt + t) * LANES, (l * nt + t + 1) * LANES)
                pe_lanes = slice(t * LANES, (t + 1) * LANES)
                xa_s[g * ncp:(g + 1) * ncp, lanes] = (x + pe_ref[l:l + 1, pe_lanes]).astype(BF16)
                xb_s[g * ncp:(g + 1) * ncp, lanes] = (
                    x + pe_ref[CMP_STRIDE + l:CMP_STRIDE + l + 1, pe_lanes]).astype(BF16)
    pa = jnp.dot(xa_s[...], w1a_ref[...], preferred_element_type=F32)
    pb = jnp.dot(xb_s[...], w1b_ref[...], preferred_element_type=F32)
    h = pa + pltpu.roll(pb, rows - 1, 0) + b1_ref[...]
    y = jnp.dot(jax.nn.gelu(h).astype(BF16), w2_ref[...], preferred_element_type=F32) + b2_ref[...]
    if is_key:
        ya, yb = y[:, :LANES], y[:, LANES:]
        r = _rms_inv(ya, yb, NSA_DK)
        c = jnp.concatenate([c_ref[...]] * gb, axis=0)
        s = jnp.concatenate([s_ref[...]] * gb, axis=0)
        ya = _rope(ya * r * g_ref[:, :LANES], c, s)
        yb = yb * r * g_ref[:, LANES:]
        y = jnp.concatenate([ya, yb], axis=1)
    o_ref[0] = y.astype(BF16).reshape(gb, ncp, y.shape[-1])


def _compress(proj, c0, d, B, T, pe, w1, b1, w2, b2, g, tabs, is_key, name, gb=2):
    G = NSA_GROUPS
    ncp = T // CMP_STRIDE
    nt = -(-d // LANES)
    dp = nt * LANES
    kd = CMP_STRIDE * dp
    dout = w2.shape[1]
    hid = w1.shape[1]
    w1 = jnp.pad(w1.astype(BF16).reshape(CMP_LEN, d, hid), ((0, 0), (0, dp - d), (0, 0))).reshape(CMP_LEN * dp, hid)
    pe = _pad_cols(pe, dp)
    c, s = tabs
    const = lambda shape: pl.BlockSpec(shape, lambda b, gi: (0,) * len(shape))
    tile = lambda k: pl.BlockSpec((T, LANES), lambda b, gi: (b, c0 // LANES + gi * gb * nt + k))
    return pl.pallas_call(
        functools.partial(_compress_kernel, is_key=is_key, gb=gb, ncp=ncp, nt=nt),
        grid=(B, G // gb),
        in_specs=[tile(k) for k in range(gb * nt)] + [
                  const((CMP_LEN, dp)), const((kd, hid)), const((kd, hid)), const((1, hid)),
                  const((hid, dout)), const((1, dout)), const((1, dout)),
                  const((ncp, LANES)), const((ncp, LANES))],
        out_specs=pl.BlockSpec((1, gb, ncp, dout), lambda b, gi: (b, gi, 0, 0)),
        out_shape=jax.ShapeDtypeStruct((B, G, ncp, dout), BF16),
        scratch_shapes=[pltpu.VMEM((gb * ncp, kd), BF16), pltpu.VMEM((gb * ncp, kd), BF16)],
        compiler_params=_params(("parallel", "parallel")),
        name=name,
    )(*([proj] * (gb * nt)), pe, w1[:kd], w1[kd:], b1.reshape(1, hid), w2.astype(BF16), b2.reshape(1, dout),
      g, c, s)


def _nsa_kernel(q_ref, ks_ref, kw_ref, vs_ref, vw_ref, gate_ref, kc_ref, vc_ref, gq_ref, gks_ref, gkw_ref,
                cq_ref, sq_ref, ck_ref, sk_ref, ovt_ref,
                o_ref, ks_s, kw_s, vst_s, vwt_s, vct_s, q_s, g_s, *flash_scratch, tq, tk, T, scale):
    qi = pl.program_id(2)
    R = NSA_HPG
    nq = R * tq
    nb = T // SLC_LEN
    topk = min(SLC_TOPK, nb)
    ncp = kc_ref.shape[2]

    @pl.when(qi == 0)
    def _():
        def prep_k(k_ref, g_ref):
            ka, kb = k_ref[:, :LANES], k_ref[:, LANES:]
            r = _rms_inv(ka, kb, NSA_DK)
            return _rope(ka * r * g_ref[:, :LANES], ck_ref[...], sk_ref[...]), kb * r * g_ref[:, LANES:]

        ka, kb = prep_k(ks_ref, gks_ref)
        blk = lax.broadcasted_iota(I32, (T, LANES), 0) // SLC_LEN
        lane = lax.broadcasted_iota(I32, (T, LANES), 1)
        ks_s[:, :LANES] = ka.astype(BF16)
        ks_s[:, LANES:] = (kb + (lane - SEL_LANE0 == blk).astype(F32)).astype(BF16)
        ka, kb = prep_k(kw_ref, gkw_ref)
        lane_w = lax.broadcasted_iota(I32, (WINDOW, LANES), 1)
        kw_s[:WINDOW, :LANES] = jnp.zeros((WINDOW, LANES), BF16)
        kw_s[:WINDOW, LANES:] = jnp.where(lane_w == PAD_LANE, NEG_INF, 0.0).astype(BF16)
        kw_s[WINDOW:, :LANES] = ka.astype(BF16)
        kw_s[WINDOW:, LANES:] = kb.astype(BF16)
        vst_s[:DV, :] = vs_ref[...].T.astype(BF16)
        vst_s[DV:, :] = jnp.ones((DV_EXT - DV, T), BF16)
        vwt_s[:DV, :WINDOW] = jnp.zeros((DV, WINDOW), BF16)
        vwt_s[:DV, WINDOW:] = vw_ref[...].T.astype(BF16)
        vwt_s[DV:, :] = jnp.ones((DV_EXT - DV, WINDOW + T), BF16)
        vct_s[...] = vc_ref[0, 0].astype(F32).T.astype(BF16)

    t0 = pl.multiple_of(qi * tq, tq)
    lane_q = lax.broadcasted_iota(I32, (tq, LANES), 1)
    for r in range(R):
        qa = q_ref[:, r * HEAD_PAD:r * HEAD_PAD + LANES]
        qb = q_ref[:, r * HEAD_PAD + LANES:(r + 1) * HEAD_PAD]
        rs = _rms_inv(qa, qb, NSA_DK) * scale
        qa = _rope(qa * rs * gq_ref[:, :LANES], cq_ref[...], sq_ref[...])
        qb = jnp.where(lane_q == PAD_LANE, 1.0, qb * rs * gq_ref[:, LANES:])
        q_s[r * tq:(r + 1) * tq, :LANES] = qa.astype(BF16)
        q_s[r * tq:(r + 1) * tq, LANES:] = qb.astype(BF16)

    c_q = lax.broadcasted_iota(I32, (1, nq), 1) & (tq - 1)
    t_cols = t0 + c_q
    q = q_s[...]

    g_s[...] = jax.nn.sigmoid(gate_ref[...]).T

    span = tq + WINDOW
    sw = lax.dot_general(kw_s[pl.ds(t0, span), :], q, _NT, preferred_element_type=F32)
    sc = lax.dot_general(kc_ref[0, 0], q, _NT, preferred_element_type=F32)

    cend = lax.broadcasted_iota(I32, (ncp, 1), 0) * CMP_STRIDE + (CMP_LEN - 1)
    mask_c = cend <= t_cols
    sc = jnp.where(mask_c, sc, NEG_INF)
    e = jnp.where(mask_c, jnp.exp(sc - _col_max(sc)), 0.0)
    p = e * (1.0 / jnp.maximum(_col_sum(e), 1e-30))
    o_c = jnp.dot(vct_s[...], p.astype(BF16), preferred_element_type=F32)

    ps = p[:, :tq]
    for r in range(1, R):
        ps = ps + p[:, r * tq:(r + 1) * tq]
    ps_hi = ps.astype(BF16)
    ps_lo = (ps - ps_hi.astype(F32)).astype(BF16)
    imp = (jnp.dot(ovt_ref[...], ps_hi, preferred_element_type=F32)
           + jnp.dot(ovt_ref[...], ps_lo, preferred_element_type=F32))
    jb = lax.broadcasted_iota(I32, (nb, tq), 0)
    tb = (t0 + lax.broadcasted_iota(I32, (1, tq), 1)) // SLC_LEN
    forced = (jb == 0) | (jb == tb) | (jb == tb - 1)
    score = jnp.where(forced, FORCE, jnp.where(jb <= tb, imp, -FORCE))
    rank = jnp.zeros((nb, tq), I32)
    for j in range(nb):
        row = score[j:j + 1, :]
        rank = rank + ((row > score) | ((row == score) & (j < jb))).astype(I32)
    unsel = (rank >= topk).astype(F32)
    unsel = jnp.concatenate([jnp.zeros((SEL_LANE0, tq), F32), unsel,
                             jnp.zeros((LANES - SEL_LANE0 - nb, tq), F32)], axis=0)
    bias = unsel.T * NEG_INF

    for r in range(R):
        qb = q_s[r * tq:(r + 1) * tq, LANES:].astype(F32)
        q_s[r * tq:(r + 1) * tq, LANES:] = (qb + bias).astype(BF16)

    i_k = lax.broadcasted_iota(I32, (tq, 1), 0)
    s_lo = jnp.where(i_k > c_q, sw[:tq], NEG_INF)
    s_mid = sw[tq:WINDOW]
    s_hi = jnp.where(i_k <= c_q, sw[WINDOW:], NEG_INF)
    m = jnp.maximum(jnp.maximum(_col_max(s_lo), _col_max(s_mid)), _col_max(s_hi))
    pw = jnp.concatenate([jnp.exp(s_lo - m), jnp.exp(s_mid - m), jnp.exp(s_hi - m)], axis=0).astype(BF16)
    acc = jnp.dot(vwt_s[:, pl.ds(t0, span)], pw, preferred_element_type=F32)
    o_w = acc[:DV] * (1.0 / acc[DV:DV + 1])

    k_at = lambda off: ks_s[pl.ds(off, tk), :]
    vt_at = lambda off: vst_s[:, pl.ds(off, tk)]
    chains = [(lambda r=r: q_s[r * tq:(r + 1) * tq], k_at, vt_at, t_cols[:, r * tq:(r + 1) * tq])
              for r in range(R)]
    o_s = jnp.concatenate(_flash_t(chains, t0 // tk, T // tk, tk, *flash_scratch), axis=1)

    gs = g_s[pl.ds(pl.multiple_of(pl.program_id(1) * GATE_PITCH, GATE_PITCH), GATE_PITCH), :]
    for r in range(R):
        cols = slice(r * tq, (r + 1) * tq)
        o = (gs[3 * r:3 * r + 1] * o_c[:, cols] + gs[3 * r + 1:3 * r + 2] * o_s[:, cols]
             + gs[3 * r + 2:3 * r + 3] * o_w[:, cols])
        o_ref[:, r * DV:(r + 1) * DV] = o.T.astype(BF16)


def _nsa_attention(proj, k_c, v_c, gq, gks, gkw, tabs, ovt, B, T, tq=256, tk=512):
    nq = T // tq
    G, R = NSA_GROUPS, NSA_HPG
    ncp = k_c.shape[2]
    c, s = tabs
    qtab = pl.BlockSpec((tq, LANES), lambda b, g, i: (i, 0))
    ktab = pl.BlockSpec((T, LANES), lambda b, g, i: (0, 0))
    gspec = pl.BlockSpec((1, HEAD_PAD), lambda b, g, i: (0, 0))
    return pl.pallas_call(
        functools.partial(_nsa_kernel, tq=tq, tk=tk, T=T, scale=NSA_DK ** -0.5),
        grid=(B, G, nq),
        in_specs=[pl.BlockSpec((tq, R * HEAD_PAD), lambda b, g, i: (b * nq + i, g)),
                  pl.BlockSpec((T, HEAD_PAD), lambda b, g, i: (b, C_KS // HEAD_PAD + g)),
                  pl.BlockSpec((T, HEAD_PAD), lambda b, g, i: (b, C_KW // HEAD_PAD + g)),
                  pl.BlockSpec((T, DV), lambda b, g, i: (b, C_VS // DV + g)),
                  pl.BlockSpec((T, DV), lambda b, g, i: (b, C_VW // DV + g)),
                  pl.BlockSpec((tq, LANES), lambda b, g, i: (b * nq + i, C_GATE // LANES)),
                  pl.BlockSpec((1, 1, ncp, HEAD_PAD), lambda b, g, i: (b, g, 0, 0)),
                  pl.BlockSpec((1, 1, ncp, DV), lambda b, g, i: (b, g, 0, 0)),
                  gspec, gspec, gspec, qtab, qtab, ktab, ktab,
                  pl.BlockSpec(ovt.shape, lambda b, g, i: (0, 0))],
        out_specs=pl.BlockSpec((tq, R * DV), lambda b, g, i: (b * nq + i, g)),
        out_shape=jax.ShapeDtypeStruct((B * T, NSA_HEADS * DV), BF16),
        scratch_shapes=[pltpu.VMEM((T, HEAD_PAD), BF16), pltpu.VMEM((WINDOW + T, HEAD_PAD), BF16),
                        pltpu.VMEM((DV_EXT, T), BF16), pltpu.VMEM((DV_EXT, WINDOW + T), BF16),
                        pltpu.VMEM((DV, ncp), BF16), pltpu.VMEM((R * tq, HEAD_PAD), BF16),
                        pltpu.VMEM((LANES, tq), F32)]
        + _flash_scratch(R, tk, tq),
        compiler_params=_params(("parallel", "parallel", "arbitrary")),
        name="nsa_attention",
    )(proj, proj, proj, proj, proj, proj, k_c, v_c, gq, gks, gkw, c, s, c, s, ovt)


def _pad_cols(w, n):
    return jnp.pad(w, ((0, 0), (0, n - w.shape[1])))


def _take_cols(w, plan):
    zeros = lambda n: jnp.zeros((w.shape[0], n), w.dtype)
    return jnp.concatenate([zeros(p) if isinstance(p, int) else w[:, p[0]:p[1]] for p in plan], axis=1)


def _nsa_head_plan(c0):
    h = NSA_ROPE // 2
    fill = HALF_LANES - h
    return [(c0, c0 + h), (c0 + 2 * h, c0 + 2 * h + fill), (c0 + h, c0 + 2 * h),
            (c0 + 2 * h + fill, c0 + NSA_DK), HEAD_PAD - NSA_DK]


def _mla_rope_plan(c0):
    h = MLA_ROPE // 2
    return [(c0, c0 + h), HALF_LANES - h, (c0 + h, c0 + 2 * h), HALF_LANES - h]


def _mla_head_plan(c0):
    return [(c0, c0 + MLA_NOPE)] + _mla_rope_plan(c0 + MLA_NOPE)


def _heads(plan_fn, c0, nh, d):
    return [piece for h in range(nh) for piece in plan_fn(c0 + h * d)]


def _nsa_head_layout(w, nh):
    return _take_cols(w, _heads(_nsa_head_plan, 0, nh, NSA_DK))


def _mla_head_layout(w, nh):
    return _take_cols(w, _heads(_mla_head_plan, 0, nh, MLA_QK))


def _layout_w_in(w_in):
    sizes = (MLA_RANK, MLA_RANK, MLA_ROPE, NSA_HEADS * NSA_DK,
             NSA_GROUPS * NSA_DK, NSA_GROUPS * NSA_DV, NSA_GROUPS * NSA_DK, NSA_GROUPS * NSA_DV,
             NSA_GROUPS * NSA_DK, NSA_GROUPS * NSA_DV, NSA_HEADS * 3, 2 * D_MODEL)
    offs, o = [], 0
    for s in sizes:
        offs.append(o)
        o += s
    cq, ckv, kr, qn, kc, vc, ks, vs, kw, vw, gn, gm = offs
    whole = lambda c0, n: [(c0, c0 + n)]
    ngate = NSA_HPG * 3
    plan = (_heads(_nsa_head_plan, qn, NSA_HEADS, NSA_DK) + whole(gm, 2 * D_MODEL)
            + _heads(_nsa_head_plan, ks, NSA_GROUPS, NSA_DK) + _heads(_nsa_head_plan, kw, NSA_GROUPS, NSA_DK)
            + whole(vs, NSA_GROUPS * NSA_DV) + whole(vw, NSA_GROUPS * NSA_DV)
            + whole(cq, MLA_RANK) + whole(ckv, MLA_RANK) + _mla_rope_plan(kr)
            + _heads(lambda c: [(c, c + ngate), GATE_PITCH - ngate], gn, NSA_GROUPS, ngate)
            + [LANES - NSA_GROUPS * GATE_PITCH]
            + _heads(lambda c: [(c, c + NSA_DK), HEAD_PAD - NSA_DK], kc, NSA_GROUPS, NSA_DK)
            + whole(vc, NSA_GROUPS * NSA_DV))
    width = sum(p if isinstance(p, int) else p[1] - p[0] for p in plan)
    assert width == C_VC + NSA_GROUPS * NSA_DV <= C_END
    return _permute_cols(w_in, plan + [C_END - width])


def _permute_cols_kernel(tab_ref, a_ref, b_ref, p_ref, o_ref, *, n_src):
    j = pl.program_id(0)
    lane = lax.broadcasted_iota(I32, (1, LANES), 1)
    acc = None
    for k, ref in enumerate((a_ref, b_ref)):
        in_bounds = tab_ref[j, k] * LANES + lane < n_src
        x = jnp.where(in_bounds, ref[...], 0.0).astype(BF16)
        y = jnp.dot(x, p_ref[0, k], preferred_element_type=F32)
        acc = y if acc is None else acc + y
    o_ref[...] = acc.astype(BF16)


def _permute_cols(w, plan):
    rows, n_src = w.shape
    src = np.concatenate([np.full(p, -1) if isinstance(p, int) else np.arange(p[0], p[1]) for p in plan])
    nt = len(src) // LANES
    tab = np.zeros((nt, 2), np.int32)
    place = np.zeros((nt, 2, LANES, LANES), np.float32)
    for t in range(nt):
        cols = src[t * LANES:(t + 1) * LANES]
        used = cols[cols >= 0]
        if used.size == 0:
            continue
        t0, t1 = used.min() // LANES, used.max() // LANES
        assert t1 - t0 <= 1
        tab[t] = (t0, t1)
        for lane, c in enumerate(cols):
            if c >= 0:
                place[t, c // LANES - t0, c % LANES, lane] = 1.0
    src_tile = lambda k: pl.BlockSpec((rows, LANES), lambda j, tab: (0, tab[j, k]))
    return pl.pallas_call(
        functools.partial(_permute_cols_kernel, n_src=n_src),
        grid_spec=pltpu.PrefetchScalarGridSpec(
            num_scalar_prefetch=1, grid=(nt,),
            in_specs=[src_tile(0), src_tile(1), pl.BlockSpec((1, 2, LANES, LANES), lambda j, tab: (j, 0, 0, 0))],
            out_specs=pl.BlockSpec((rows, LANES), lambda j, tab: (0, j))),
        out_shape=jax.ShapeDtypeStruct((rows, nt * LANES), BF16),
        compiler_params=_params(("parallel",)),
        name="weight_layout",
    )(jnp.asarray(tab), w, w, jnp.asarray(place, BF16))


def _rope_tables(pos, rot_dim):
    half = rot_dim // 2
    inv = ROPE_THETA ** (-np.arange(0, rot_dim, 2, dtype=np.float64) / rot_dim)
    ang = np.asarray(pos, np.float64)[:, None] * inv[None, :]
    c = np.ones((len(pos), LANES))
    s = np.zeros((len(pos), LANES))
    c[:, :half] = c[:, HALF_LANES:HALF_LANES + half] = np.cos(ang)
    s[:, :half] = -np.sin(ang)
    s[:, HALF_LANES:HALF_LANES + half] = np.sin(ang)
    return jnp.asarray(c, F32), jnp.asarray(s, F32)


def _overlap_table(ncp, nb):
    cs = np.arange(ncp)[None, :] * CMP_STRIDE
    ss = np.arange(nb)[:, None] * SLC_LEN
    ov = np.clip(np.minimum(cs + CMP_LEN, ss + SLC_LEN) - np.maximum(cs, ss), 0, None) / CMP_LEN
    return jnp.asarray(ov, BF16)


def _layer(x, attn_norm_g, w_in, mla_q_lat_g, mla_kv_lat_g, mla_w_uq, mla_w_uk, mla_w_uv,
           mla_q_norm_g, mla_k_norm_g, nsa_q_norm_g, nsa_k_norm_g, cmp_pe_k, cmp_pe_v,
           cmp_k_w1, cmp_k_b1, cmp_k_w2, cmp_k_b2, cmp_v_w1, cmp_v_b1, cmp_v_w2, cmp_v_b2,
           w_proj_mla, w_proj_nsa, w_out, ffn_norm_g, w_ffn_gate, w_ffn_up, w_ffn_down):
    B, T, _ = x.shape
    n = B * T
    ncp = T // CMP_STRIDE
    nb = T // SLC_LEN
    assert T % 512 == 0 and ncp % LANES == 0 and nb % 8 == 0 and SEL_LANE0 + nb <= PAD_LANE and T >= WINDOW
    x2 = x.reshape(n, D_MODEL)
    pos = np.arange(T)

    proj = _norm_matmul(x2, 0, D_MODEL, attn_norm_g.reshape(1, -1), _layout_w_in(w_in), 1024, 1024, "in_proj")

    q_up = _norm_matmul(proj, C_CQ // MLA_RANK, MLA_RANK, mla_q_lat_g.reshape(1, -1),
                        _mla_head_layout(mla_w_uq.astype(BF16), MLA_HEADS), 1024, 2048, "mla_q_up")
    kv_up = _norm_matmul(proj, C_CKV // MLA_RANK, MLA_RANK, mla_kv_lat_g.reshape(1, -1),
                         jnp.concatenate([mla_w_uk, mla_w_uv], axis=1).astype(BF16), 1024, 2048, "mla_kv_up")
    o_mla = _mla_attention(q_up, kv_up, proj, _mla_head_layout(mla_q_norm_g[None], 1),
                           _mla_head_layout(mla_k_norm_g[None], 1), _rope_tables(pos, MLA_ROPE), B, T)

    cmp_tabs = _rope_tables(np.arange(ncp) * CMP_STRIDE + CMP_LEN - 1, NSA_ROPE)
    k_c = _compress(proj, C_KC, NSA_DK, B, T, cmp_pe_k, cmp_k_w1, cmp_k_b1, _nsa_head_layout(cmp_k_w2, 1),
                    _nsa_head_layout(cmp_k_b2[None], 1)[0], _nsa_head_layout(nsa_k_norm_g[0:1], 1), cmp_tabs,
                    True, "compress_k")
    v_c = _compress(proj, C_VC, NSA_DV, B, T, cmp_pe_v, cmp_v_w1, cmp_v_b1, cmp_v_w2, cmp_v_b2,
                    jnp.zeros((1, NSA_DV), F32), cmp_tabs, False, "compress_v")
    o_nsa = _nsa_attention(proj, k_c, v_c, _nsa_head_layout(nsa_q_norm_g[None], 1),
                           _nsa_head_layout(nsa_k_norm_g[1:2], 1), _nsa_head_layout(nsa_k_norm_g[2:3], 1),
                           _rope_tables(pos, NSA_ROPE), _overlap_table(ncp, nb), B, T)

    u = _merge(o_mla, o_nsa, w_proj_mla.astype(BF16), w_proj_nsa.astype(BF16), proj)
    h, hn = _out_proj(u, w_out.astype(BF16), x2, ffn_norm_g.reshape(1, -1))
    act = _ffn_up(hn, w_ffn_gate.astype(BF16), w_ffn_up.astype(BF16))
    out = _ffn_down(act, w_ffn_down.astype(BF16), h)
    return out.reshape(B, T, D_MODEL)


def kernel(x, attn_norm_g, w_in, mla_q_lat_g, mla_kv_lat_g, mla_w_uq, mla_w_uk, mla_w_uv, mla_q_norm_g, mla_k_norm_g, nsa_q_norm_g, nsa_k_norm_g, cmp_pe_k, cmp_pe_v, cmp_k_w1, cmp_k_b1, cmp_k_w2, cmp_k_b2, cmp_v_w1, cmp_v_b1, cmp_v_w2, cmp_v_b2, w_proj_mla, w_proj_nsa, w_out, ffn_norm_g, w_ffn_gate, w_ffn_up, w_ffn_down):
    params = (attn_norm_g, w_in, mla_q_lat_g, mla_kv_lat_g, mla_w_uq, mla_w_uk, mla_w_uv, mla_q_norm_g,
              mla_k_norm_g, nsa_q_norm_g, nsa_k_norm_g, cmp_pe_k, cmp_pe_v, cmp_k_w1, cmp_k_b1, cmp_k_w2,
              cmp_k_b2, cmp_v_w1, cmp_v_b1, cmp_v_w2, cmp_v_b2, w_proj_mla, w_proj_nsa, w_out, ffn_norm_g,
              w_ffn_gate, w_ffn_up, w_ffn_down)
    h = x
    for l in range(attn_norm_g.shape[0]):
        h = _layer(h, *(p[l] for p in params))
    return h
```

```python
import functools

import numpy as np
import jax
import jax.numpy as jnp
from jax import lax
from jax.experimental import pallas as pl
from jax.experimental.pallas import tpu as pltpu

F32, BF16, I32 = jnp.float32, jnp.bfloat16, jnp.int32

EPS = 1e-6
NEG_INF = -1e30
FORCE = 1e9
ROPE_THETA = 500000.0

D_MODEL = 2048
MLA_HEADS, MLA_NOPE, MLA_ROPE, MLA_V = 16, 128, 64, 128
MLA_QK = MLA_NOPE + MLA_ROPE
MLA_RANK = 512
NSA_HEADS, NSA_GROUPS, NSA_DK, NSA_DV = 16, 4, 192, 128
NSA_HPG = NSA_HEADS // NSA_GROUPS
NSA_ROPE = NSA_DK // 4
CMP_LEN, CMP_STRIDE = 32, 16
SLC_LEN, SLC_TOPK, WINDOW = 64, 16, 512
D_FF = 5632

LANES = 128
HALF_LANES = 64
HEAD_PAD = 256
SEL_LANE0 = 64
PAD_LANE = 127
DV = 128
DV_EXT = DV + 16
VMEM_LIMIT = 48 * 1024 * 1024

C_QN, C_GM, C_KS, C_KW, C_VS, C_VW = 0, 4096, 8192, 9216, 10240, 10752
C_CQ, C_CKV, C_KR, C_GATE, C_KC, C_VC, C_END = 11264, 11776, 12288, 12416, 12544, 13568, 14336
GATE_PITCH = 16

_NT = (((1,), (1,)), ((), ()))


def _params(sem):
    return pltpu.CompilerParams(dimension_semantics=sem, vmem_limit_bytes=VMEM_LIMIT)


def _norm_matmul_kernel(x_ref, g_ref, w_ref, o_ref, xn_ref):
    @pl.when(pl.program_id(1) == 0)
    def _():
        x = x_ref[...]
        ms = jnp.mean(x * x, axis=-1, keepdims=True)
        xn_ref[...] = (x * lax.rsqrt(ms + EPS) * g_ref[...]).astype(BF16)

    o_ref[...] = jnp.dot(xn_ref[...], w_ref[...], preferred_element_type=F32).astype(o_ref.dtype)


def _norm_matmul(x, xcol, k, g, w, tm, tn, name):
    n, m = x.shape[0], w.shape[1]
    return pl.pallas_call(
        _norm_matmul_kernel,
        grid=(n // tm, m // tn),
        in_specs=[pl.BlockSpec((tm, k), lambda i, j: (i, xcol)),
                  pl.BlockSpec((1, k), lambda i, j: (0, 0)),
                  pl.BlockSpec((k, tn), lambda i, j: (0, j))],
        out_specs=pl.BlockSpec((tm, tn), lambda i, j: (i, j)),
        out_shape=jax.ShapeDtypeStruct((n, m), F32),
        scratch_shapes=[pltpu.VMEM((tm, k), BF16)],
        compiler_params=_params(("parallel", "arbitrary")),
        name=name,
    )(x, g, w)


def _merge_kernel(om_ref, on_ref, wm_ref, wn_ref, g0_ref, g1_ref, o_ref):
    ym = jnp.dot(om_ref[...], wm_ref[...], preferred_element_type=F32)
    yn = jnp.dot(on_ref[...], wn_ref[...], preferred_element_type=F32)
    o_ref[...] = (jax.nn.sigmoid(g0_ref[...]) * ym + jax.nn.sigmoid(g1_ref[...]) * yn).astype(BF16)


def _merge(o_mla, o_nsa, w_pm, w_pn, proj, tm=1024, tn=512):
    n = o_mla.shape[0]
    g0, g1 = C_GM // tn, (C_GM + D_MODEL) // tn
    return pl.pallas_call(
        _merge_kernel,
        grid=(n // tm, D_MODEL // tn),
        in_specs=[pl.BlockSpec((tm, D_MODEL), lambda i, j: (i, 0)),
                  pl.BlockSpec((tm, D_MODEL), lambda i, j: (i, 0)),
                  pl.BlockSpec((D_MODEL, tn), lambda i, j: (0, j)),
                  pl.BlockSpec((D_MODEL, tn), lambda i, j: (0, j)),
                  pl.BlockSpec((tm, tn), lambda i, j: (i, g0 + j)),
                  pl.BlockSpec((tm, tn), lambda i, j: (i, g1 + j))],
        out_specs=pl.BlockSpec((tm, tn), lambda i, j: (i, j)),
        out_shape=jax.ShapeDtypeStruct((n, D_MODEL), BF16),
        compiler_params=_params(("parallel", "arbitrary")),
        name="merge",
    )(o_mla, o_nsa, w_pm, w_pn, proj, proj)


def _out_proj_kernel(u_ref, w_ref, x_ref, g_ref, h_ref, hn_ref):
    h = x_ref[...] + jnp.dot(u_ref[...], w_ref[...], preferred_element_type=F32)
    h_ref[...] = h
    ms = jnp.mean(h * h, axis=-1, keepdims=True)
    hn_ref[...] = (h * lax.rsqrt(ms + EPS) * g_ref[...]).astype(BF16)


def _out_proj(u, w_out, x, g, tm=512):
    n = u.shape[0]
    row = pl.BlockSpec((tm, D_MODEL), lambda i: (i, 0))
    return pl.pallas_call(
        _out_proj_kernel,
        grid=(n // tm,),
        in_specs=[row, pl.BlockSpec((D_MODEL, D_MODEL), lambda i: (0, 0)), row,
                  pl.BlockSpec((1, D_MODEL), lambda i: (0, 0))],
        out_specs=[row, row],
        out_shape=[jax.ShapeDtypeStruct((n, D_MODEL), F32), jax.ShapeDtypeStruct((n, D_MODEL), BF16)],
        compiler_params=_params(("parallel",)),
        name="out_proj",
    )(u, w_out, x, g)


def _ffn_up_kernel(hn_ref, wg_ref, wu_ref, o_ref):
    hn = hn_ref[...]
    a = jnp.dot(hn, wg_ref[...], preferred_element_type=F32)
    b = jnp.dot(hn, wu_ref[...], preferred_element_type=F32)
    o_ref[...] = (jax.nn.silu(a) * b).astype(BF16)


def _ffn_up(hn, w_gate, w_up, tm=1024, tn=512):
    n = hn.shape[0]
    return pl.pallas_call(
        _ffn_up_kernel,
        grid=(n // tm, D_FF // tn),
        in_specs=[pl.BlockSpec((tm, D_MODEL), lambda i, j: (i, 0)),
                  pl.BlockSpec((D_MODEL, tn), lambda i, j: (0, j)),
                  pl.BlockSpec((D_MODEL, tn), lambda i, j: (0, j))],
        out_specs=pl.BlockSpec((tm, tn), lambda i, j: (i, j)),
        out_shape=jax.ShapeDtypeStruct((n, D_FF), BF16),
        compiler_params=_params(("parallel", "arbitrary")),
        name="ffn_up",
    )(hn, w_gate, w_up)


def _ffn_down_kernel(a_ref, w_ref, h_ref, o_ref):
    o_ref[...] = h_ref[...] + jnp.dot(a_ref[...], w_ref[...], preferred_element_type=F32)


def _ffn_down(act, w_down, h, tm=1024, tn=512):
    n = act.shape[0]
    return pl.pallas_call(
        _ffn_down_kernel,
        grid=(n // tm, D_MODEL // tn),
        in_specs=[pl.BlockSpec((tm, D_FF), lambda i, j: (i, 0)),
                  pl.BlockSpec((D_FF, tn), lambda i, j: (0, j)),
                  pl.BlockSpec((tm, tn), lambda i, j: (i, j))],
        out_specs=pl.BlockSpec((tm, tn), lambda i, j: (i, j)),
        out_shape=jax.ShapeDtypeStruct((n, D_MODEL), F32),
        compiler_params=_params(("parallel", "arbitrary")),
        name="ffn_down",
    )(act, w_down, h)


def _rms_inv(xa, xb, d):
    return lax.rsqrt(jnp.sum(xa * xa + xb * xb, axis=-1, keepdims=True) * (1.0 / d) + EPS)


def _rope(x, c, s):
    return x * c + pltpu.roll(x, HALF_LANES, 1) * s


def _col_max(s, slabs=8):
    rows = s.shape[0]
    if rows % (8 * slabs) == 0:
        s = jnp.max(s.reshape(slabs, rows // slabs, s.shape[1]), axis=0)
    return jnp.max(s, axis=0, keepdims=True)


def _col_sum(s, slabs=8):
    rows = s.shape[0]
    if rows % (8 * slabs) == 0:
        s = jnp.sum(s.reshape(slabs, rows // slabs, s.shape[1]), axis=0)
    return jnp.sum(s, axis=0, keepdims=True)


def _flash_t(chains, n_last, n_max, tk, s_buf0, s_buf1, mt_buf0, mt_buf1, m_ref, acc_ref):
    nc = len(chains)
    s_buf, mt_buf = (s_buf0, s_buf1), (mt_buf0, mt_buf1)

    def stage_a(j, masked=True):
        off = j * tk
        for c, (q_at, k_at, _, t_cols) in enumerate(chains):
            s = lax.dot_general(k_at(off), q_at(), _NT, preferred_element_type=F32)
            if masked:
                pk = off + lax.broadcasted_iota(I32, (tk, 1), 0)
                s = jnp.where(pk <= t_cols, s, NEG_INF)
            s_buf[j % 2][c] = s
            mt_buf[j % 2][c] = _col_max(s)

    def stage_b(j):
        off = j * tk
        for c, (_, _, vt_at, _) in enumerate(chains):
            m = m_ref[c]
            m_new = jnp.maximum(m, mt_buf[j % 2][c])
            p = jnp.exp(s_buf[j % 2][c] - m_new).astype(BF16)
            acc_ref[c] = jnp.exp(m - m_new) * acc_ref[c] + jnp.dot(vt_at(off), p, preferred_element_type=F32)
            m_ref[c] = m_new

    m_ref[...] = jnp.full(m_ref.shape, NEG_INF, F32)
    acc_ref[...] = jnp.zeros(acc_ref.shape, F32)
    stage_a(0)
    for j in range(n_max):
        if j + 1 < n_max:
            @pl.when(j + 1 < n_last)
            def _(j=j):
                stage_a(j + 1, masked=False)
                stage_b(j)

            @pl.when(j + 1 == n_last)
            def _(j=j):
                stage_a(j + 1)
                stage_b(j)

        @pl.when(j == n_last)
        def _(j=j):
            stage_b(j)

    return [acc_ref[c, :DV] * (1.0 / acc_ref[c, DV:DV + 1]) for c in range(nc)]


def _flash_scratch(nc, tk, nq):
    return [pltpu.VMEM((nc, tk, nq), F32), pltpu.VMEM((nc, tk, nq), F32),
            pltpu.VMEM((nc, 1, nq), F32), pltpu.VMEM((nc, 1, nq), F32),
            pltpu.VMEM((nc, 1, nq), F32), pltpu.VMEM((nc, DV_EXT, nq), F32)]


def _mla_kernel(q_ref, kn_ref, kr_ref, v_ref, gq_ref, gk_ref, cq_ref, sq_ref, ck_ref, sk_ref,
                o_ref, k_s, vt_s, q_s, *flash_scratch, tq, tk, scale, hps):
    qi = pl.program_id(2)
    T = kn_ref.shape[0]

    @pl.when(qi == 0)
    def _():
        kr = kr_ref[...]
        ss_r = jnp.sum(kr * kr, axis=-1, keepdims=True)
        kr_rot = _rope(kr * gk_ref[:, LANES:], ck_ref[...], sk_ref[...])
        for h in range(hps):
            kn = kn_ref[:, h * LANES:(h + 1) * LANES]
            r = lax.rsqrt((jnp.sum(kn * kn, axis=-1, keepdims=True) + ss_r) * (1.0 / MLA_QK) + EPS)
            k_s[h, :, :LANES] = (kn * r * gk_ref[:, :LANES]).astype(BF16)
            k_s[h, :, LANES:] = (kr_rot * r).astype(BF16)
            vt_s[h, :DV, :] = v_ref[:, h * LANES:(h + 1) * LANES].T.astype(BF16)
            vt_s[h, DV:, :] = jnp.ones((DV_EXT - DV, T), BF16)

    t0 = qi * tq
    t_cols = t0 + lax.broadcasted_iota(I32, (1, tq), 1)
    chains = []
    for h in range(hps):
        qa = q_ref[:, h * HEAD_PAD:h * HEAD_PAD + LANES]
        qb = q_ref[:, h * HEAD_PAD + LANES:(h + 1) * HEAD_PAD]
        r = _rms_inv(qa, qb, MLA_QK) * scale
        qa = qa * r * gq_ref[:, :LANES]
        qb = _rope(qb * r * gq_ref[:, LANES:], cq_ref[...], sq_ref[...])
        q_s[h, :, :LANES] = qa.astype(BF16)
        q_s[h, :, LANES:] = qb.astype(BF16)
        chains.append((lambda h=h: q_s[h], lambda off, h=h: k_s[h, pl.ds(off, tk), :],
                       lambda off, h=h: vt_s[h, :, pl.ds(off, tk)], t_cols))
    for h, o_t in enumerate(_flash_t(chains, t0 // tk, T // tk, tk, *flash_scratch)):
        o_ref[:, h * LANES:(h + 1) * LANES] = o_t.T.astype(BF16)


def _mla_attention(q_up, kv_up, proj, gq, gk, tabs, B, T, tq=512, tk=512, hps=4):
    nq = T // tq
    ng = MLA_HEADS // hps
    c, s = tabs
    qtab = pl.BlockSpec((tq, LANES), lambda b, h, i: (i, 0))
    ktab = pl.BlockSpec((T, LANES), lambda b, h, i: (0, 0))
    gspec = pl.BlockSpec((1, HEAD_PAD), lambda b, h, i: (0, 0))
    return pl.pallas_call(
        functools.partial(_mla_kernel, tq=tq, tk=tk, scale=MLA_QK ** -0.5, hps=hps),
        grid=(B, ng, nq),
        in_specs=[pl.BlockSpec((tq, hps * HEAD_PAD), lambda b, h, i: (b * nq + i, h)),
                  pl.BlockSpec((T, hps * LANES), lambda b, h, i: (b, h)),
                  pl.BlockSpec((T, LANES), lambda b, h, i: (b, C_KR // LANES)),
                  pl.BlockSpec((T, hps * LANES), lambda b, h, i: (b, ng + h)),
                  gspec, gspec, qtab, qtab, ktab, ktab],
        out_specs=pl.BlockSpec((tq, hps * LANES), lambda b, h, i: (b * nq + i, h)),
        out_shape=jax.ShapeDtypeStruct((B * T, MLA_HEADS * MLA_V), BF16),
        scratch_shapes=[pltpu.VMEM((hps, T, HEAD_PAD), BF16), pltpu.VMEM((hps, DV_EXT, T), BF16),
                        pltpu.VMEM((hps, tq, HEAD_PAD), BF16)] + _flash_scratch(hps, tk, tq),
        compiler_params=_params(("parallel", "parallel", "arbitrary")),
        name="mla_attention",
    )(q_up, kv_up, proj, kv_up, gq, gk, c, s, c, s)


def _compress_kernel(*refs, is_key, gb, ncp, nt):
    x_tiles = refs[:gb * nt]
    pe_ref, w1a_ref, w1b_ref, b1_ref, w2_ref, b2_ref, g_ref, c_ref, s_ref, o_ref, xa_s, xb_s = refs[gb * nt:]
    rows = gb * ncp
    for g in range(gb):
        for l in range(CMP_STRIDE):
            for t in range(nt):
                x = x_tiles[g * nt + t][pl.ds(l, ncp, stride=CMP_STRIDE), :]
                lanes = slice((l * nt + t) * LANES, (l * nt + t + 1) * LANES)
                pe_lanes = slice(t * LANES, (t + 1) * LANES)
                xa_s[g * ncp:(g + 1) * ncp, lanes] = (x + pe_ref[l:l + 1, pe_lanes]).astype(BF16)
                xb_s[g * ncp:(g + 1) * ncp, lanes] = (
                    x + pe_ref[CMP_STRIDE + l:CMP_STRIDE + l + 1, pe_lanes]).astype(BF16)
    pa = jnp.dot(xa_s[...], w1a_ref[...], preferred_element_type=F32)
    pb = jnp.dot(xb_s[...], w1b_ref[...], preferred_element_type=F32)
    h = pa + pltpu.roll(pb, rows - 1, 0) + b1_ref[...]
    y = jnp.dot(jax.nn.gelu(h).astype(BF16), w2_ref[...], preferred_element_type=F32) + b2_ref[...]
    if is_key:
        ya, yb = y[:, :LANES], y[:, LANES:]
        r = _rms_inv(ya, yb, NSA_DK)
        c = jnp.concatenate([c_ref[...]] * gb, axis=0)
        s = jnp.concatenate([s_ref[...]] * gb, axis=0)
        ya = _rope(ya * r * g_ref[:, :LANES], c, s)
        yb = yb * r * g_ref[:, LANES:]
        y = jnp.concatenate([ya, yb], axis=1)
    o_ref[0] = y.astype(BF16).reshape(gb, ncp, y.shape[-1])


def _compress(proj, c0, d, B, T, pe, w1, b1, w2, b2, g, tabs, is_key, name, gb=2):
    G = NSA_GROUPS
    ncp = T // CMP_STRIDE
    nt = -(-d // LANES)
    dp = nt * LANES
    kd = CMP_STRIDE * dp
    dout = w2.shape[1]
    hid = w1.shape[1]
    w1 = jnp.pad(w1.astype(BF16).reshape(CMP_LEN, d, hid), ((0, 0), (0, dp - d), (0, 0))).reshape(CMP_LEN * dp, hid)
    pe = _pad_cols(pe, dp)
    c, s = tabs
    const = lambda shape: pl.BlockSpec(shape, lambda b, gi: (0,) * len(shape))
    tile = lambda k: pl.BlockSpec((T, LANES), lambda b, gi: (b, c0 // LANES + gi * gb * nt + k))
    return pl.pallas_call(
        functools.partial(_compress_kernel, is_key=is_key, gb=gb, ncp=ncp, nt=nt),
        grid=(B, G // gb),
        in_specs=[tile(k) for k in range(gb * nt)] + [
                  const((CMP_LEN, dp)), const((kd, hid)), const((kd, hid)), const((1, hid)),
                  const((hid, dout)), const((1, dout)), const((1, dout)),
                  const((ncp, LANES)), const((ncp, LANES))],
        out_specs=pl.BlockSpec((1, gb, ncp, dout), lambda b, gi: (b, gi, 0, 0)),
        out_shape=jax.ShapeDtypeStruct((B, G, ncp, dout), BF16),
        scratch_shapes=[pltpu.VMEM((gb * ncp, kd), BF16), pltpu.VMEM((gb * ncp, kd), BF16)],
        compiler_params=_params(("parallel", "parallel")),
        name=name,
    )(*([proj] * (gb * nt)), pe, w1[:kd], w1[kd:], b1.reshape(1, hid), w2.astype(BF16), b2.reshape(1, dout),
      g, c, s)


def _nsa_kernel(q_ref, ks_ref, kw_ref, vs_ref, vw_ref, gate_ref, kc_ref, vc_ref, gq_ref, gks_ref, gkw_ref,
                cq_ref, sq_ref, ck_ref, sk_ref, ovt_ref,
                o_ref, ks_s, kw_s, vst_s, vwt_s, vct_s, q_s, g_s, *flash_scratch, tq, tk, T, scale):
    qi = pl.program_id(2)
    R = NSA_HPG
    nq = R * tq
    nb = T // SLC_LEN
    topk = min(SLC_TOPK, nb)
    ncp = kc_ref.shape[2]

    @pl.when(qi == 0)
    def _():
        def prep_k(k_ref, g_ref):
            ka, kb = k_ref[:, :LANES], k_ref[:, LANES:]
            r = _rms_inv(ka, kb, NSA_DK)
            return _rope(ka * r * g_ref[:, :LANES], ck_ref[...], sk_ref[...]), kb * r * g_ref[:, LANES:]

        ka, kb = prep_k(ks_ref, gks_ref)
        blk = lax.broadcasted_iota(I32, (T, LANES), 0) // SLC_LEN
        lane = lax.broadcasted_iota(I32, (T, LANES), 1)
        ks_s[:, :LANES] = ka.astype(BF16)
        ks_s[:, LANES:] = (kb + (lane - SEL_LANE0 == blk).astype(F32)).astype(BF16)
        ka, kb = prep_k(kw_ref, gkw_ref)
        lane_w = lax.broadcasted_iota(I32, (WINDOW, LANES), 1)
        kw_s[:WINDOW, :LANES] = jnp.zeros((WINDOW, LANES), BF16)
        kw_s[:WINDOW, LANES:] = jnp.where(lane_w == PAD_LANE, NEG_INF, 0.0).astype(BF16)
        kw_s[WINDOW:, :LANES] = ka.astype(BF16)
        kw_s[WINDOW:, LANES:] = kb.astype(BF16)
        vst_s[:DV, :] = vs_ref[...].T.astype(BF16)
        vst_s[DV:, :] = jnp.ones((DV_EXT - DV, T), BF16)
        vwt_s[:DV, :WINDOW] = jnp.zeros((DV, WINDOW), BF16)
        vwt_s[:DV, WINDOW:] = vw_ref[...].T.astype(BF16)
        vwt_s[DV:, :] = jnp.ones((DV_EXT - DV, WINDOW + T), BF16)
        vct_s[...] = vc_ref[0, 0].astype(F32).T.astype(BF16)

    t0 = pl.multiple_of(qi * tq, tq)
    lane_q = lax.broadcasted_iota(I32, (tq, LANES), 1)
    for r in range(R):
        qa = q_ref[:, r * HEAD_PAD:r * HEAD_PAD + LANES]
        qb = q_ref[:, r * HEAD_PAD + LANES:(r + 1) * HEAD_PAD]
        rs = _rms_inv(qa, qb, NSA_DK) * scale
        qa = _rope(qa * rs * gq_ref[:, :LANES], cq_ref[...], sq_ref[...])
        qb = jnp.where(lane_q == PAD_LANE, 1.0, qb * rs * gq_ref[:, LANES:])
        q_s[r * tq:(r + 1) * tq, :LANES] = qa.astype(BF16)
        q_s[r * tq:(r + 1) * tq, LANES:] = qb.astype(BF16)

    c_q = lax.broadcasted_iota(I32, (1, nq), 1) & (tq - 1)
    t_cols = t0 + c_q
    q = q_s[...]

    g_s[...] = jax.nn.sigmoid(gate_ref[...]).T

    span = tq + WINDOW
    sw = lax.dot_general(kw_s[pl.ds(t0, span), :], q, _NT, preferred_element_type=F32)
    sc = lax.dot_general(kc_ref[0, 0], q, _NT, preferred_element_type=F32)

    cend = lax.broadcasted_iota(I32, (ncp, 1), 0) * CMP_STRIDE + (CMP_LEN - 1)
    mask_c = cend <= t_cols
    sc = jnp.where(mask_c, sc, NEG_INF)
    e = jnp.where(mask_c, jnp.exp(sc - _col_max(sc)), 0.0)
    p = e * (1.0 / jnp.maximum(_col_sum(e), 1e-30))
    o_c = jnp.dot(vct_s[...], p.astype(BF16), preferred_element_type=F32)

    ps = p[:, :tq]
    for r in range(1, R):
        ps = ps + p[:, r * tq:(r + 1) * tq]
    ps_hi = ps.astype(BF16)
    ps_lo = (ps - ps_hi.astype(F32)).astype(BF16)
    imp = (jnp.dot(ovt_ref[...], ps_hi, preferred_element_type=F32)
           + jnp.dot(ovt_ref[...], ps_lo, preferred_element_type=F32))
    jb = lax.broadcasted_iota(I32, (nb, tq), 0)
    tb = (t0 + lax.broadcasted_iota(I32, (1, tq), 1)) // SLC_LEN
    forced = (jb == 0) | (jb == tb) | (jb == tb - 1)
    score = jnp.where(forced, FORCE, jnp.where(jb <= tb, imp, -FORCE))
    rank = jnp.zeros((nb, tq), I32)
    for j in range(nb):
        row = score[j:j + 1, :]
        rank = rank + ((row > score) | ((row == score) & (j < jb))).astype(I32)
    unsel = (rank >= topk).astype(F32)
    unsel = jnp.concatenate([jnp.zeros((SEL_LANE0, tq), F32), unsel,
                             jnp.zeros((LANES - SEL_LANE0 - nb, tq), F32)], axis=0)
    bias = unsel.T * NEG_INF

    for r in range(R):
        qb = q_s[r * tq:(r + 1) * tq, LANES:].astype(F32)
        q_s[r * tq:(r + 1) * tq, LANES:] = (qb + bias).astype(BF16)

    i_k = lax.broadcasted_iota(I32, (tq, 1), 0)
    segs = [jnp.where(i_k > c_q, sw[:tq], NEG_INF)]
    if tq < WINDOW:
        segs.append(sw[tq:WINDOW])
    segs.append(jnp.where(i_k <= c_q, sw[WINDOW:], NEG_INF))
    m = functools.reduce(jnp.maximum, [_col_max(s) for s in segs])
    pw = jnp.concatenate([jnp.exp(s - m) for s in segs], axis=0).astype(BF16)
    acc = jnp.dot(vwt_s[:, pl.ds(t0, span)], pw, preferred_element_type=F32)
    o_w = acc[:DV] * (1.0 / acc[DV:DV + 1])

    k_at = lambda off: ks_s[pl.ds(off, tk), :]
    vt_at = lambda off: vst_s[:, pl.ds(off, tk)]
    chains = [(lambda r=r: q_s[r * tq:(r + 1) * tq], k_at, vt_at, t_cols[:, r * tq:(r + 1) * tq])
              for r in range(R)]
    o_s = jnp.concatenate(_flash_t(chains, t0 // tk, T // tk, tk, *flash_scratch), axis=1)

    gs = g_s[pl.ds(pl.multiple_of(pl.program_id(1) * GATE_PITCH, GATE_PITCH), GATE_PITCH), :]
    for r in range(R):
        cols = slice(r * tq, (r + 1) * tq)
        o = (gs[3 * r:3 * r + 1] * o_c[:, cols] + gs[3 * r + 1:3 * r + 2] * o_s[:, cols]
             + gs[3 * r + 2:3 * r + 3] * o_w[:, cols])
        o_ref[:, r * DV:(r + 1) * DV] = o.T.astype(BF16)


def _nsa_attention(proj, k_c, v_c, gq, gks, gkw, tabs, ovt, B, T, tq=512, tk=512):
    nq = T // tq
    G, R = NSA_GROUPS, NSA_HPG
    ncp = k_c.shape[2]
    c, s = tabs
    qtab = pl.BlockSpec((tq, LANES), lambda b, g, i: (i, 0))
    ktab = pl.BlockSpec((T, LANES), lambda b, g, i: (0, 0))
    gspec = pl.BlockSpec((1, HEAD_PAD), lambda b, g, i: (0, 0))
    return pl.pallas_call(
        functools.partial(_nsa_kernel, tq=tq, tk=tk, T=T, scale=NSA_DK ** -0.5),
        grid=(B, G, nq),
        in_specs=[pl.BlockSpec((tq, R * HEAD_PAD), lambda b, g, i: (b * nq + i, g)),
                  pl.BlockSpec((T, HEAD_PAD), lambda b, g, i: (b, C_KS // HEAD_PAD + g)),
                  pl.BlockSpec((T, HEAD_PAD), lambda b, g, i: (b, C_KW // HEAD_PAD + g)),
                  pl.BlockSpec((T, DV), lambda b, g, i: (b, C_VS // DV + g)),
                  pl.BlockSpec((T, DV), lambda b, g, i: (b, C_VW // DV + g)),
                  pl.BlockSpec((tq, LANES), lambda b, g, i: (b * nq + i, C_GATE // LANES)),
                  pl.BlockSpec((1, 1, ncp, HEAD_PAD), lambda b, g, i: (b, g, 0, 0)),
                  pl.BlockSpec((1, 1, ncp, DV), lambda b, g, i: (b, g, 0, 0)),
                  gspec, gspec, gspec, qtab, qtab, ktab, ktab,
                  pl.BlockSpec(ovt.shape, lambda b, g, i: (0, 0))],
        out_specs=pl.BlockSpec((tq, R * DV), lambda b, g, i: (b * nq + i, g)),
        out_shape=jax.ShapeDtypeStruct((B * T, NSA_HEADS * DV), BF16),
        scratch_shapes=[pltpu.VMEM((T, HEAD_PAD), BF16), pltpu.VMEM((WINDOW + T, HEAD_PAD), BF16),
                        pltpu.VMEM((DV_EXT, T), BF16), pltpu.VMEM((DV_EXT, WINDOW + T), BF16),
                        pltpu.VMEM((DV, ncp), BF16), pltpu.VMEM((R * tq, HEAD_PAD), BF16),
                        pltpu.VMEM((LANES, tq), F32)]
        + _flash_scratch(R, tk, tq),
        compiler_params=_params(("parallel", "parallel", "arbitrary")),
        name="nsa_attention",
    )(proj, proj, proj, proj, proj, proj, k_c, v_c, gq, gks, gkw, c, s, c, s, ovt)


def _pad_cols(w, n):
    return jnp.pad(w, ((0, 0), (0, n - w.shape[1])))


def _take_cols(w, plan):
    zeros = lambda n: jnp.zeros((w.shape[0], n), w.dtype)
    return jnp.concatenate([zeros(p) if isinstance(p, int) else w[:, p[0]:p[1]] for p in plan], axis=1)


def _nsa_head_plan(c0):
    h = NSA_ROPE // 2
    fill = HALF_LANES - h
    return [(c0, c0 + h), (c0 + 2 * h, c0 + 2 * h + fill), (c0 + h, c0 + 2 * h),
            (c0 + 2 * h + fill, c0 + NSA_DK), HEAD_PAD - NSA_DK]


def _mla_rope_plan(c0):
    h = MLA_ROPE // 2
    return [(c0, c0 + h), HALF_LANES - h, (c0 + h, c0 + 2 * h), HALF_LANES - h]


def _mla_head_plan(c0):
    return [(c0, c0 + MLA_NOPE)] + _mla_rope_plan(c0 + MLA_NOPE)


def _heads(plan_fn, c0, nh, d):
    return [piece for h in range(nh) for piece in plan_fn(c0 + h * d)]


def _nsa_head_layout(w, nh):
    return _take_cols(w, _heads(_nsa_head_plan, 0, nh, NSA_DK))


def _mla_head_layout(w, nh):
    return _take_cols(w, _heads(_mla_head_plan, 0, nh, MLA_QK))


def _layout_w_in(w_in):
    sizes = (MLA_RANK, MLA_RANK, MLA_ROPE, NSA_HEADS * NSA_DK,
             NSA_GROUPS * NSA_DK, NSA_GROUPS * NSA_DV, NSA_GROUPS * NSA_DK, NSA_GROUPS * NSA_DV,
             NSA_GROUPS * NSA_DK, NSA_GROUPS * NSA_DV, NSA_HEADS * 3, 2 * D_MODEL)
    offs, o = [], 0
    for s in sizes:
        offs.append(o)
        o += s
    cq, ckv, kr, qn, kc, vc, ks, vs, kw, vw, gn, gm = offs
    whole = lambda c0, n: [(c0, c0 + n)]
    ngate = NSA_HPG * 3
    plan = (_heads(_nsa_head_plan, qn, NSA_HEADS, NSA_DK) + whole(gm, 2 * D_MODEL)
            + _heads(_nsa_head_plan, ks, NSA_GROUPS, NSA_DK) + _heads(_nsa_head_plan, kw, NSA_GROUPS, NSA_DK)
            + whole(vs, NSA_GROUPS * NSA_DV) + whole(vw, NSA_GROUPS * NSA_DV)
            + whole(cq, MLA_RANK) + whole(ckv, MLA_RANK) + _mla_rope_plan(kr)
            + _heads(lambda c: [(c, c + ngate), GATE_PITCH - ngate], gn, NSA_GROUPS, ngate)
            + [LANES - NSA_GROUPS * GATE_PITCH]
            + _heads(lambda c: [(c, c + NSA_DK), HEAD_PAD - NSA_DK], kc, NSA_GROUPS, NSA_DK)
            + whole(vc, NSA_GROUPS * NSA_DV))
    width = sum(p if isinstance(p, int) else p[1] - p[0] for p in plan)
    assert width == C_VC + NSA_GROUPS * NSA_DV <= C_END
    return _permute_cols(w_in, plan + [C_END - width])


def _permute_cols_kernel(tab_ref, a_ref, b_ref, p_ref, o_ref, *, n_src):
    j = pl.program_id(0)
    lane = lax.broadcasted_iota(I32, (1, LANES), 1)
    acc = None
    for k, ref in enumerate((a_ref, b_ref)):
        in_bounds = tab_ref[j, k] * LANES + lane < n_src
        x = jnp.where(in_bounds, ref[...], 0.0).astype(BF16)
        y = jnp.dot(x, p_ref[0, k], preferred_element_type=F32)
        acc = y if acc is None else acc + y
    o_ref[...] = acc.astype(BF16)


def _permute_cols(w, plan):
    rows, n_src = w.shape
    src = np.concatenate([np.full(p, -1) if isinstance(p, int) else np.arange(p[0], p[1]) for p in plan])
    nt = len(src) // LANES
    tab = np.zeros((nt, 2), np.int32)
    place = np.zeros((nt, 2, LANES, LANES), np.float32)
    for t in range(nt):
        cols = src[t * LANES:(t + 1) * LANES]
        used = cols[cols >= 0]
        if used.size == 0:
            continue
        t0, t1 = used.min() // LANES, used.max() // LANES
        assert t1 - t0 <= 1
        tab[t] = (t0, t1)
        for lane, c in enumerate(cols):
            if c >= 0:
                place[t, c // LANES - t0, c % LANES, lane] = 1.0
    src_tile = lambda k: pl.BlockSpec((rows, LANES), lambda j, tab: (0, tab[j, k]))
    return pl.pallas_call(
        functools.partial(_permute_cols_kernel, n_src=n_src),
        grid_spec=pltpu.PrefetchScalarGridSpec(
            num_scalar_prefetch=1, grid=(nt,),
            in_specs=[src_tile(0), src_tile(1), pl.BlockSpec((1, 2, LANES, LANES), lambda j, tab: (j, 0, 0, 0))],
            out_specs=pl.BlockSpec((rows, LANES), lambda j, tab: (0, j))),
        out_shape=jax.ShapeDtypeStruct((rows, nt * LANES), BF16),
        compiler_params=_params(("parallel",)),
        name="weight_layout",
    )(jnp.asarray(tab), w, w, jnp.asarray(place, BF16))


def _rope_tables(pos, rot_dim):
    half = rot_dim // 2
    inv = ROPE_THETA ** (-np.arange(0, rot_dim, 2, dtype=np.float64) / rot_dim)
    ang = np.asarray(pos, np.float64)[:, None] * inv[None, :]
    c = np.ones((len(pos), LANES))
    s = np.zeros((len(pos), LANES))
    c[:, :half] = c[:, HALF_LANES:HALF_LANES + half] = np.cos(ang)
    s[:, :half] = -np.sin(ang)
    s[:, HALF_LANES:HALF_LANES + half] = np.sin(ang)
    return jnp.asarray(c, F32), jnp.asarray(s, F32)


def _overlap_table(ncp, nb):
    cs = np.arange(ncp)[None, :] * CMP_STRIDE
    ss = np.arange(nb)[:, None] * SLC_LEN
    ov = np.clip(np.minimum(cs + CMP_LEN, ss + SLC_LEN) - np.maximum(cs, ss), 0, None) / CMP_LEN
    return jnp.asarray(ov, BF16)


def _layer(x, attn_norm_g, w_in, mla_q_lat_g, mla_kv_lat_g, mla_w_uq, mla_w_uk, mla_w_uv,
           mla_q_norm_g, mla_k_norm_g, nsa_q_norm_g, nsa_k_norm_g, cmp_pe_k, cmp_pe_v,
           cmp_k_w1, cmp_k_b1, cmp_k_w2, cmp_k_b2, cmp_v_w1, cmp_v_b1, cmp_v_w2, cmp_v_b2,
           w_proj_mla, w_proj_nsa, w_out, ffn_norm_g, w_ffn_gate, w_ffn_up, w_ffn_down):
    B, T, _ = x.shape
    n = B * T
    ncp = T // CMP_STRIDE
    nb = T // SLC_LEN
    assert T % 512 == 0 and ncp % LANES == 0 and nb % 8 == 0 and SEL_LANE0 + nb <= PAD_LANE and T >= WINDOW
    x2 = x.reshape(n, D_MODEL)
    pos = np.arange(T)

    proj = _norm_matmul(x2, 0, D_MODEL, attn_norm_g.reshape(1, -1), _layout_w_in(w_in), 1024, 1024, "in_proj")

    q_up = _norm_matmul(proj, C_CQ // MLA_RANK, MLA_RANK, mla_q_lat_g.reshape(1, -1),
                        _mla_head_layout(mla_w_uq.astype(BF16), MLA_HEADS), 1024, 2048, "mla_q_up")
    kv_up = _norm_matmul(proj, C_CKV // MLA_RANK, MLA_RANK, mla_kv_lat_g.reshape(1, -1),
                         jnp.concatenate([mla_w_uk, mla_w_uv], axis=1).astype(BF16), 1024, 2048, "mla_kv_up")
    o_mla = _mla_attention(q_up, kv_up, proj, _mla_head_layout(mla_q_norm_g[None], 1),
                           _mla_head_layout(mla_k_norm_g[None], 1), _rope_tables(pos, MLA_ROPE), B, T)

    cmp_tabs = _rope_tables(np.arange(ncp) * CMP_STRIDE + CMP_LEN - 1, NSA_ROPE)
    k_c = _compress(proj, C_KC, NSA_DK, B, T, cmp_pe_k, cmp_k_w1, cmp_k_b1, _nsa_head_layout(cmp_k_w2, 1),
                    _nsa_head_layout(cmp_k_b2[None], 1)[0], _nsa_head_layout(nsa_k_norm_g[0:1], 1), cmp_tabs,
                    True, "compress_k")
    v_c = _compress(proj, C_VC, NSA_DV, B, T, cmp_pe_v, cmp_v_w1, cmp_v_b1, cmp_v_w2, cmp_v_b2,
                    jnp.zeros((1, NSA_DV), F32), cmp_tabs, False, "compress_v")
    o_nsa = _nsa_attention(proj, k_c, v_c, _nsa_head_layout(nsa_q_norm_g[None], 1),
                           _nsa_head_layout(nsa_k_norm_g[1:2], 1), _nsa_head_layout(nsa_k_norm_g[2:3], 1),
                           _rope_tables(pos, NSA_ROPE), _overlap_table(ncp, nb), B, T)

    u = _merge(o_mla, o_nsa, w_proj_mla.astype(BF16), w_proj_nsa.astype(BF16), proj)
    h, hn = _out_proj(u, w_out.astype(BF16), x2, ffn_norm_g.reshape(1, -1))
    act = _ffn_up(hn, w_ffn_gate.astype(BF16), w_ffn_up.astype(BF16))
    out = _ffn_down(act, w_ffn_down.astype(BF16), h)
    return out.reshape(B, T, D_MODEL)


def kernel(x, attn_norm_g, w_in, mla_q_lat_g, mla_kv_lat_g, mla_w_uq, mla_w_uk, mla_w_uv, mla_q_norm_g, mla_k_norm_g, nsa_q_norm_g, nsa_k_norm_g, cmp_pe_k, cmp_pe_v, cmp_k_w1, cmp_k_b1, cmp_k_w2, cmp_k_b2, cmp_v_w1, cmp_v_b1, cmp_v_w2, cmp_v_b2, w_proj_mla, w_proj_nsa, w_out, ffn_norm_g, w_ffn_gate, w_ffn_up, w_ffn_down):
    params = (attn_norm_g, w_in, mla_q_lat_g, mla_kv_lat_g, mla_w_uq, mla_w_uk, mla_w_uv, mla_q_norm_g,
              mla_k_norm_g, nsa_q_norm_g, nsa_k_norm_g, cmp_pe_k, cmp_pe_v, cmp_k_w1, cmp_k_b1, cmp_k_w2,
              cmp_k_b2, cmp_v_w1, cmp_v_b1, cmp_v_w2, cmp_v_b2, w_proj_mla, w_proj_nsa, w_out, ffn_norm_g,
              w_ffn_gate, w_ffn_up, w_ffn_down)
    depth = attn_norm_g.shape[0]
    h = x
    for l in range(depth):
        h = _layer(h, *(p.reshape(p.shape[1:]) if depth == 1 else p[l] for p in params))
    return h
```

```python
import functools

import numpy as np
import jax
import jax.numpy as jnp
from jax import lax
from jax.experimental import pallas as pl
from jax.experimental.pallas import tpu as pltpu

F32, BF16, I32 = jnp.float32, jnp.bfloat16, jnp.int32

EPS = 1e-6
NEG_INF = -1e30
FORCE = 1e9
ROPE_THETA = 500000.0

D_MODEL = 2048
MLA_HEADS, MLA_NOPE, MLA_ROPE, MLA_V = 16, 128, 64, 128
MLA_QK = MLA_NOPE + MLA_ROPE
MLA_RANK = 512
NSA_HEADS, NSA_GROUPS, NSA_DK, NSA_DV = 16, 4, 192, 128
NSA_HPG = NSA_HEADS // NSA_GROUPS
NSA_ROPE = NSA_DK // 4
CMP_LEN, CMP_STRIDE = 32, 16
SLC_LEN, SLC_TOPK, WINDOW = 64, 16, 512
D_FF = 5632

LANES = 128
HALF_LANES = 64
HEAD_PAD = 256
SEL_LANE0 = 64
PAD_LANE = 127
DV = 128
DV_EXT = DV + 16
VMEM_LIMIT = 48 * 1024 * 1024

C_QN, C_GM, C_KS, C_KW, C_VS, C_VW = 0, 4096, 8192, 9216, 10240, 10752
C_CQ, C_CKV, C_KR, C_GATE, C_KC, C_VC, C_END = 11264, 11776, 12288, 12416, 12544, 13568, 14336
GATE_PITCH = 16

_NT = (((1,), (1,)), ((), ()))


def _params(sem):
    return pltpu.CompilerParams(dimension_semantics=sem, vmem_limit_bytes=VMEM_LIMIT)


def _norm_matmul_kernel(x_ref, g_ref, w_ref, o_ref, xn_ref, *, w_t):
    @pl.when(pl.program_id(1) == 0)
    def _():
        x = x_ref[...]
        ms = jnp.mean(x * x, axis=-1, keepdims=True)
        xn_ref[...] = (x * lax.rsqrt(ms + EPS) * g_ref[...]).astype(BF16)

    if w_t:
        y = lax.dot_general(xn_ref[...], w_ref[...], _NT, preferred_element_type=F32)
    else:
        y = jnp.dot(xn_ref[...], w_ref[...], preferred_element_type=F32)
    o_ref[...] = y.astype(o_ref.dtype)


def _norm_matmul(x, xcol, k, g, w, tm, tn, name, w_t=False):
    n, m = x.shape[0], w.shape[0 if w_t else 1]
    return pl.pallas_call(
        functools.partial(_norm_matmul_kernel, w_t=w_t),
        grid=(n // tm, m // tn),
        in_specs=[pl.BlockSpec((tm, k), lambda i, j: (i, xcol)),
                  pl.BlockSpec((1, k), lambda i, j: (0, 0)),
                  pl.BlockSpec((tn, k), lambda i, j: (j, 0)) if w_t else pl.BlockSpec((k, tn), lambda i, j: (0, j))],
        out_specs=pl.BlockSpec((tm, tn), lambda i, j: (i, j)),
        out_shape=jax.ShapeDtypeStruct((n, m), F32),
        scratch_shapes=[pltpu.VMEM((tm, k), BF16)],
        compiler_params=_params(("parallel", "arbitrary")),
        name=name,
    )(x, g, w)


def _merge_kernel(om_ref, on_ref, wm_ref, wn_ref, g0_ref, g1_ref, o_ref):
    ym = jnp.dot(om_ref[...], wm_ref[...], preferred_element_type=F32)
    yn = jnp.dot(on_ref[...], wn_ref[...], preferred_element_type=F32)
    o_ref[...] = (jax.nn.sigmoid(g0_ref[...]) * ym + jax.nn.sigmoid(g1_ref[...]) * yn).astype(BF16)


def _merge(o_mla, o_nsa, w_pm, w_pn, proj, tm=1024, tn=512):
    n = o_mla.shape[0]
    g0, g1 = C_GM // tn, (C_GM + D_MODEL) // tn
    return pl.pallas_call(
        _merge_kernel,
        grid=(n // tm, D_MODEL // tn),
        in_specs=[pl.BlockSpec((tm, D_MODEL), lambda i, j: (i, 0)),
                  pl.BlockSpec((tm, D_MODEL), lambda i, j: (i, 0)),
                  pl.BlockSpec((D_MODEL, tn), lambda i, j: (0, j)),
                  pl.BlockSpec((D_MODEL, tn), lambda i, j: (0, j)),
                  pl.BlockSpec((tm, tn), lambda i, j: (i, g0 + j)),
                  pl.BlockSpec((tm, tn), lambda i, j: (i, g1 + j))],
        out_specs=pl.BlockSpec((tm, tn), lambda i, j: (i, j)),
        out_shape=jax.ShapeDtypeStruct((n, D_MODEL), BF16),
        compiler_params=_params(("parallel", "arbitrary")),
        name="merge",
    )(o_mla, o_nsa, w_pm, w_pn, proj, proj)


def _out_proj_kernel(u_ref, w_ref, x_ref, g_ref, h_ref, hn_ref):
    h = x_ref[...] + jnp.dot(u_ref[...], w_ref[...], preferred_element_type=F32)
    h_ref[...] = h
    ms = jnp.mean(h * h, axis=-1, keepdims=True)
    hn_ref[...] = (h * lax.rsqrt(ms + EPS) * g_ref[...]).astype(BF16)


def _out_proj(u, w_out, x, g, tm=512):
    n = u.shape[0]
    row = pl.BlockSpec((tm, D_MODEL), lambda i: (i, 0))
    return pl.pallas_call(
        _out_proj_kernel,
        grid=(n // tm,),
        in_specs=[row, pl.BlockSpec((D_MODEL, D_MODEL), lambda i: (0, 0)), row,
                  pl.BlockSpec((1, D_MODEL), lambda i: (0, 0))],
        out_specs=[row, row],
        out_shape=[jax.ShapeDtypeStruct((n, D_MODEL), F32), jax.ShapeDtypeStruct((n, D_MODEL), BF16)],
        compiler_params=_params(("parallel",)),
        name="out_proj",
    )(u, w_out, x, g)


def _ffn_up_kernel(hn_ref, wg_ref, wu_ref, o_ref):
    hn = hn_ref[...]
    a = jnp.dot(hn, wg_ref[...], preferred_element_type=F32)
    b = jnp.dot(hn, wu_ref[...], preferred_element_type=F32)
    o_ref[...] = (jax.nn.silu(a) * b).astype(BF16)


def _ffn_up(hn, w_gate, w_up, tm=1024, tn=512):
    n = hn.shape[0]
    return pl.pallas_call(
        _ffn_up_kernel,
        grid=(n // tm, D_FF // tn),
        in_specs=[pl.BlockSpec((tm, D_MODEL), lambda i, j: (i, 0)),
                  pl.BlockSpec((D_MODEL, tn), lambda i, j: (0, j)),
                  pl.BlockSpec((D_MODEL, tn), lambda i, j: (0, j))],
        out_specs=pl.BlockSpec((tm, tn), lambda i, j: (i, j)),
        out_shape=jax.ShapeDtypeStruct((n, D_FF), BF16),
        compiler_params=_params(("parallel", "arbitrary")),
        name="ffn_up",
    )(hn, w_gate, w_up)


def _ffn_down_kernel(a_ref, w_ref, h_ref, o_ref):
    o_ref[...] = h_ref[...] + jnp.dot(a_ref[...], w_ref[...], preferred_element_type=F32)


def _ffn_down(act, w_down, h, tm=1024, tn=512):
    n = act.shape[0]
    return pl.pallas_call(
        _ffn_down_kernel,
        grid=(n // tm, D_MODEL // tn),
        in_specs=[pl.BlockSpec((tm, D_FF), lambda i, j: (i, 0)),
                  pl.BlockSpec((D_FF, tn), lambda i, j: (0, j)),
                  pl.BlockSpec((tm, tn), lambda i, j: (i, j))],
        out_specs=pl.BlockSpec((tm, tn), lambda i, j: (i, j)),
        out_shape=jax.ShapeDtypeStruct((n, D_MODEL), F32),
        compiler_params=_params(("parallel", "arbitrary")),
        name="ffn_down",
    )(act, w_down, h)


def _rms_inv(xa, xb, d):
    return lax.rsqrt(jnp.sum(xa * xa + xb * xb, axis=-1, keepdims=True) * (1.0 / d) + EPS)


def _rope(x, c, s):
    return x * c + pltpu.roll(x, HALF_LANES, 1) * s


def _col_max(s, slabs=8):
    rows = s.shape[0]
    if rows % (8 * slabs) == 0:
        s = jnp.max(s.reshape(slabs, rows // slabs, s.shape[1]), axis=0)
    return jnp.max(s, axis=0, keepdims=True)


def _col_sum(s, slabs=8):
    rows = s.shape[0]
    if rows % (8 * slabs) == 0:
        s = jnp.sum(s.reshape(slabs, rows // slabs, s.shape[1]), axis=0)
    return jnp.sum(s, axis=0, keepdims=True)


def _flash_t(chains, n_last, n_max, tk, s_buf0, s_buf1, mt_buf0, mt_buf1, m_ref, acc_ref):
    nc = len(chains)
    s_buf, mt_buf = (s_buf0, s_buf1), (mt_buf0, mt_buf1)

    def stage_a(j, masked=True):
        off = j * tk
        for c, (q_at, k_at, _, t_cols) in enumerate(chains):
            s = lax.dot_general(k_at(off), q_at(), _NT, preferred_element_type=F32)
            if masked:
                pk = off + lax.broadcasted_iota(I32, (tk, 1), 0)
                s = jnp.where(pk <= t_cols, s, NEG_INF)
            s_buf[j % 2][c] = s
            mt_buf[j % 2][c] = _col_max(s)

    def stage_b(j):
        off = j * tk
        for c, (_, _, vt_at, _) in enumerate(chains):
            m = m_ref[c]
            m_new = jnp.maximum(m, mt_buf[j % 2][c])
            p = jnp.exp(s_buf[j % 2][c] - m_new).astype(BF16)
            acc_ref[c] = jnp.exp(m - m_new) * acc_ref[c] + jnp.dot(vt_at(off), p, preferred_element_type=F32)
            m_ref[c] = m_new

    m_ref[...] = jnp.full(m_ref.shape, NEG_INF, F32)
    acc_ref[...] = jnp.zeros(acc_ref.shape, F32)
    stage_a(0)
    for j in range(n_max):
        if j + 1 < n_max:
            @pl.when(j + 1 < n_last)
            def _(j=j):
                stage_a(j + 1, masked=False)
                stage_b(j)

            @pl.when(j + 1 == n_last)
            def _(j=j):
                stage_a(j + 1)
                stage_b(j)

        @pl.when(j == n_last)
        def _(j=j):
            stage_b(j)

    return [acc_ref[c, :DV] * (1.0 / acc_ref[c, DV:DV + 1]) for c in range(nc)]


def _flash_scratch(nc, tk, nq):
    return [pltpu.VMEM((nc, tk, nq), F32), pltpu.VMEM((nc, tk, nq), F32),
            pltpu.VMEM((nc, 1, nq), F32), pltpu.VMEM((nc, 1, nq), F32),
            pltpu.VMEM((nc, 1, nq), F32), pltpu.VMEM((nc, DV_EXT, nq), F32)]


def _mla_kernel(q_ref, kn_ref, kr_ref, v_ref, gq_ref, gk_ref, cq_ref, sq_ref, ck_ref, sk_ref,
                o_ref, k_s, vt_s, q_s, *flash_scratch, tq, tk, scale, hps):
    qi = pl.program_id(2)
    T = kn_ref.shape[0]

    @pl.when(qi == 0)
    def _():
        kr = kr_ref[...]
        ss_r = jnp.sum(kr * kr, axis=-1, keepdims=True)
        kr_rot = _rope(kr * gk_ref[:, LANES:], ck_ref[...], sk_ref[...])
        for h in range(hps):
            kn = kn_ref[:, h * LANES:(h + 1) * LANES]
            r = lax.rsqrt((jnp.sum(kn * kn, axis=-1, keepdims=True) + ss_r) * (1.0 / MLA_QK) + EPS)
            k_s[h, :, :LANES] = (kn * r * gk_ref[:, :LANES]).astype(BF16)
            k_s[h, :, LANES:] = (kr_rot * r).astype(BF16)
            vt_s[h, :DV, :] = v_ref[:, h * LANES:(h + 1) * LANES].T.astype(BF16)
            vt_s[h, DV:, :] = jnp.ones((DV_EXT - DV, T), BF16)

    t0 = qi * tq
    t_cols = t0 + lax.broadcasted_iota(I32, (1, tq), 1)
    chains = []
    for h in range(hps):
        qa = q_ref[:, h * HEAD_PAD:h * HEAD_PAD + LANES]
        qb = q_ref[:, h * HEAD_PAD + LANES:(h + 1) * HEAD_PAD]
        r = _rms_inv(qa, qb, MLA_QK) * scale
        qa = qa * r * gq_ref[:, :LANES]
        qb = _rope(qb * r * gq_ref[:, LANES:], cq_ref[...], sq_ref[...])
        q_s[h, :, :LANES] = qa.astype(BF16)
        q_s[h, :, LANES:] = qb.astype(BF16)
        chains.append((lambda h=h: q_s[h], lambda off, h=h: k_s[h, pl.ds(off, tk), :],
                       lambda off, h=h: vt_s[h, :, pl.ds(off, tk)], t_cols))
    for h, o_t in enumerate(_flash_t(chains, t0 // tk, T // tk, tk, *flash_scratch)):
        o_ref[:, h * LANES:(h + 1) * LANES] = o_t.T.astype(BF16)


def _mla_attention(q_up, kv_up, proj, gq, gk, tabs, B, T, tq=512, tk=512, hps=4):
    nq = T // tq
    ng = MLA_HEADS // hps
    c, s = tabs
    qtab = pl.BlockSpec((tq, LANES), lambda b, h, i: (i, 0))
    ktab = pl.BlockSpec((T, LANES), lambda b, h, i: (0, 0))
    gspec = pl.BlockSpec((1, HEAD_PAD), lambda b, h, i: (0, 0))
    return pl.pallas_call(
        functools.partial(_mla_kernel, tq=tq, tk=tk, scale=MLA_QK ** -0.5, hps=hps),
        grid=(B, ng, nq),
        in_specs=[pl.BlockSpec((tq, hps * HEAD_PAD), lambda b, h, i: (b * nq + i, h)),
                  pl.BlockSpec((T, hps * LANES), lambda b, h, i: (b, h)),
                  pl.BlockSpec((T, LANES), lambda b, h, i: (b, C_KR // LANES)),
                  pl.BlockSpec((T, hps * LANES), lambda b, h, i: (b, ng + h)),
                  gspec, gspec, qtab, qtab, ktab, ktab],
        out_specs=pl.BlockSpec((tq, hps * LANES), lambda b, h, i: (b * nq + i, h)),
        out_shape=jax.ShapeDtypeStruct((B * T, MLA_HEADS * MLA_V), BF16),
        scratch_shapes=[pltpu.VMEM((hps, T, HEAD_PAD), BF16), pltpu.VMEM((hps, DV_EXT, T), BF16),
                        pltpu.VMEM((hps, tq, HEAD_PAD), BF16)] + _flash_scratch(hps, tk, tq),
        compiler_params=_params(("parallel", "parallel", "arbitrary")),
        name="mla_attention",
    )(q_up, kv_up, proj, kv_up, gq, gk, c, s, c, s)


def _compress_kernel(*refs, is_key, gb, ncp, nt):
    x_tiles = refs[:gb * nt]
    pe_ref, w1a_ref, w1b_ref, b1_ref, w2_ref, b2_ref, g_ref, c_ref, s_ref, o_ref, xa_s, xb_s = refs[gb * nt:]
    rows = gb * ncp
    for g in range(gb):
        for l in range(CMP_STRIDE):
            for t in range(nt):
                x = x_tiles[g * nt + t][pl.ds(l, ncp, stride=CMP_STRIDE), :]
                lanes = slice((l * nt + t) * LANES, (l * nt + t + 1) * LANES)
                pe_lanes = slice(t * LANES, (t + 1) * LANES)
                xa_s[g * ncp:(g + 1) * ncp, lanes] = (x + pe_ref[l:l + 1, pe_lanes]).astype(BF16)
                xb_s[g * ncp:(g + 1) * ncp, lanes] = (
                    x + pe_ref[CMP_STRIDE + l:CMP_STRIDE + l + 1, pe_lanes]).astype(BF16)
    pa = jnp.dot(xa_s[...], w1a_ref[...], preferred_element_type=F32)
    pb = jnp.dot(xb_s[...], w1b_ref[...], preferred_element_type=F32)
    h = pa + pltpu.roll(pb, rows - 1, 0) + b1_ref[...]
    y = jnp.dot(jax.nn.gelu(h).astype(BF16), w2_ref[...], preferred_element_type=F32) + b2_ref[...]
    if is_key:
        ya, yb = y[:, :LANES], y[:, LANES:]
        r = _rms_inv(ya, yb, NSA_DK)
        c = jnp.concatenate([c_ref[...]] * gb, axis=0)
        s = jnp.concatenate([s_ref[...]] * gb, axis=0)
        ya = _rope(ya * r * g_ref[:, :LANES], c, s)
        yb = yb * r * g_ref[:, LANES:]
        y = jnp.concatenate([ya, yb], axis=1)
    o_ref[0] = y.astype(BF16).reshape(gb, ncp, y.shape[-1])


def _compress(proj, c0, d, B, T, pe, w1, b1, w2, b2, g, tabs, is_key, name, gb=2):
    G = NSA_GROUPS
    ncp = T // CMP_STRIDE
    nt = -(-d // LANES)
    dp = nt * LANES
    kd = CMP_STRIDE * dp
    dout = w2.shape[1]
    hid = w1.shape[1]
    w1 = jnp.pad(w1.astype(BF16).reshape(CMP_LEN, d, hid), ((0, 0), (0, dp - d), (0, 0))).reshape(CMP_LEN * dp, hid)
    pe = _pad_cols(pe, dp)
    c, s = tabs
    const = lambda shape: pl.BlockSpec(shape, lambda b, gi: (0,) * len(shape))
    tile = lambda k: pl.BlockSpec((T, LANES), lambda b, gi: (b, c0 // LANES + gi * gb * nt + k))
    return pl.pallas_call(
        functools.partial(_compress_kernel, is_key=is_key, gb=gb, ncp=ncp, nt=nt),
        grid=(B, G // gb),
        in_specs=[tile(k) for k in range(gb * nt)] + [
                  const((CMP_LEN, dp)), const((kd, hid)), const((kd, hid)), const((1, hid)),
                  const((hid, dout)), const((1, dout)), const((1, dout)),
                  const((ncp, LANES)), const((ncp, LANES))],
        out_specs=pl.BlockSpec((1, gb, ncp, dout), lambda b, gi: (b, gi, 0, 0)),
        out_shape=jax.ShapeDtypeStruct((B, G, ncp, dout), BF16),
        scratch_shapes=[pltpu.VMEM((gb * ncp, kd), BF16), pltpu.VMEM((gb * ncp, kd), BF16)],
        compiler_params=_params(("parallel", "parallel")),
        name=name,
    )(*([proj] * (gb * nt)), pe, w1[:kd], w1[kd:], b1.reshape(1, hid), w2.astype(BF16), b2.reshape(1, dout),
      g, c, s)


def _nsa_kernel(q_ref, ks_ref, kw_ref, vs_ref, vw_ref, gate_ref, kc_ref, vc_ref, gq_ref, gks_ref, gkw_ref,
                cq_ref, sq_ref, ck_ref, sk_ref, ovt_ref,
                o_ref, ks_s, kw_s, vst_s, vwt_s, vct_s, q_s, g_s, *flash_scratch, tq, tk, T, scale):
    qi = pl.program_id(2)
    R = NSA_HPG
    nq = R * tq
    nb = T // SLC_LEN
    topk = min(SLC_TOPK, nb)
    ncp = kc_ref.shape[2]

    @pl.when(qi == 0)
    def _():
        def prep_k(k_ref, g_ref):
            ka, kb = k_ref[:, :LANES], k_ref[:, LANES:]
            r = _rms_inv(ka, kb, NSA_DK)
            return _rope(ka * r * g_ref[:, :LANES], ck_ref[...], sk_ref[...]), kb * r * g_ref[:, LANES:]

        ka, kb = prep_k(ks_ref, gks_ref)
        blk = lax.broadcasted_iota(I32, (T, LANES), 0) // SLC_LEN
        lane = lax.broadcasted_iota(I32, (T, LANES), 1)
        ks_s[:, :LANES] = ka.astype(BF16)
        ks_s[:, LANES:] = (kb + (lane - SEL_LANE0 == blk).astype(F32)).astype(BF16)
        ka, kb = prep_k(kw_ref, gkw_ref)
        lane_w = lax.broadcasted_iota(I32, (WINDOW, LANES), 1)
        kw_s[:WINDOW, :LANES] = jnp.zeros((WINDOW, LANES), BF16)
        kw_s[:WINDOW, LANES:] = jnp.where(lane_w == PAD_LANE, NEG_INF, 0.0).astype(BF16)
        kw_s[WINDOW:, :LANES] = ka.astype(BF16)
        kw_s[WINDOW:, LANES:] = kb.astype(BF16)
        vst_s[:DV, :] = vs_ref[...].T.astype(BF16)
        vst_s[DV:, :] = jnp.ones((DV_EXT - DV, T), BF16)
        vwt_s[:DV, :WINDOW] = jnp.zeros((DV, WINDOW), BF16)
        vwt_s[:DV, WINDOW:] = vw_ref[...].T.astype(BF16)
        vwt_s[DV:, :] = jnp.ones((DV_EXT - DV, WINDOW + T), BF16)
        vct_s[...] = vc_ref[0, 0].astype(F32).T.astype(BF16)

    t0 = pl.multiple_of(qi * tq, tq)
    lane_q = lax.broadcasted_iota(I32, (tq, LANES), 1)
    for r in range(R):
        qa = q_ref[:, r * HEAD_PAD:r * HEAD_PAD + LANES]
        qb = q_ref[:, r * HEAD_PAD + LANES:(r + 1) * HEAD_PAD]
        rs = _rms_inv(qa, qb, NSA_DK) * scale
        qa = _rope(qa * rs * gq_ref[:, :LANES], cq_ref[...], sq_ref[...])
        qb = jnp.where(lane_q == PAD_LANE, 1.0, qb * rs * gq_ref[:, LANES:])
        q_s[r * tq:(r + 1) * tq, :LANES] = qa.astype(BF16)
        q_s[r * tq:(r + 1) * tq, LANES:] = qb.astype(BF16)

    c_q = lax.broadcasted_iota(I32, (1, nq), 1) & (tq - 1)
    t_cols = t0 + c_q
    q = q_s[...]

    g_s[...] = jax.nn.sigmoid(gate_ref[...]).T

    span = tq + WINDOW
    sw = lax.dot_general(kw_s[pl.ds(t0, span), :], q, _NT, preferred_element_type=F32)
    sc = lax.dot_general(kc_ref[0, 0], q, _NT, preferred_element_type=F32)

    cend = lax.broadcasted_iota(I32, (ncp, 1), 0) * CMP_STRIDE + (CMP_LEN - 1)
    mask_c = cend <= t_cols
    sc = jnp.where(mask_c, sc, NEG_INF)
    e = jnp.where(mask_c, jnp.exp(sc - _col_max(sc)), 0.0)
    p = e * (1.0 / jnp.maximum(_col_sum(e), 1e-30))
    o_c = jnp.dot(vct_s[...], p.astype(BF16), preferred_element_type=F32)

    ps = p[:, :tq]
    for r in range(1, R):
        ps = ps + p[:, r * tq:(r + 1) * tq]
    ps_hi = ps.astype(BF16)
    ps_lo = (ps - ps_hi.astype(F32)).astype(BF16)
    imp = (jnp.dot(ovt_ref[...], ps_hi, preferred_element_type=F32)
           + jnp.dot(ovt_ref[...], ps_lo, preferred_element_type=F32))
    jb = lax.broadcasted_iota(I32, (nb, tq), 0)
    tb = (t0 + lax.broadcasted_iota(I32, (1, tq), 1)) // SLC_LEN
    forced = (jb == 0) | (jb == tb) | (jb == tb - 1)
    score = jnp.where(forced, FORCE, jnp.where(jb <= tb, imp, -FORCE))
    rank = jnp.zeros((nb, tq), I32)
    for j in range(nb):
        row = score[j:j + 1, :]
        rank = rank + ((row > score) | ((row == score) & (j < jb))).astype(I32)
    unsel = (rank >= topk).astype(F32)
    unsel = jnp.concatenate([jnp.zeros((SEL_LANE0, tq), F32), unsel,
                             jnp.zeros((LANES - SEL_LANE0 - nb, tq), F32)], axis=0)
    bias = unsel.T * NEG_INF

    for r in range(R):
        qb = q_s[r * tq:(r + 1) * tq, LANES:].astype(F32)
        q_s[r * tq:(r + 1) * tq, LANES:] = (qb + bias).astype(BF16)

    i_k = lax.broadcasted_iota(I32, (tq, 1), 0)
    segs = [jnp.where(i_k > c_q, sw[:tq], NEG_INF)]
    if tq < WINDOW:
        segs.append(sw[tq:WINDOW])
    segs.append(jnp.where(i_k <= c_q, sw[WINDOW:], NEG_INF))
    m = functools.reduce(jnp.maximum, [_col_max(s) for s in segs])
    pw = jnp.concatenate([jnp.exp(s - m) for s in segs], axis=0).astype(BF16)
    acc = jnp.dot(vwt_s[:, pl.ds(t0, span)], pw, preferred_element_type=F32)
    o_w = acc[:DV] * (1.0 / acc[DV:DV + 1])

    k_at = lambda off: ks_s[pl.ds(off, tk), :]
    vt_at = lambda off: vst_s[:, pl.ds(off, tk)]
    chains = [(lambda r=r: q_s[r * tq:(r + 1) * tq], k_at, vt_at, t_cols[:, r * tq:(r + 1) * tq])
              for r in range(R)]
    o_s = jnp.concatenate(_flash_t(chains, t0 // tk, T // tk, tk, *flash_scratch), axis=1)

    gs = g_s[pl.ds(pl.multiple_of(pl.program_id(1) * GATE_PITCH, GATE_PITCH), GATE_PITCH), :]
    for r in range(R):
        cols = slice(r * tq, (r + 1) * tq)
        o = (gs[3 * r:3 * r + 1] * o_c[:, cols] + gs[3 * r + 1:3 * r + 2] * o_s[:, cols]
             + gs[3 * r + 2:3 * r + 3] * o_w[:, cols])
        o_ref[:, r * DV:(r + 1) * DV] = o.T.astype(BF16)


def _nsa_attention(proj, k_c, v_c, gq, gks, gkw, tabs, ovt, B, T, tq=512, tk=512):
    nq = T // tq
    G, R = NSA_GROUPS, NSA_HPG
    ncp = k_c.shape[2]
    c, s = tabs
    qtab = pl.BlockSpec((tq, LANES), lambda b, g, i: (i, 0))
    ktab = pl.BlockSpec((T, LANES), lambda b, g, i: (0, 0))
    gspec = pl.BlockSpec((1, HEAD_PAD), lambda b, g, i: (0, 0))
    return pl.pallas_call(
        functools.partial(_nsa_kernel, tq=tq, tk=tk, T=T, scale=NSA_DK ** -0.5),
        grid=(B, G, nq),
        in_specs=[pl.BlockSpec((tq, R * HEAD_PAD), lambda b, g, i: (b * nq + i, g)),
                  pl.BlockSpec((T, HEAD_PAD), lambda b, g, i: (b, C_KS // HEAD_PAD + g)),
                  pl.BlockSpec((T, HEAD_PAD), lambda b, g, i: (b, C_KW // HEAD_PAD + g)),
                  pl.BlockSpec((T, DV), lambda b, g, i: (b, C_VS // DV + g)),
                  pl.BlockSpec((T, DV), lambda b, g, i: (b, C_VW // DV + g)),
                  pl.BlockSpec((tq, LANES), lambda b, g, i: (b * nq + i, C_GATE // LANES)),
                  pl.BlockSpec((1, 1, ncp, HEAD_PAD), lambda b, g, i: (b, g, 0, 0)),
                  pl.BlockSpec((1, 1, ncp, DV), lambda b, g, i: (b, g, 0, 0)),
                  gspec, gspec, gspec, qtab, qtab, ktab, ktab,
                  pl.BlockSpec(ovt.shape, lambda b, g, i: (0, 0))],
        out_specs=pl.BlockSpec((tq, R * DV), lambda b, g, i: (b * nq + i, g)),
        out_shape=jax.ShapeDtypeStruct((B * T, NSA_HEADS * DV), BF16),
        scratch_shapes=[pltpu.VMEM((T, HEAD_PAD), BF16), pltpu.VMEM((WINDOW + T, HEAD_PAD), BF16),
                        pltpu.VMEM((DV_EXT, T), BF16), pltpu.VMEM((DV_EXT, WINDOW + T), BF16),
                        pltpu.VMEM((DV, ncp), BF16), pltpu.VMEM((R * tq, HEAD_PAD), BF16),
                        pltpu.VMEM((LANES, tq), F32)]
        + _flash_scratch(R, tk, tq),
        compiler_params=_params(("parallel", "parallel", "arbitrary")),
        name="nsa_attention",
    )(proj, proj, proj, proj, proj, proj, k_c, v_c, gq, gks, gkw, c, s, c, s, ovt)


def _pad_cols(w, n):
    return jnp.pad(w, ((0, 0), (0, n - w.shape[1])))


def _take_cols(w, plan):
    zeros = lambda n: jnp.zeros((w.shape[0], n), w.dtype)
    return jnp.concatenate([zeros(p) if isinstance(p, int) else w[:, p[0]:p[1]] for p in plan], axis=1)


def _nsa_head_plan(c0):
    h = NSA_ROPE // 2
    fill = HALF_LANES - h
    return [(c0, c0 + h), (c0 + 2 * h, c0 + 2 * h + fill), (c0 + h, c0 + 2 * h),
            (c0 + 2 * h + fill, c0 + NSA_DK), HEAD_PAD - NSA_DK]


def _mla_rope_plan(c0):
    h = MLA_ROPE // 2
    return [(c0, c0 + h), HALF_LANES - h, (c0 + h, c0 + 2 * h), HALF_LANES - h]


def _mla_head_plan(c0):
    return [(c0, c0 + MLA_NOPE)] + _mla_rope_plan(c0 + MLA_NOPE)


def _heads(plan_fn, c0, nh, d):
    return [piece for h in range(nh) for piece in plan_fn(c0 + h * d)]


def _nsa_head_layout(w, nh):
    return _take_cols(w, _heads(_nsa_head_plan, 0, nh, NSA_DK))


def _mla_head_layout(w, nh):
    return _take_cols(w, _heads(_mla_head_plan, 0, nh, MLA_QK))


def _layout_w_in(w_in):
    sizes = (MLA_RANK, MLA_RANK, MLA_ROPE, NSA_HEADS * NSA_DK,
             NSA_GROUPS * NSA_DK, NSA_GROUPS * NSA_DV, NSA_GROUPS * NSA_DK, NSA_GROUPS * NSA_DV,
             NSA_GROUPS * NSA_DK, NSA_GROUPS * NSA_DV, NSA_HEADS * 3, 2 * D_MODEL)
    offs, o = [], 0
    for s in sizes:
        offs.append(o)
        o += s
    cq, ckv, kr, qn, kc, vc, ks, vs, kw, vw, gn, gm = offs
    whole = lambda c0, n: [(c0, c0 + n)]
    ngate = NSA_HPG * 3
    plan = (_heads(_nsa_head_plan, qn, NSA_HEADS, NSA_DK) + whole(gm, 2 * D_MODEL)
            + _heads(_nsa_head_plan, ks, NSA_GROUPS, NSA_DK) + _heads(_nsa_head_plan, kw, NSA_GROUPS, NSA_DK)
            + whole(vs, NSA_GROUPS * NSA_DV) + whole(vw, NSA_GROUPS * NSA_DV)
            + whole(cq, MLA_RANK) + whole(ckv, MLA_RANK) + _mla_rope_plan(kr)
            + _heads(lambda c: [(c, c + ngate), GATE_PITCH - ngate], gn, NSA_GROUPS, ngate)
            + [LANES - NSA_GROUPS * GATE_PITCH]
            + _heads(lambda c: [(c, c + NSA_DK), HEAD_PAD - NSA_DK], kc, NSA_GROUPS, NSA_DK)
            + whole(vc, NSA_GROUPS * NSA_DV))
    width = sum(p if isinstance(p, int) else p[1] - p[0] for p in plan)
    assert width == C_VC + NSA_GROUPS * NSA_DV <= C_END
    return _permute_rows(w_in.T, plan + [C_END - width])


def _permute_rows_kernel(tab_ref, a_ref, b_ref, p_ref, o_ref, *, n_src):
    j = pl.program_id(0)
    row = lax.broadcasted_iota(I32, (LANES, 1), 0)
    acc = None
    for k, ref in enumerate((a_ref, b_ref)):
        in_bounds = tab_ref[j, k] * LANES + row < n_src
        x = jnp.where(in_bounds, ref[...], 0.0).astype(BF16)
        y = jnp.dot(p_ref[0, k], x, preferred_element_type=F32)
        acc = y if acc is None else acc + y
    o_ref[...] = acc.astype(BF16)


def _permute_rows(w, plan):
    n_src, cols = w.shape
    src = np.concatenate([np.full(p, -1) if isinstance(p, int) else np.arange(p[0], p[1]) for p in plan])
    nt = len(src) // LANES
    tab = np.zeros((nt, 2), np.int32)
    place = np.zeros((nt, 2, LANES, LANES), np.float32)
    for t in range(nt):
        rows = src[t * LANES:(t + 1) * LANES]
        used = rows[rows >= 0]
        if used.size == 0:
            continue
        t0, t1 = used.min() // LANES, used.max() // LANES
        assert t1 - t0 <= 1
        tab[t] = (t0, t1)
        for out_row, r in enumerate(rows):
            if r >= 0:
                place[t, r // LANES - t0, out_row, r % LANES] = 1.0
    src_tile = lambda k: pl.BlockSpec((LANES, cols), lambda j, tab: (tab[j, k], 0))
    return pl.pallas_call(
        functools.partial(_permute_rows_kernel, n_src=n_src),
        grid_spec=pltpu.PrefetchScalarGridSpec(
            num_scalar_prefetch=1, grid=(nt,),
            in_specs=[src_tile(0), src_tile(1), pl.BlockSpec((1, 2, LANES, LANES), lambda j, tab: (j, 0, 0, 0))],
            out_specs=pl.BlockSpec((LANES, cols), lambda j, tab: (j, 0))),
        out_shape=jax.ShapeDtypeStruct((nt * LANES, cols), BF16),
        compiler_params=_params(("parallel",)),
        name="weight_layout",
    )(jnp.asarray(tab), w, w, jnp.asarray(place, BF16))


def _rope_tables(pos, rot_dim):
    half = rot_dim // 2
    inv = ROPE_THETA ** (-np.arange(0, rot_dim, 2, dtype=np.float64) / rot_dim)
    ang = np.asarray(pos, np.float64)[:, None] * inv[None, :]
    c = np.ones((len(pos), LANES))
    s = np.zeros((len(pos), LANES))
    c[:, :half] = c[:, HALF_LANES:HALF_LANES + half] = np.cos(ang)
    s[:, :half] = -np.sin(ang)
    s[:, HALF_LANES:HALF_LANES + half] = np.sin(ang)
    return jnp.asarray(c, F32), jnp.asarray(s, F32)


def _overlap_table(ncp, nb):
    cs = np.arange(ncp)[None, :] * CMP_STRIDE
    ss = np.arange(nb)[:, None] * SLC_LEN
    ov = np.clip(np.minimum(cs + CMP_LEN, ss + SLC_LEN) - np.maximum(cs, ss), 0, None) / CMP_LEN
    return jnp.asarray(ov, BF16)


def _layer(x, attn_norm_g, w_in, mla_q_lat_g, mla_kv_lat_g, mla_w_uq, mla_w_uk, mla_w_uv,
           mla_q_norm_g, mla_k_norm_g, nsa_q_norm_g, nsa_k_norm_g, cmp_pe_k, cmp_pe_v,
           cmp_k_w1, cmp_k_b1, cmp_k_w2, cmp_k_b2, cmp_v_w1, cmp_v_b1, cmp_v_w2, cmp_v_b2,
           w_proj_mla, w_proj_nsa, w_out, ffn_norm_g, w_ffn_gate, w_ffn_up, w_ffn_down):
    B, T, _ = x.shape
    n = B * T
    ncp = T // CMP_STRIDE
    nb = T // SLC_LEN
    assert T % 512 == 0 and ncp % LANES == 0 and nb % 8 == 0 and SEL_LANE0 + nb <= PAD_LANE and T >= WINDOW
    x2 = x.reshape(n, D_MODEL)
    pos = np.arange(T)

    proj = _norm_matmul(x2, 0, D_MODEL, attn_norm_g.reshape(1, -1), _layout_w_in(w_in), 1024, 1024, "in_proj",
                        w_t=True)

    q_up = _norm_matmul(proj, C_CQ // MLA_RANK, MLA_RANK, mla_q_lat_g.reshape(1, -1),
                        _mla_head_layout(mla_w_uq.astype(BF16), MLA_HEADS), 1024, 2048, "mla_q_up")
    kv_up = _norm_matmul(proj, C_CKV // MLA_RANK, MLA_RANK, mla_kv_lat_g.reshape(1, -1),
                         jnp.concatenate([mla_w_uk, mla_w_uv], axis=1).astype(BF16), 1024, 2048, "mla_kv_up")
    o_mla = _mla_attention(q_up, kv_up, proj, _mla_head_layout(mla_q_norm_g[None], 1),
                           _mla_head_layout(mla_k_norm_g[None], 1), _rope_tables(pos, MLA_ROPE), B, T)

    cmp_tabs = _rope_tables(np.arange(ncp) * CMP_STRIDE + CMP_LEN - 1, NSA_ROPE)
    k_c = _compress(proj, C_KC, NSA_DK, B, T, cmp_pe_k, cmp_k_w1, cmp_k_b1, _nsa_head_layout(cmp_k_w2, 1),
                    _nsa_head_layout(cmp_k_b2[None], 1)[0], _nsa_head_layout(nsa_k_norm_g[0:1], 1), cmp_tabs,
                    True, "compress_k")
    v_c = _compress(proj, C_VC, NSA_DV, B, T, cmp_pe_v, cmp_v_w1, cmp_v_b1, cmp_v_w2, cmp_v_b2,
                    jnp.zeros((1, NSA_DV), F32), cmp_tabs, False, "compress_v")
    o_nsa = _nsa_attention(proj, k_c, v_c, _nsa_head_layout(nsa_q_norm_g[None], 1),
                           _nsa_head_layout(nsa_k_norm_g[1:2], 1), _nsa_head_layout(nsa_k_norm_g[2:3], 1),
                           _rope_tables(pos, NSA_ROPE), _overlap_table(ncp, nb), B, T)

    u = _merge(o_mla, o_nsa, w_proj_mla.astype(BF16), w_proj_nsa.astype(BF16), proj)
    h, hn = _out_proj(u, w_out.astype(BF16), x2, ffn_norm_g.reshape(1, -1))
    act = _ffn_up(hn, w_ffn_gate.astype(BF16), w_ffn_up.astype(BF16))
    out = _ffn_down(act, w_ffn_down.astype(BF16), h)
    return out.reshape(B, T, D_MODEL)


def kernel(x, attn_norm_g, w_in, mla_q_lat_g, mla_kv_lat_g, mla_w_uq, mla_w_uk, mla_w_uv, mla_q_norm_g, mla_k_norm_g, nsa_q_norm_g, nsa_k_norm_g, cmp_pe_k, cmp_pe_v, cmp_k_w1, cmp_k_b1, cmp_k_w2, cmp_k_b2, cmp_v_w1, cmp_v_b1, cmp_v_w2, cmp_v_b2, w_proj_mla, w_proj_nsa, w_out, ffn_norm_g, w_ffn_gate, w_ffn_up, w_ffn_down):
    params = (attn_norm_g, w_in, mla_q_lat_g, mla_kv_lat_g, mla_w_uq, mla_w_uk, mla_w_uv, mla_q_norm_g,
              mla_k_norm_g, nsa_q_norm_g, nsa_k_norm_g, cmp_pe_k, cmp_pe_v, cmp_k_w1, cmp_k_b1, cmp_k_w2,
              cmp_k_b2, cmp_v_w1, cmp_v_b1, cmp_v_w2, cmp_v_b2, w_proj_mla, w_proj_nsa, w_out, ffn_norm_g,
              w_ffn_gate, w_ffn_up, w_ffn_down)
    depth = attn_norm_g.shape[0]
    h = x
    for l in range(depth):
        h = _layer(h, *(p.reshape(p.shape[1:]) if depth == 1 else p[l] for p in params))
    return h
```

```python
import functools

import numpy as np
import jax
import jax.numpy as jnp
from jax import lax
from jax.experimental import pallas as pl
from jax.experimental.pallas import tpu as pltpu

F32, BF16, I32 = jnp.float32, jnp.bfloat16, jnp.int32

EPS = 1e-6
NEG_INF = -1e30
FORCE = 1e9
ROPE_THETA = 500000.0

D_MODEL = 2048
MLA_HEADS, MLA_NOPE, MLA_ROPE, MLA_V = 16, 128, 64, 128
MLA_QK = MLA_NOPE + MLA_ROPE
MLA_RANK = 512
NSA_HEADS, NSA_GROUPS, NSA_DK, NSA_DV = 16, 4, 192, 128
NSA_HPG = NSA_HEADS // NSA_GROUPS
NSA_ROPE = NSA_DK // 4
CMP_LEN, CMP_STRIDE = 32, 16
SLC_LEN, SLC_TOPK, WINDOW = 64, 16, 512
D_FF = 5632

LANES = 128
HALF_LANES = 64
HEAD_PAD = 256
SEL_LANE0 = 64
PAD_LANE = 127
DV = 128
DV_EXT = DV + 16
VMEM_LIMIT = 48 * 1024 * 1024

C_QN, C_GM, C_KS, C_KW, C_VS, C_VW = 0, 4096, 8192, 9216, 10240, 10752
C_CQ, C_CKV, C_KR, C_GATE, C_KC, C_VC, C_END = 11264, 11776, 12288, 12416, 12544, 13568, 14336
GATE_PITCH = 16

_NT = (((1,), (1,)), ((), ()))


def _params(sem):
    return pltpu.CompilerParams(dimension_semantics=sem, vmem_limit_bytes=VMEM_LIMIT)


def _norm_matmul_kernel(x_ref, g_ref, w_ref, o_ref, xn_ref, *, w_t):
    @pl.when(pl.program_id(1) == 0)
    def _():
        x = x_ref[...]
        ms = jnp.mean(x * x, axis=-1, keepdims=True)
        xn_ref[...] = (x * lax.rsqrt(ms + EPS) * g_ref[...]).astype(BF16)

    if w_t:
        y = lax.dot_general(xn_ref[...], w_ref[...], _NT, preferred_element_type=F32)
    else:
        y = jnp.dot(xn_ref[...], w_ref[...], preferred_element_type=F32)
    o_ref[...] = y.astype(o_ref.dtype)


def _norm_matmul(x, xcol, k, g, w, tm, tn, name, w_t=False, out_dtype=F32):
    n, m = x.shape[0], w.shape[0 if w_t else 1]
    return pl.pallas_call(
        functools.partial(_norm_matmul_kernel, w_t=w_t),
        grid=(n // tm, m // tn),
        in_specs=[pl.BlockSpec((tm, k), lambda i, j: (i, xcol)),
                  pl.BlockSpec((1, k), lambda i, j: (0, 0)),
                  pl.BlockSpec((tn, k), lambda i, j: (j, 0)) if w_t else pl.BlockSpec((k, tn), lambda i, j: (0, j))],
        out_specs=pl.BlockSpec((tm, tn), lambda i, j: (i, j)),
        out_shape=jax.ShapeDtypeStruct((n, m), out_dtype),
        scratch_shapes=[pltpu.VMEM((tm, k), BF16)],
        compiler_params=_params(("parallel", "arbitrary")),
        name=name,
    )(x, g, w)


def _merge_kernel(om_ref, on_ref, wm_ref, wn_ref, g0_ref, g1_ref, o_ref):
    ym = jnp.dot(om_ref[...], wm_ref[...], preferred_element_type=F32)
    yn = jnp.dot(on_ref[...], wn_ref[...], preferred_element_type=F32)
    o_ref[...] = (jax.nn.sigmoid(g0_ref[...]) * ym + jax.nn.sigmoid(g1_ref[...]) * yn).astype(BF16)


def _merge(o_mla, o_nsa, w_pm, w_pn, proj, tm=1024, tn=512):
    n = o_mla.shape[0]
    g0, g1 = C_GM // tn, (C_GM + D_MODEL) // tn
    return pl.pallas_call(
        _merge_kernel,
        grid=(n // tm, D_MODEL // tn),
        in_specs=[pl.BlockSpec((tm, D_MODEL), lambda i, j: (i, 0)),
                  pl.BlockSpec((tm, D_MODEL), lambda i, j: (i, 0)),
                  pl.BlockSpec((D_MODEL, tn), lambda i, j: (0, j)),
                  pl.BlockSpec((D_MODEL, tn), lambda i, j: (0, j)),
                  pl.BlockSpec((tm, tn), lambda i, j: (i, g0 + j)),
                  pl.BlockSpec((tm, tn), lambda i, j: (i, g1 + j))],
        out_specs=pl.BlockSpec((tm, tn), lambda i, j: (i, j)),
        out_shape=jax.ShapeDtypeStruct((n, D_MODEL), BF16),
        compiler_params=_params(("parallel", "arbitrary")),
        name="merge",
    )(o_mla, o_nsa, w_pm, w_pn, proj, proj)


def _out_proj_kernel(u_ref, w_ref, x_ref, g_ref, h_ref, hn_ref):
    h = x_ref[...] + jnp.dot(u_ref[...], w_ref[...], preferred_element_type=F32)
    h_ref[...] = h
    ms = jnp.mean(h * h, axis=-1, keepdims=True)
    hn_ref[...] = (h * lax.rsqrt(ms + EPS) * g_ref[...]).astype(BF16)


def _out_proj(u, w_out, x, g, tm=512):
    n = u.shape[0]
    row = pl.BlockSpec((tm, D_MODEL), lambda i: (i, 0))
    return pl.pallas_call(
        _out_proj_kernel,
        grid=(n // tm,),
        in_specs=[row, pl.BlockSpec((D_MODEL, D_MODEL), lambda i: (0, 0)), row,
                  pl.BlockSpec((1, D_MODEL), lambda i: (0, 0))],
        out_specs=[row, row],
        out_shape=[jax.ShapeDtypeStruct((n, D_MODEL), F32), jax.ShapeDtypeStruct((n, D_MODEL), BF16)],
        compiler_params=_params(("parallel",)),
        name="out_proj",
    )(u, w_out, x, g)


def _ffn_up_kernel(hn_ref, wg_ref, wu_ref, o_ref):
    hn = hn_ref[...]
    a = jnp.dot(hn, wg_ref[...], preferred_element_type=F32)
    b = jnp.dot(hn, wu_ref[...], preferred_element_type=F32)
    o_ref[...] = (jax.nn.silu(a) * b).astype(BF16)


def _ffn_up(hn, w_gate, w_up, tm=1024, tn=512):
    n = hn.shape[0]
    return pl.pallas_call(
        _ffn_up_kernel,
        grid=(n // tm, D_FF // tn),
        in_specs=[pl.BlockSpec((tm, D_MODEL), lambda i, j: (i, 0)),
                  pl.BlockSpec((D_MODEL, tn), lambda i, j: (0, j)),
                  pl.BlockSpec((D_MODEL, tn), lambda i, j: (0, j))],
        out_specs=pl.BlockSpec((tm, tn), lambda i, j: (i, j)),
        out_shape=jax.ShapeDtypeStruct((n, D_FF), BF16),
        compiler_params=_params(("parallel", "arbitrary")),
        name="ffn_up",
    )(hn, w_gate, w_up)


def _ffn_down_kernel(a_ref, w_ref, h_ref, o_ref):
    o_ref[...] = h_ref[...] + jnp.dot(a_ref[...], w_ref[...], preferred_element_type=F32)


def _ffn_down(act, w_down, h, tm=1024, tn=512):
    n = act.shape[0]
    return pl.pallas_call(
        _ffn_down_kernel,
        grid=(n // tm, D_MODEL // tn),
        in_specs=[pl.BlockSpec((tm, D_FF), lambda i, j: (i, 0)),
                  pl.BlockSpec((D_FF, tn), lambda i, j: (0, j)),
                  pl.BlockSpec((tm, tn), lambda i, j: (i, j))],
        out_specs=pl.BlockSpec((tm, tn), lambda i, j: (i, j)),
        out_shape=jax.ShapeDtypeStruct((n, D_MODEL), F32),
        compiler_params=_params(("parallel", "arbitrary")),
        name="ffn_down",
    )(act, w_down, h)


def _rms_inv(xa, xb, d):
    return lax.rsqrt(jnp.sum(xa * xa + xb * xb, axis=-1, keepdims=True) * (1.0 / d) + EPS)


def _rope(x, c, s):
    return x * c + pltpu.roll(x, HALF_LANES, 1) * s


def _col_max(s, slabs=8):
    rows = s.shape[0]
    if rows % (8 * slabs) == 0:
        s = jnp.max(s.reshape(slabs, rows // slabs, s.shape[1]), axis=0)
    return jnp.max(s, axis=0, keepdims=True)


def _col_sum(s, slabs=8):
    rows = s.shape[0]
    if rows % (8 * slabs) == 0:
        s = jnp.sum(s.reshape(slabs, rows // slabs, s.shape[1]), axis=0)
    return jnp.sum(s, axis=0, keepdims=True)


def _flash_t(chains, n_last, n_max, tk, s_buf0, s_buf1, mt_buf0, mt_buf1, m_ref, acc_ref):
    nc = len(chains)
    s_buf, mt_buf = (s_buf0, s_buf1), (mt_buf0, mt_buf1)

    def stage_a(j, masked=True):
        off = j * tk
        for c, (q_at, k_at, _, t_cols) in enumerate(chains):
            s = lax.dot_general(k_at(off), q_at(), _NT, preferred_element_type=F32)
            if masked:
                pk = off + lax.broadcasted_iota(I32, (tk, 1), 0)
                s = jnp.where(pk <= t_cols, s, NEG_INF)
            s_buf[j % 2][c] = s
            mt_buf[j % 2][c] = _col_max(s)

    def stage_b(j):
        off = j * tk
        for c, (_, _, vt_at, _) in enumerate(chains):
            m = m_ref[c]
            m_new = jnp.maximum(m, mt_buf[j % 2][c])
            p = jnp.exp(s_buf[j % 2][c] - m_new).astype(BF16)
            acc_ref[c] = jnp.exp(m - m_new) * acc_ref[c] + jnp.dot(vt_at(off), p, preferred_element_type=F32)
            m_ref[c] = m_new

    m_ref[...] = jnp.full(m_ref.shape, NEG_INF, F32)
    acc_ref[...] = jnp.zeros(acc_ref.shape, F32)
    stage_a(0)
    for j in range(n_max):
        if j + 1 < n_max:
            @pl.when(j + 1 < n_last)
            def _(j=j):
                stage_a(j + 1, masked=False)
                stage_b(j)

            @pl.when(j + 1 == n_last)
            def _(j=j):
                stage_a(j + 1)
                stage_b(j)

        @pl.when(j == n_last)
        def _(j=j):
            stage_b(j)

    return [acc_ref[c, :DV] * (1.0 / acc_ref[c, DV:DV + 1]) for c in range(nc)]


def _flash_scratch(nc, tk, nq):
    return [pltpu.VMEM((nc, tk, nq), F32), pltpu.VMEM((nc, tk, nq), F32),
            pltpu.VMEM((nc, 1, nq), F32), pltpu.VMEM((nc, 1, nq), F32),
            pltpu.VMEM((nc, 1, nq), F32), pltpu.VMEM((nc, DV_EXT, nq), F32)]


def _mla_kernel(q_ref, kn_ref, kr_ref, v_ref, gq_ref, gk_ref, cq_ref, sq_ref, ck_ref, sk_ref,
                o_ref, k_s, vt_s, q_s, *flash_scratch, tq, tk, scale, hps):
    qi = pl.program_id(2)
    T = kn_ref.shape[0]

    @pl.when(qi == 0)
    def _():
        kr = kr_ref[...]
        ss_r = jnp.sum(kr * kr, axis=-1, keepdims=True)
        kr_rot = _rope(kr * gk_ref[:, LANES:], ck_ref[...], sk_ref[...])
        for h in range(hps):
            kn = kn_ref[:, h * LANES:(h + 1) * LANES].astype(F32)
            r = lax.rsqrt((jnp.sum(kn * kn, axis=-1, keepdims=True) + ss_r) * (1.0 / MLA_QK) + EPS)
            k_s[h, :, :LANES] = (kn * r * gk_ref[:, :LANES]).astype(BF16)
            k_s[h, :, LANES:] = (kr_rot * r).astype(BF16)
            vt_s[h, :DV, :] = v_ref[:, h * LANES:(h + 1) * LANES].astype(F32).T.astype(BF16)
            vt_s[h, DV:, :] = jnp.ones((DV_EXT - DV, T), BF16)

    t0 = qi * tq
    t_cols = t0 + lax.broadcasted_iota(I32, (1, tq), 1)
    chains = []
    for h in range(hps):
        qa = q_ref[:, h * HEAD_PAD:h * HEAD_PAD + LANES].astype(F32)
        qb = q_ref[:, h * HEAD_PAD + LANES:(h + 1) * HEAD_PAD].astype(F32)
        r = _rms_inv(qa, qb, MLA_QK) * scale
        qa = qa * r * gq_ref[:, :LANES]
        qb = _rope(qb * r * gq_ref[:, LANES:], cq_ref[...], sq_ref[...])
        q_s[h, :, :LANES] = qa.astype(BF16)
        q_s[h, :, LANES:] = qb.astype(BF16)
        chains.append((lambda h=h: q_s[h], lambda off, h=h: k_s[h, pl.ds(off, tk), :],
                       lambda off, h=h: vt_s[h, :, pl.ds(off, tk)], t_cols))
    for h, o_t in enumerate(_flash_t(chains, t0 // tk, T // tk, tk, *flash_scratch)):
        o_ref[:, h * LANES:(h + 1) * LANES] = o_t.T.astype(BF16)


def _mla_attention(q_up, kv_up, proj, gq, gk, tabs, B, T, tq=512, tk=512, hps=4):
    nq = T // tq
    ng = MLA_HEADS // hps
    c, s = tabs
    qtab = pl.BlockSpec((tq, LANES), lambda b, h, i: (i, 0))
    ktab = pl.BlockSpec((T, LANES), lambda b, h, i: (0, 0))
    gspec = pl.BlockSpec((1, HEAD_PAD), lambda b, h, i: (0, 0))
    return pl.pallas_call(
        functools.partial(_mla_kernel, tq=tq, tk=tk, scale=MLA_QK ** -0.5, hps=hps),
        grid=(B, ng, nq),
        in_specs=[pl.BlockSpec((tq, hps * HEAD_PAD), lambda b, h, i: (b * nq + i, h)),
                  pl.BlockSpec((T, hps * LANES), lambda b, h, i: (b, h)),
                  pl.BlockSpec((T, LANES), lambda b, h, i: (b, C_KR // LANES)),
                  pl.BlockSpec((T, hps * LANES), lambda b, h, i: (b, ng + h)),
                  gspec, gspec, qtab, qtab, ktab, ktab],
        out_specs=pl.BlockSpec((tq, hps * LANES), lambda b, h, i: (b * nq + i, h)),
        out_shape=jax.ShapeDtypeStruct((B * T, MLA_HEADS * MLA_V), BF16),
        scratch_shapes=[pltpu.VMEM((hps, T, HEAD_PAD), BF16), pltpu.VMEM((hps, DV_EXT, T), BF16),
                        pltpu.VMEM((hps, tq, HEAD_PAD), BF16)] + _flash_scratch(hps, tk, tq),
        compiler_params=_params(("parallel", "parallel", "arbitrary")),
        name="mla_attention",
    )(q_up, kv_up, proj, kv_up, gq, gk, c, s, c, s)


def _compress_kernel(*refs, is_key, gb, ncp, nt):
    x_tiles = refs[:gb * nt]
    pe_ref, w1a_ref, w1b_ref, b1_ref, w2_ref, b2_ref, g_ref, c_ref, s_ref, o_ref, xa_s, xb_s = refs[gb * nt:]
    rows = gb * ncp
    for g in range(gb):
        for l in range(CMP_STRIDE):
            for t in range(nt):
                x = x_tiles[g * nt + t][pl.ds(l, ncp, stride=CMP_STRIDE), :]
                lanes = slice((l * nt + t) * LANES, (l * nt + t + 1) * LANES)
                pe_lanes = slice(t * LANES, (t + 1) * LANES)
                xa_s[g * ncp:(g + 1) * ncp, lanes] = (x + pe_ref[l:l + 1, pe_lanes]).astype(BF16)
                xb_s[g * ncp:(g + 1) * ncp, lanes] = (
                    x + pe_ref[CMP_STRIDE + l:CMP_STRIDE + l + 1, pe_lanes]).astype(BF16)
    pa = jnp.dot(xa_s[...], w1a_ref[...], preferred_element_type=F32)
    pb = jnp.dot(xb_s[...], w1b_ref[...], preferred_element_type=F32)
    h = pa + pltpu.roll(pb, rows - 1, 0) + b1_ref[...]
    y = jnp.dot(jax.nn.gelu(h).astype(BF16), w2_ref[...], preferred_element_type=F32) + b2_ref[...]
    if is_key:
        ya, yb = y[:, :LANES], y[:, LANES:]
        r = _rms_inv(ya, yb, NSA_DK)
        c = jnp.concatenate([c_ref[...]] * gb, axis=0)
        s = jnp.concatenate([s_ref[...]] * gb, axis=0)
        ya = _rope(ya * r * g_ref[:, :LANES], c, s)
        yb = yb * r * g_ref[:, LANES:]
        y = jnp.concatenate([ya, yb], axis=1)
    o_ref[0] = y.astype(BF16).reshape(gb, ncp, y.shape[-1])


def _compress(proj, c0, d, B, T, pe, w1, b1, w2, b2, g, tabs, is_key, name, gb=2):
    G = NSA_GROUPS
    ncp = T // CMP_STRIDE
    nt = -(-d // LANES)
    dp = nt * LANES
    kd = CMP_STRIDE * dp
    dout = w2.shape[1]
    hid = w1.shape[1]
    w1 = jnp.pad(w1.astype(BF16).reshape(CMP_LEN, d, hid), ((0, 0), (0, dp - d), (0, 0))).reshape(CMP_LEN * dp, hid)
    pe = _pad_cols(pe, dp)
    c, s = tabs
    const = lambda shape: pl.BlockSpec(shape, lambda b, gi: (0,) * len(shape))
    tile = lambda k: pl.BlockSpec((T, LANES), lambda b, gi: (b, c0 // LANES + gi * gb * nt + k))
    return pl.pallas_call(
        functools.partial(_compress_kernel, is_key=is_key, gb=gb, ncp=ncp, nt=nt),
        grid=(B, G // gb),
        in_specs=[tile(k) for k in range(gb * nt)] + [
                  const((CMP_LEN, dp)), const((kd, hid)), const((kd, hid)), const((1, hid)),
                  const((hid, dout)), const((1, dout)), const((1, dout)),
                  const((ncp, LANES)), const((ncp, LANES))],
        out_specs=pl.BlockSpec((1, gb, ncp, dout), lambda b, gi: (b, gi, 0, 0)),
        out_shape=jax.ShapeDtypeStruct((B, G, ncp, dout), BF16),
        scratch_shapes=[pltpu.VMEM((gb * ncp, kd), BF16), pltpu.VMEM((gb * ncp, kd), BF16)],
        compiler_params=_params(("parallel", "parallel")),
        name=name,
    )(*([proj] * (gb * nt)), pe, w1[:kd], w1[kd:], b1.reshape(1, hid), w2.astype(BF16), b2.reshape(1, dout),
      g, c, s)


def _nsa_kernel(q_ref, ks_ref, kw_ref, vs_ref, vw_ref, gate_ref, kc_ref, vc_ref, gq_ref, gks_ref, gkw_ref,
                cq_ref, sq_ref, ck_ref, sk_ref, ovt_ref,
                o_ref, ks_s, kw_s, vst_s, vwt_s, vct_s, q_s, g_s, *flash_scratch, tq, tk, T, scale):
    qi = pl.program_id(2)
    R = NSA_HPG
    nq = R * tq
    nb = T // SLC_LEN
    topk = min(SLC_TOPK, nb)
    ncp = kc_ref.shape[2]

    @pl.when(qi == 0)
    def _():
        def prep_k(k_ref, g_ref):
            ka, kb = k_ref[:, :LANES], k_ref[:, LANES:]
            r = _rms_inv(ka, kb, NSA_DK)
            return _rope(ka * r * g_ref[:, :LANES], ck_ref[...], sk_ref[...]), kb * r * g_ref[:, LANES:]

        ka, kb = prep_k(ks_ref, gks_ref)
        blk = lax.broadcasted_iota(I32, (T, LANES), 0) // SLC_LEN
        lane = lax.broadcasted_iota(I32, (T, LANES), 1)
        ks_s[:, :LANES] = ka.astype(BF16)
        ks_s[:, LANES:] = (kb + (lane - SEL_LANE0 == blk).astype(F32)).astype(BF16)
        ka, kb = prep_k(kw_ref, gkw_ref)
        lane_w = lax.broadcasted_iota(I32, (WINDOW, LANES), 1)
        kw_s[:WINDOW, :LANES] = jnp.zeros((WINDOW, LANES), BF16)
        kw_s[:WINDOW, LANES:] = jnp.where(lane_w == PAD_LANE, NEG_INF, 0.0).astype(BF16)
        kw_s[WINDOW:, :LANES] = ka.astype(BF16)
        kw_s[WINDOW:, LANES:] = kb.astype(BF16)
        vst_s[:DV, :] = vs_ref[...].T.astype(BF16)
        vst_s[DV:, :] = jnp.ones((DV_EXT - DV, T), BF16)
        vwt_s[:DV, :WINDOW] = jnp.zeros((DV, WINDOW), BF16)
        vwt_s[:DV, WINDOW:] = vw_ref[...].T.astype(BF16)
        vwt_s[DV:, :] = jnp.ones((DV_EXT - DV, WINDOW + T), BF16)
        vct_s[...] = vc_ref[0, 0].astype(F32).T.astype(BF16)

    t0 = pl.multiple_of(qi * tq, tq)
    lane_q = lax.broadcasted_iota(I32, (tq, LANES), 1)
    for r in range(R):
        qa = q_ref[:, r * HEAD_PAD:r * HEAD_PAD + LANES]
        qb = q_ref[:, r * HEAD_PAD + LANES:(r + 1) * HEAD_PAD]
        rs = _rms_inv(qa, qb, NSA_DK) * scale
        qa = _rope(qa * rs * gq_ref[:, :LANES], cq_ref[...], sq_ref[...])
        qb = jnp.where(lane_q == PAD_LANE, 1.0, qb * rs * gq_ref[:, LANES:])
        q_s[r * tq:(r + 1) * tq, :LANES] = qa.astype(BF16)
        q_s[r * tq:(r + 1) * tq, LANES:] = qb.astype(BF16)

    c_q = lax.broadcasted_iota(I32, (1, nq), 1) & (tq - 1)
    t_cols = t0 + c_q
    q = q_s[...]

    g_s[...] = jax.nn.sigmoid(gate_ref[...]).T

    span = tq + WINDOW
    sw = lax.dot_general(kw_s[pl.ds(t0, span), :], q, _NT, preferred_element_type=F32)
    sc = lax.dot_general(kc_ref[0, 0], q, _NT, preferred_element_type=F32)

    cend = lax.broadcasted_iota(I32, (ncp, 1), 0) * CMP_STRIDE + (CMP_LEN - 1)
    mask_c = cend <= t_cols
    sc = jnp.where(mask_c, sc, NEG_INF)
    e = jnp.where(mask_c, jnp.exp(sc - _col_max(sc)), 0.0)
    p = e * (1.0 / jnp.maximum(_col_sum(e), 1e-30))
    o_c = jnp.dot(vct_s[...], p.astype(BF16), preferred_element_type=F32)

    ps = p[:, :tq]
    for r in range(1, R):
        ps = ps + p[:, r * tq:(r + 1) * tq]
    ps_hi = ps.astype(BF16)
    ps_lo = (ps - ps_hi.astype(F32)).astype(BF16)
    imp = (jnp.dot(ovt_ref[...], ps_hi, preferred_element_type=F32)
           + jnp.dot(ovt_ref[...], ps_lo, preferred_element_type=F32))
    jb = lax.broadcasted_iota(I32, (nb, tq), 0)
    tb = (t0 + lax.broadcasted_iota(I32, (1, tq), 1)) // SLC_LEN
    forced = (jb == 0) | (jb == tb) | (jb == tb - 1)
    score = jnp.where(forced, FORCE, jnp.where(jb <= tb, imp, -FORCE))
    sub = 8
    slabs = [score[v * sub:(v + 1) * sub] for v in range(nb // sub)]
    ranks = [jnp.zeros((sub, tq), I32) for _ in slabs]
    for j in range(nb):
        row = score[j:j + 1, :]
        for v, sl in enumerate(slabs):
            if v * sub > j:
                beats = row >= sl
            elif (v + 1) * sub - 1 < j:
                beats = row > sl
            else:
                beats = (row > sl) | ((row == sl) & (j < v * sub + lax.broadcasted_iota(I32, (sub, tq), 0)))
            ranks[v] = ranks[v] + beats.astype(I32)
    rank = jnp.concatenate(ranks, axis=0)
    unsel = (rank >= topk).astype(F32)
    unsel = jnp.concatenate([jnp.zeros((SEL_LANE0, tq), F32), unsel,
                             jnp.zeros((LANES - SEL_LANE0 - nb, tq), F32)], axis=0)
    bias = unsel.T * NEG_INF

    for r in range(R):
        qb = q_s[r * tq:(r + 1) * tq, LANES:].astype(F32)
        q_s[r * tq:(r + 1) * tq, LANES:] = (qb + bias).astype(BF16)

    i_k = lax.broadcasted_iota(I32, (tq, 1), 0)
    segs = [jnp.where(i_k > c_q, sw[:tq], NEG_INF)]
    if tq < WINDOW:
        segs.append(sw[tq:WINDOW])
    segs.append(jnp.where(i_k <= c_q, sw[WINDOW:], NEG_INF))
    m = functools.reduce(jnp.maximum, [_col_max(s) for s in segs])
    pw = jnp.concatenate([jnp.exp(s - m) for s in segs], axis=0).astype(BF16)
    acc = jnp.dot(vwt_s[:, pl.ds(t0, span)], pw, preferred_element_type=F32)
    o_w = acc[:DV] * (1.0 / acc[DV:DV + 1])

    k_at = lambda off: ks_s[pl.ds(off, tk), :]
    vt_at = lambda off: vst_s[:, pl.ds(off, tk)]
    chains = [(lambda r=r: q_s[r * tq:(r + 1) * tq], k_at, vt_at, t_cols[:, r * tq:(r + 1) * tq])
              for r in range(R)]
    o_s = jnp.concatenate(_flash_t(chains, t0 // tk, T // tk, tk, *flash_scratch), axis=1)

    gs = g_s[pl.ds(pl.multiple_of(pl.program_id(1) * GATE_PITCH, GATE_PITCH), GATE_PITCH), :]
    for r in range(R):
        cols = slice(r * tq, (r + 1) * tq)
        o = (gs[3 * r:3 * r + 1] * o_c[:, cols] + gs[3 * r + 1:3 * r + 2] * o_s[:, cols]
             + gs[3 * r + 2:3 * r + 3] * o_w[:, cols])
        o_ref[:, r * DV:(r + 1) * DV] = o.T.astype(BF16)


def _nsa_attention(proj, k_c, v_c, gq, gks, gkw, tabs, ovt, B, T, tq=512, tk=512):
    nq = T // tq
    G, R = NSA_GROUPS, NSA_HPG
    ncp = k_c.shape[2]
    c, s = tabs
    qtab = pl.BlockSpec((tq, LANES), lambda b, g, i: (i, 0))
    ktab = pl.BlockSpec((T, LANES), lambda b, g, i: (0, 0))
    gspec = pl.BlockSpec((1, HEAD_PAD), lambda b, g, i: (0, 0))
    return pl.pallas_call(
        functools.partial(_nsa_kernel, tq=tq, tk=tk, T=T, scale=NSA_DK ** -0.5),
        grid=(B, G, nq),
        in_specs=[pl.BlockSpec((tq, R * HEAD_PAD), lambda b, g, i: (b * nq + i, g)),
                  pl.BlockSpec((T, HEAD_PAD), lambda b, g, i: (b, C_KS // HEAD_PAD + g)),
                  pl.BlockSpec((T, HEAD_PAD), lambda b, g, i: (b, C_KW // HEAD_PAD + g)),
                  pl.BlockSpec((T, DV), lambda b, g, i: (b, C_VS // DV + g)),
                  pl.BlockSpec((T, DV), lambda b, g, i: (b, C_VW // DV + g)),
                  pl.BlockSpec((tq, LANES), lambda b, g, i: (b * nq + i, C_GATE // LANES)),
                  pl.BlockSpec((1, 1, ncp, HEAD_PAD), lambda b, g, i: (b, g, 0, 0)),
                  pl.BlockSpec((1, 1, ncp, DV), lambda b, g, i: (b, g, 0, 0)),
                  gspec, gspec, gspec, qtab, qtab, ktab, ktab,
                  pl.BlockSpec(ovt.shape, lambda b, g, i: (0, 0))],
        out_specs=pl.BlockSpec((tq, R * DV), lambda b, g, i: (b * nq + i, g)),
        out_shape=jax.ShapeDtypeStruct((B * T, NSA_HEADS * DV), BF16),
        scratch_shapes=[pltpu.VMEM((T, HEAD_PAD), BF16), pltpu.VMEM((WINDOW + T, HEAD_PAD), BF16),
                        pltpu.VMEM((DV_EXT, T), BF16), pltpu.VMEM((DV_EXT, WINDOW + T), BF16),
                        pltpu.VMEM((DV, ncp), BF16), pltpu.VMEM((R * tq, HEAD_PAD), BF16),
                        pltpu.VMEM((LANES, tq), F32)]
        + _flash_scratch(R, tk, tq),
        compiler_params=_params(("parallel", "parallel", "arbitrary")),
        name="nsa_attention",
    )(proj, proj, proj, proj, proj, proj, k_c, v_c, gq, gks, gkw, c, s, c, s, ovt)


def _pad_cols(w, n):
    return jnp.pad(w, ((0, 0), (0, n - w.shape[1])))


def _take_cols(w, plan):
    zeros = lambda n: jnp.zeros((w.shape[0], n), w.dtype)
    return jnp.concatenate([zeros(p) if isinstance(p, int) else w[:, p[0]:p[1]] for p in plan], axis=1)


def _nsa_head_plan(c0):
    h = NSA_ROPE // 2
    fill = HALF_LANES - h
    return [(c0, c0 + h), (c0 + 2 * h, c0 + 2 * h + fill), (c0 + h, c0 + 2 * h),
            (c0 + 2 * h + fill, c0 + NSA_DK), HEAD_PAD - NSA_DK]


def _mla_rope_plan(c0):
    h = MLA_ROPE // 2
    return [(c0, c0 + h), HALF_LANES - h, (c0 + h, c0 + 2 * h), HALF_LANES - h]


def _mla_head_plan(c0):
    return [(c0, c0 + MLA_NOPE)] + _mla_rope_plan(c0 + MLA_NOPE)


def _heads(plan_fn, c0, nh, d):
    return [piece for h in range(nh) for piece in plan_fn(c0 + h * d)]


def _nsa_head_layout(w, nh):
    return _take_cols(w, _heads(_nsa_head_plan, 0, nh, NSA_DK))


def _mla_head_layout(w, nh):
    return _take_cols(w, _heads(_mla_head_plan, 0, nh, MLA_QK))


def _layout_w_in(w_in):
    sizes = (MLA_RANK, MLA_RANK, MLA_ROPE, NSA_HEADS * NSA_DK,
             NSA_GROUPS * NSA_DK, NSA_GROUPS * NSA_DV, NSA_GROUPS * NSA_DK, NSA_GROUPS * NSA_DV,
             NSA_GROUPS * NSA_DK, NSA_GROUPS * NSA_DV, NSA_HEADS * 3, 2 * D_MODEL)
    offs, o = [], 0
    for s in sizes:
        offs.append(o)
        o += s
    cq, ckv, kr, qn, kc, vc, ks, vs, kw, vw, gn, gm = offs
    whole = lambda c0, n: [(c0, c0 + n)]
    ngate = NSA_HPG * 3
    plan = (_heads(_nsa_head_plan, qn, NSA_HEADS, NSA_DK) + whole(gm, 2 * D_MODEL)
            + _heads(_nsa_head_plan, ks, NSA_GROUPS, NSA_DK) + _heads(_nsa_head_plan, kw, NSA_GROUPS, NSA_DK)
            + whole(vs, NSA_GROUPS * NSA_DV) + whole(vw, NSA_GROUPS * NSA_DV)
            + whole(cq, MLA_RANK) + whole(ckv, MLA_RANK) + _mla_rope_plan(kr)
            + _heads(lambda c: [(c, c + ngate), GATE_PITCH - ngate], gn, NSA_GROUPS, ngate)
            + [LANES - NSA_GROUPS * GATE_PITCH]
            + _heads(lambda c: [(c, c + NSA_DK), HEAD_PAD - NSA_DK], kc, NSA_GROUPS, NSA_DK)
            + whole(vc, NSA_GROUPS * NSA_DV))
    width = sum(p if isinstance(p, int) else p[1] - p[0] for p in plan)
    assert width == C_VC + NSA_GROUPS * NSA_DV <= C_END
    return _permute_rows(w_in.T, plan + [C_END - width])


def _permute_rows_kernel(tab_ref, a_ref, b_ref, p_ref, o_ref, *, n_src):
    j = pl.program_id(0)
    row = lax.broadcasted_iota(I32, (LANES, 1), 0)
    acc = None
    for k, ref in enumerate((a_ref, b_ref)):
        in_bounds = tab_ref[j, k] * LANES + row < n_src
        x = jnp.where(in_bounds, ref[...], 0.0).astype(BF16)
        y = jnp.dot(p_ref[0, k], x, preferred_element_type=F32)
        acc = y if acc is None else acc + y
    o_ref[...] = acc.astype(BF16)


def _permute_rows(w, plan):
    n_src, cols = w.shape
    src = np.concatenate([np.full(p, -1) if isinstance(p, int) else np.arange(p[0], p[1]) for p in plan])
    nt = len(src) // LANES
    tab = np.zeros((nt, 2), np.int32)
    place = np.zeros((nt, 2, LANES, LANES), np.float32)
    for t in range(nt):
        rows = src[t * LANES:(t + 1) * LANES]
        used = rows[rows >= 0]
        if used.size == 0:
            continue
        t0, t1 = used.min() // LANES, used.max() // LANES
        assert t1 - t0 <= 1
        tab[t] = (t0, t1)
        for out_row, r in enumerate(rows):
            if r >= 0:
                place[t, r // LANES - t0, out_row, r % LANES] = 1.0
    src_tile = lambda k: pl.BlockSpec((LANES, cols), lambda j, tab: (tab[j, k], 0))
    return pl.pallas_call(
        functools.partial(_permute_rows_kernel, n_src=n_src),
        grid_spec=pltpu.PrefetchScalarGridSpec(
            num_scalar_prefetch=1, grid=(nt,),
            in_specs=[src_tile(0), src_tile(1), pl.BlockSpec((1, 2, LANES, LANES), lambda j, tab: (j, 0, 0, 0))],
            out_specs=pl.BlockSpec((LANES, cols), lambda j, tab: (j, 0))),
        out_shape=jax.ShapeDtypeStruct((nt * LANES, cols), BF16),
        compiler_params=_params(("parallel",)),
        name="weight_layout",
    )(jnp.asarray(tab), w, w, jnp.asarray(place, BF16))


def _rope_tables(pos, rot_dim):
    half = rot_dim // 2
    inv = ROPE_THETA ** (-np.arange(0, rot_dim, 2, dtype=np.float64) / rot_dim)
    ang = np.asarray(pos, np.float64)[:, None] * inv[None, :]
    c = np.ones((len(pos), LANES))
    s = np.zeros((len(pos), LANES))
    c[:, :half] = c[:, HALF_LANES:HALF_LANES + half] = np.cos(ang)
    s[:, :half] = -np.sin(ang)
    s[:, HALF_LANES:HALF_LANES + half] = np.sin(ang)
    return jnp.asarray(c, F32), jnp.asarray(s, F32)


def _overlap_table(ncp, nb):
    cs = np.arange(ncp)[None, :] * CMP_STRIDE
    ss = np.arange(nb)[:, None] * SLC_LEN
    ov = np.clip(np.minimum(cs + CMP_LEN, ss + SLC_LEN) - np.maximum(cs, ss), 0, None) / CMP_LEN
    return jnp.asarray(ov, BF16)


def _layer(x, attn_norm_g, w_in, mla_q_lat_g, mla_kv_lat_g, mla_w_uq, mla_w_uk, mla_w_uv,
           mla_q_norm_g, mla_k_norm_g, nsa_q_norm_g, nsa_k_norm_g, cmp_pe_k, cmp_pe_v,
           cmp_k_w1, cmp_k_b1, cmp_k_w2, cmp_k_b2, cmp_v_w1, cmp_v_b1, cmp_v_w2, cmp_v_b2,
           w_proj_mla, w_proj_nsa, w_out, ffn_norm_g, w_ffn_gate, w_ffn_up, w_ffn_down):
    B, T, _ = x.shape
    n = B * T
    ncp = T // CMP_STRIDE
    nb = T // SLC_LEN
    assert T % 512 == 0 and ncp % LANES == 0 and nb % 8 == 0 and SEL_LANE0 + nb <= PAD_LANE and T >= WINDOW
    x2 = x.reshape(n, D_MODEL)
    pos = np.arange(T)

    proj = _norm_matmul(x2, 0, D_MODEL, attn_norm_g.reshape(1, -1), _layout_w_in(w_in), 1024, 1024, "in_proj",
                        w_t=True)

    q_up = _norm_matmul(proj, C_CQ // MLA_RANK, MLA_RANK, mla_q_lat_g.reshape(1, -1),
                        _mla_head_layout(mla_w_uq.astype(BF16), MLA_HEADS), 1024, 2048, "mla_q_up",
                        out_dtype=BF16)
    kv_up = _norm_matmul(proj, C_CKV // MLA_RANK, MLA_RANK, mla_kv_lat_g.reshape(1, -1),
                         jnp.concatenate([mla_w_uk, mla_w_uv], axis=1).astype(BF16), 1024, 2048, "mla_kv_up",
                         out_dtype=BF16)
    o_mla = _mla_attention(q_up, kv_up, proj, _mla_head_layout(mla_q_norm_g[None], 1),
                           _mla_head_layout(mla_k_norm_g[None], 1), _rope_tables(pos, MLA_ROPE), B, T)

    cmp_tabs = _rope_tables(np.arange(ncp) * CMP_STRIDE + CMP_LEN - 1, NSA_ROPE)
    k_c = _compress(proj, C_KC, NSA_DK, B, T, cmp_pe_k, cmp_k_w1, cmp_k_b1, _nsa_head_layout(cmp_k_w2, 1),
                    _nsa_head_layout(cmp_k_b2[None], 1)[0], _nsa_head_layout(nsa_k_norm_g[0:1], 1), cmp_tabs,
                    True, "compress_k")
    v_c = _compress(proj, C_VC, NSA_DV, B, T, cmp_pe_v, cmp_v_w1, cmp_v_b1, cmp_v_w2, cmp_v_b2,
                    jnp.zeros((1, NSA_DV), F32), cmp_tabs, False, "compress_v")
    o_nsa = _nsa_attention(proj, k_c, v_c, _nsa_head_layout(nsa_q_norm_g[None], 1),
                           _nsa_head_layout(nsa_k_norm_g[1:2], 1), _nsa_head_layout(nsa_k_norm_g[2:3], 1),
                           _rope_tables(pos, NSA_ROPE), _overlap_table(ncp, nb), B, T)

    u = _merge(o_mla, o_nsa, w_proj_mla.astype(BF16), w_proj_nsa.astype(BF16), proj)
    h, hn = _out_proj(u, w_out.astype(BF16), x2, ffn_norm_g.reshape(1, -1))
    act = _ffn_up(hn, w_ffn_gate.astype(BF16), w_ffn_up.astype(BF16))
    out = _ffn_down(act, w_ffn_down.astype(BF16), h)
    return out.reshape(B, T, D_MODEL)


def kernel(x, attn_norm_g, w_in, mla_q_lat_g, mla_kv_lat_g, mla_w_uq, mla_w_uk, mla_w_uv, mla_q_norm_g, mla_k_norm_g, nsa_q_norm_g, nsa_k_norm_g, cmp_pe_k, cmp_pe_v, cmp_k_w1, cmp_k_b1, cmp_k_w2, cmp_k_b2, cmp_v_w1, cmp_v_b1, cmp_v_w2, cmp_v_b2, w_proj_mla, w_proj_nsa, w_out, ffn_norm_g, w_ffn_gate, w_ffn_up, w_ffn_down):
    params = (attn_norm_g, w_in, mla_q_lat_g, mla_kv_lat_g, mla_w_uq, mla_w_uk, mla_w_uv, mla_q_norm_g,
              mla_k_norm_g, nsa_q_norm_g, nsa_k_norm_g, cmp_pe_k, cmp_pe_v, cmp_k_w1, cmp_k_b1, cmp_k_w2,
              cmp_k_b2, cmp_v_w1, cmp_v_b1, cmp_v_w2, cmp_v_b2, w_proj_mla, w_proj_nsa, w_out, ffn_norm_g,
              w_ffn_gate, w_ffn_up, w_ffn_down)
    depth = attn_norm_g.shape[0]
    h = x
    for l in range(depth):
        h = _layer(h, *(p.reshape(p.shape[1:]) if depth == 1 else p[l] for p in params))
    return h
```

```python
import functools

import numpy as np
import jax
import jax.numpy as jnp
from jax import lax
from jax.experimental import pallas as pl
from jax.experimental.pallas import tpu as pltpu

F32, BF16, I32 = jnp.float32, jnp.bfloat16, jnp.int32

EPS = 1e-6
NEG_INF = -1e30
FORCE = 1e9
ROPE_THETA = 500000.0

D_MODEL = 2048
MLA_HEADS, MLA_NOPE, MLA_ROPE, MLA_V = 16, 128, 64, 128
MLA_QK = MLA_NOPE + MLA_ROPE
MLA_RANK = 512
NSA_HEADS, NSA_GROUPS, NSA_DK, NSA_DV = 16, 4, 192, 128
NSA_HPG = NSA_HEADS // NSA_GROUPS
NSA_ROPE = NSA_DK // 4
CMP_LEN, CMP_STRIDE = 32, 16
SLC_LEN, SLC_TOPK, WINDOW = 64, 16, 512
D_FF = 5632

LANES = 128
HALF_LANES = 64
HEAD_PAD = 256
SEL_LANE0 = 64
PAD_LANE = 127
DV = 128
DV_EXT = DV + 16
VMEM_LIMIT = 48 * 1024 * 1024

C_QN, C_GM, C_KS, C_KW, C_VS, C_VW = 0, 4096, 8192, 9216, 10240, 10752
C_CQ, C_CKV, C_KR, C_GATE, C_KC, C_VC, C_END = 11264, 11776, 12288, 12416, 12544, 13568, 14080
IN_PROJ_TN = 1280
GATE_PITCH = 16

_NT = (((1,), (1,)), ((), ()))


def _params(sem):
    return pltpu.CompilerParams(dimension_semantics=sem, vmem_limit_bytes=VMEM_LIMIT)


def _norm_matmul_kernel(x_ref, g_ref, w_ref, o_ref, xn_ref, *, w_t):
    @pl.when(pl.program_id(1) == 0)
    def _():
        x = x_ref[...]
        ms = jnp.mean(x * x, axis=-1, keepdims=True)
        xn_ref[...] = (x * lax.rsqrt(ms + EPS) * g_ref[...]).astype(BF16)

    if w_t:
        y = lax.dot_general(xn_ref[...], w_ref[...], _NT, preferred_element_type=F32)
    else:
        y = jnp.dot(xn_ref[...], w_ref[...], preferred_element_type=F32)
    o_ref[...] = y.astype(o_ref.dtype)


def _norm_matmul(x, xcol, k, g, w, tm, tn, name, w_t=False, out_dtype=F32):
    n, m = x.shape[0], w.shape[0 if w_t else 1]
    return pl.pallas_call(
        functools.partial(_norm_matmul_kernel, w_t=w_t),
        grid=(n // tm, m // tn),
        in_specs=[pl.BlockSpec((tm, k), lambda i, j: (i, xcol)),
                  pl.BlockSpec((1, k), lambda i, j: (0, 0)),
                  pl.BlockSpec((tn, k), lambda i, j: (j, 0)) if w_t else pl.BlockSpec((k, tn), lambda i, j: (0, j))],
        out_specs=pl.BlockSpec((tm, tn), lambda i, j: (i, j)),
        out_shape=jax.ShapeDtypeStruct((n, m), out_dtype),
        scratch_shapes=[pltpu.VMEM((tm, k), BF16)],
        compiler_params=_params(("parallel", "arbitrary")),
        name=name,
    )(x, g, w)


def _merge_kernel(om_ref, on_ref, wm_ref, wn_ref, g0_ref, g1_ref, o_ref):
    ym = jnp.dot(om_ref[...], wm_ref[...], preferred_element_type=F32)
    yn = jnp.dot(on_ref[...], wn_ref[...], preferred_element_type=F32)
    o_ref[...] = (jax.nn.sigmoid(g0_ref[...]) * ym + jax.nn.sigmoid(g1_ref[...]) * yn).astype(BF16)


def _merge(o_mla, o_nsa, w_pm, w_pn, proj, tm=1024, tn=512):
    n = o_mla.shape[0]
    g0, g1 = C_GM // tn, (C_GM + D_MODEL) // tn
    return pl.pallas_call(
        _merge_kernel,
        grid=(n // tm, D_MODEL // tn),
        in_specs=[pl.BlockSpec((tm, D_MODEL), lambda i, j: (i, 0)),
                  pl.BlockSpec((tm, D_MODEL), lambda i, j: (i, 0)),
                  pl.BlockSpec((D_MODEL, tn), lambda i, j: (0, j)),
                  pl.BlockSpec((D_MODEL, tn), lambda i, j: (0, j)),
                  pl.BlockSpec((tm, tn), lambda i, j: (i, g0 + j)),
                  pl.BlockSpec((tm, tn), lambda i, j: (i, g1 + j))],
        out_specs=pl.BlockSpec((tm, tn), lambda i, j: (i, j)),
        out_shape=jax.ShapeDtypeStruct((n, D_MODEL), BF16),
        compiler_params=_params(("parallel", "arbitrary")),
        name="merge",
    )(o_mla, o_nsa, w_pm, w_pn, proj, proj)


def _out_proj_kernel(u_ref, w_ref, x_ref, g_ref, h_ref, hn_ref):
    h = x_ref[...] + jnp.dot(u_ref[...], w_ref[...], preferred_element_type=F32)
    h_ref[...] = h
    ms = jnp.mean(h * h, axis=-1, keepdims=True)
    hn_ref[...] = (h * lax.rsqrt(ms + EPS) * g_ref[...]).astype(BF16)


def _out_proj(u, w_out, x, g, tm=512):
    n = u.shape[0]
    row = pl.BlockSpec((tm, D_MODEL), lambda i: (i, 0))
    return pl.pallas_call(
        _out_proj_kernel,
        grid=(n // tm,),
        in_specs=[row, pl.BlockSpec((D_MODEL, D_MODEL), lambda i: (0, 0)), row,
                  pl.BlockSpec((1, D_MODEL), lambda i: (0, 0))],
        out_specs=[row, row],
        out_shape=[jax.ShapeDtypeStruct((n, D_MODEL), F32), jax.ShapeDtypeStruct((n, D_MODEL), BF16)],
        compiler_params=_params(("parallel",)),
        name="out_proj",
    )(u, w_out, x, g)


def _ffn_up_kernel(hn_ref, wg_ref, wu_ref, o_ref):
    hn = hn_ref[...]
    a = jnp.dot(hn, wg_ref[...], preferred_element_type=F32)
    b = jnp.dot(hn, wu_ref[...], preferred_element_type=F32)
    o_ref[...] = (jax.nn.silu(a) * b).astype(BF16)


def _ffn_up(hn, w_gate, w_up, tm=1024, tn=512):
    n = hn.shape[0]
    return pl.pallas_call(
        _ffn_up_kernel,
        grid=(n // tm, D_FF // tn),
        in_specs=[pl.BlockSpec((tm, D_MODEL), lambda i, j: (i, 0)),
                  pl.BlockSpec((D_MODEL, tn), lambda i, j: (0, j)),
                  pl.BlockSpec((D_MODEL, tn), lambda i, j: (0, j))],
        out_specs=pl.BlockSpec((tm, tn), lambda i, j: (i, j)),
        out_shape=jax.ShapeDtypeStruct((n, D_FF), BF16),
        compiler_params=_params(("parallel", "arbitrary")),
        name="ffn_up",
    )(hn, w_gate, w_up)


def _ffn_down_kernel(a_ref, w_ref, h_ref, o_ref):
    o_ref[...] = h_ref[...] + jnp.dot(a_ref[...], w_ref[...], preferred_element_type=F32)


def _ffn_down(act, w_down, h, tm=1024, tn=512):
    n = act.shape[0]
    return pl.pallas_call(
        _ffn_down_kernel,
        grid=(n // tm, D_MODEL // tn),
        in_specs=[pl.BlockSpec((tm, D_FF), lambda i, j: (i, 0)),
                  pl.BlockSpec((D_FF, tn), lambda i, j: (0, j)),
                  pl.BlockSpec((tm, tn), lambda i, j: (i, j))],
        out_specs=pl.BlockSpec((tm, tn), lambda i, j: (i, j)),
        out_shape=jax.ShapeDtypeStruct((n, D_MODEL), F32),
        compiler_params=_params(("parallel", "arbitrary")),
        name="ffn_down",
    )(act, w_down, h)


def _rms_inv(xa, xb, d):
    return lax.rsqrt(jnp.sum(xa * xa + xb * xb, axis=-1, keepdims=True) * (1.0 / d) + EPS)


def _rope(x, c, s):
    return x * c + pltpu.roll(x, HALF_LANES, 1) * s


def _col_max(s, slabs=8):
    rows = s.shape[0]
    if rows % (8 * slabs) == 0:
        s = jnp.max(s.reshape(slabs, rows // slabs, s.shape[1]), axis=0)
    return jnp.max(s, axis=0, keepdims=True)


def _col_sum(s, slabs=8):
    rows = s.shape[0]
    if rows % (8 * slabs) == 0:
        s = jnp.sum(s.reshape(slabs, rows // slabs, s.shape[1]), axis=0)
    return jnp.sum(s, axis=0, keepdims=True)


def _flash_t(chains, n_last, n_max, tk, s_buf0, s_buf1, mt_buf0, mt_buf1, m_ref, acc_ref):
    nc = len(chains)
    s_buf, mt_buf = (s_buf0, s_buf1), (mt_buf0, mt_buf1)

    def stage_a(j, masked=True):
        off = j * tk
        for c, (q_at, k_at, _, t_cols) in enumerate(chains):
            s = lax.dot_general(k_at(off), q_at(), _NT, preferred_element_type=F32)
            if masked:
                pk = off + lax.broadcasted_iota(I32, (tk, 1), 0)
                s = jnp.where(pk <= t_cols, s, NEG_INF)
            s_buf[j % 2][c] = s
            mt_buf[j % 2][c] = _col_max(s)

    def stage_b(j):
        off = j * tk
        for c, (_, _, vt_at, _) in enumerate(chains):
            m = m_ref[c]
            m_new = jnp.maximum(m, mt_buf[j % 2][c])
            p = jnp.exp(s_buf[j % 2][c] - m_new).astype(BF16)
            acc_ref[c] = jnp.exp(m - m_new) * acc_ref[c] + jnp.dot(vt_at(off), p, preferred_element_type=F32)
            m_ref[c] = m_new

    m_ref[...] = jnp.full(m_ref.shape, NEG_INF, F32)
    acc_ref[...] = jnp.zeros(acc_ref.shape, F32)
    stage_a(0)
    for j in range(n_max):
        if j + 1 < n_max:
            @pl.when(j + 1 < n_last)
            def _(j=j):
                stage_a(j + 1, masked=False)
                stage_b(j)

            @pl.when(j + 1 == n_last)
            def _(j=j):
                stage_a(j + 1)
                stage_b(j)

        @pl.when(j == n_last)
        def _(j=j):
            stage_b(j)

    return [acc_ref[c, :DV] * (1.0 / acc_ref[c, DV:DV + 1]) for c in range(nc)]


def _flash_scratch(nc, tk, nq):
    return [pltpu.VMEM((nc, tk, nq), F32), pltpu.VMEM((nc, tk, nq), F32),
            pltpu.VMEM((nc, 1, nq), F32), pltpu.VMEM((nc, 1, nq), F32),
            pltpu.VMEM((nc, 1, nq), F32), pltpu.VMEM((nc, DV_EXT, nq), F32)]


def _mla_kernel(q_ref, kn_ref, kr_ref, v_ref, gq_ref, gk_ref, cq_ref, sq_ref, ck_ref, sk_ref,
                o_ref, k_s, vt_s, q_s, *flash_scratch, tq, tk, scale, hps):
    qi = pl.program_id(2)
    T = kn_ref.shape[0]

    @pl.when(qi == 0)
    def _():
        kr = kr_ref[...]
        ss_r = jnp.sum(kr * kr, axis=-1, keepdims=True)
        kr_rot = _rope(kr * gk_ref[:, LANES:], ck_ref[...], sk_ref[...])
        for h in range(hps):
            kn = kn_ref[:, h * LANES:(h + 1) * LANES].astype(F32)
            r = lax.rsqrt((jnp.sum(kn * kn, axis=-1, keepdims=True) + ss_r) * (1.0 / MLA_QK) + EPS)
            k_s[h, :, :LANES] = (kn * r * gk_ref[:, :LANES]).astype(BF16)
            k_s[h, :, LANES:] = (kr_rot * r).astype(BF16)
            vt_s[h, :DV, :] = v_ref[:, h * LANES:(h + 1) * LANES].astype(F32).T.astype(BF16)
            vt_s[h, DV:, :] = jnp.ones((DV_EXT - DV, T), BF16)

    t0 = qi * tq
    t_cols = t0 + lax.broadcasted_iota(I32, (1, tq), 1)
    chains = []
    for h in range(hps):
        qa = q_ref[:, h * HEAD_PAD:h * HEAD_PAD + LANES].astype(F32)
        qb = q_ref[:, h * HEAD_PAD + LANES:(h + 1) * HEAD_PAD].astype(F32)
        r = _rms_inv(qa, qb, MLA_QK) * scale
        qa = qa * r * gq_ref[:, :LANES]
        qb = _rope(qb * r * gq_ref[:, LANES:], cq_ref[...], sq_ref[...])
        q_s[h, :, :LANES] = qa.astype(BF16)
        q_s[h, :, LANES:] = qb.astype(BF16)
        chains.append((lambda h=h: q_s[h], lambda off, h=h: k_s[h, pl.ds(off, tk), :],
                       lambda off, h=h: vt_s[h, :, pl.ds(off, tk)], t_cols))
    for h, o_t in enumerate(_flash_t(chains, t0 // tk, T // tk, tk, *flash_scratch)):
        o_ref[:, h * LANES:(h + 1) * LANES] = o_t.T.astype(BF16)


def _mla_attention(q_up, kv_up, proj, gq, gk, tabs, B, T, tq=512, tk=512, hps=4):
    nq = T // tq
    ng = MLA_HEADS // hps
    c, s = tabs
    qtab = pl.BlockSpec((tq, LANES), lambda b, h, i: (i, 0))
    ktab = pl.BlockSpec((T, LANES), lambda b, h, i: (0, 0))
    gspec = pl.BlockSpec((1, HEAD_PAD), lambda b, h, i: (0, 0))
    return pl.pallas_call(
        functools.partial(_mla_kernel, tq=tq, tk=tk, scale=MLA_QK ** -0.5, hps=hps),
        grid=(B, ng, nq),
        in_specs=[pl.BlockSpec((tq, hps * HEAD_PAD), lambda b, h, i: (b * nq + i, h)),
                  pl.BlockSpec((T, hps * LANES), lambda b, h, i: (b, h)),
                  pl.BlockSpec((T, LANES), lambda b, h, i: (b, C_KR // LANES)),
                  pl.BlockSpec((T, hps * LANES), lambda b, h, i: (b, ng + h)),
                  gspec, gspec, qtab, qtab, ktab, ktab],
        out_specs=pl.BlockSpec((tq, hps * LANES), lambda b, h, i: (b * nq + i, h)),
        out_shape=jax.ShapeDtypeStruct((B * T, MLA_HEADS * MLA_V), BF16),
        scratch_shapes=[pltpu.VMEM((hps, T, HEAD_PAD), BF16), pltpu.VMEM((hps, DV_EXT, T), BF16),
                        pltpu.VMEM((hps, tq, HEAD_PAD), BF16)] + _flash_scratch(hps, tk, tq),
        compiler_params=_params(("parallel", "parallel", "arbitrary")),
        name="mla_attention",
    )(q_up, kv_up, proj, kv_up, gq, gk, c, s, c, s)


def _compress_kernel(*refs, is_key, gb, ncp, nt):
    x_tiles = refs[:gb * nt]
    pe_ref, w1a_ref, w1b_ref, b1_ref, w2_ref, b2_ref, g_ref, c_ref, s_ref, o_ref, xa_s, xb_s = refs[gb * nt:]
    rows = gb * ncp
    for g in range(gb):
        for l in range(CMP_STRIDE):
            for t in range(nt):
                x = x_tiles[g * nt + t][pl.ds(l, ncp, stride=CMP_STRIDE), :]
                lanes = slice((l * nt + t) * LANES, (l * nt + t + 1) * LANES)
                pe_lanes = slice(t * LANES, (t + 1) * LANES)
                xa_s[g * ncp:(g + 1) * ncp, lanes] = (x + pe_ref[l:l + 1, pe_lanes]).astype(BF16)
                xb_s[g * ncp:(g + 1) * ncp, lanes] = (
                    x + pe_ref[CMP_STRIDE + l:CMP_STRIDE + l + 1, pe_lanes]).astype(BF16)
    pa = jnp.dot(xa_s[...], w1a_ref[...], preferred_element_type=F32)
    pb = jnp.dot(xb_s[...], w1b_ref[...], preferred_element_type=F32)
    h = pa + pltpu.roll(pb, rows - 1, 0) + b1_ref[...]
    y = jnp.dot(jax.nn.gelu(h).astype(BF16), w2_ref[...], preferred_element_type=F32) + b2_ref[...]
    if is_key:
        ya, yb = y[:, :LANES], y[:, LANES:]
        r = _rms_inv(ya, yb, NSA_DK)
        c = jnp.concatenate([c_ref[...]] * gb, axis=0)
        s = jnp.concatenate([s_ref[...]] * gb, axis=0)
        ya = _rope(ya * r * g_ref[:, :LANES], c, s)
        yb = yb * r * g_ref[:, LANES:]
        y = jnp.concatenate([ya, yb], axis=1)
    o_ref[0] = y.astype(BF16).reshape(gb, ncp, y.shape[-1])


def _compress(proj, c0, d, B, T, pe, w1, b1, w2, b2, g, tabs, is_key, name, gb=2):
    G = NSA_GROUPS
    ncp = T // CMP_STRIDE
    nt = -(-d // LANES)
    dp = nt * LANES
    kd = CMP_STRIDE * dp
    dout = w2.shape[1]
    hid = w1.shape[1]
    w1 = jnp.pad(w1.astype(BF16).reshape(CMP_LEN, d, hid), ((0, 0), (0, dp - d), (0, 0))).reshape(CMP_LEN * dp, hid)
    pe = _pad_cols(pe, dp)
    c, s = tabs
    const = lambda shape: pl.BlockSpec(shape, lambda b, gi: (0,) * len(shape))
    tile = lambda k: pl.BlockSpec((T, LANES), lambda b, gi: (b, c0 // LANES + gi * gb * nt + k))
    return pl.pallas_call(
        functools.partial(_compress_kernel, is_key=is_key, gb=gb, ncp=ncp, nt=nt),
        grid=(B, G // gb),
        in_specs=[tile(k) for k in range(gb * nt)] + [
                  const((CMP_LEN, dp)), const((kd, hid)), const((kd, hid)), const((1, hid)),
                  const((hid, dout)), const((1, dout)), const((1, dout)),
                  const((ncp, LANES)), const((ncp, LANES))],
        out_specs=pl.BlockSpec((1, gb, ncp, dout), lambda b, gi: (b, gi, 0, 0)),
        out_shape=jax.ShapeDtypeStruct((B, G, ncp, dout), BF16),
        scratch_shapes=[pltpu.VMEM((gb * ncp, kd), BF16), pltpu.VMEM((gb * ncp, kd), BF16)],
        compiler_params=_params(("parallel", "parallel")),
        name=name,
    )(*([proj] * (gb * nt)), pe, w1[:kd], w1[kd:], b1.reshape(1, hid), w2.astype(BF16), b2.reshape(1, dout),
      g, c, s)


def _nsa_kernel(q_ref, ks_ref, kw_ref, vs_ref, vw_ref, gate_ref, kc_ref, vc_ref, gq_ref, gks_ref, gkw_ref,
                cq_ref, sq_ref, ck_ref, sk_ref, ovt_ref,
                o_ref, ks_s, kw_s, vst_s, vwt_s, vct_s, q_s, g_s, *flash_scratch, tq, tk, T, scale):
    qi = pl.program_id(2)
    R = NSA_HPG
    nq = R * tq
    nb = T // SLC_LEN
    topk = min(SLC_TOPK, nb)
    ncp = kc_ref.shape[2]

    @pl.when(qi == 0)
    def _():
        def prep_k(k_ref, g_ref):
            ka, kb = k_ref[:, :LANES], k_ref[:, LANES:]
            r = _rms_inv(ka, kb, NSA_DK)
            return _rope(ka * r * g_ref[:, :LANES], ck_ref[...], sk_ref[...]), kb * r * g_ref[:, LANES:]

        ka, kb = prep_k(ks_ref, gks_ref)
        blk = lax.broadcasted_iota(I32, (T, LANES), 0) // SLC_LEN
        lane = lax.broadcasted_iota(I32, (T, LANES), 1)
        ks_s[:, :LANES] = ka.astype(BF16)
        ks_s[:, LANES:] = (kb + (lane - SEL_LANE0 == blk).astype(F32)).astype(BF16)
        ka, kb = prep_k(kw_ref, gkw_ref)
        lane_w = lax.broadcasted_iota(I32, (WINDOW, LANES), 1)
        kw_s[:WINDOW, :LANES] = jnp.zeros((WINDOW, LANES), BF16)
        kw_s[:WINDOW, LANES:] = jnp.where(lane_w == PAD_LANE, NEG_INF, 0.0).astype(BF16)
        kw_s[WINDOW:, :LANES] = ka.astype(BF16)
        kw_s[WINDOW:, LANES:] = kb.astype(BF16)
        vst_s[:DV, :] = vs_ref[...].T.astype(BF16)
        vst_s[DV:, :] = jnp.ones((DV_EXT - DV, T), BF16)
        vwt_s[:DV, :WINDOW] = jnp.zeros((DV, WINDOW), BF16)
        vwt_s[:DV, WINDOW:] = vw_ref[...].T.astype(BF16)
        vwt_s[DV:, :] = jnp.ones((DV_EXT - DV, WINDOW + T), BF16)
        vct_s[...] = vc_ref[0, 0].astype(F32).T.astype(BF16)

    t0 = pl.multiple_of(qi * tq, tq)
    lane_q = lax.broadcasted_iota(I32, (tq, LANES), 1)
    for r in range(R):
        qa = q_ref[:, r * HEAD_PAD:r * HEAD_PAD + LANES]
        qb = q_ref[:, r * HEAD_PAD + LANES:(r + 1) * HEAD_PAD]
        rs = _rms_inv(qa, qb, NSA_DK) * scale
        qa = _rope(qa * rs * gq_ref[:, :LANES], cq_ref[...], sq_ref[...])
        qb = jnp.where(lane_q == PAD_LANE, 1.0, qb * rs * gq_ref[:, LANES:])
        q_s[r * tq:(r + 1) * tq, :LANES] = qa.astype(BF16)
        q_s[r * tq:(r + 1) * tq, LANES:] = qb.astype(BF16)

    c_q = lax.broadcasted_iota(I32, (1, nq), 1) & (tq - 1)
    t_cols = t0 + c_q
    q = q_s[...]

    g_s[...] = jax.nn.sigmoid(gate_ref[...]).T

    span = tq + WINDOW
    sw = lax.dot_general(kw_s[pl.ds(t0, span), :], q, _NT, preferred_element_type=F32)
    sc = lax.dot_general(kc_ref[0, 0], q, _NT, preferred_element_type=F32)

    cend = lax.broadcasted_iota(I32, (ncp, 1), 0) * CMP_STRIDE + (CMP_LEN - 1)
    mask_c = cend <= t_cols
    sc = jnp.where(mask_c, sc, NEG_INF)
    e = jnp.where(mask_c, jnp.exp(sc - _col_max(sc)), 0.0)
    p = e * (1.0 / jnp.maximum(_col_sum(e), 1e-30))
    o_c = jnp.dot(vct_s[...], p.astype(BF16), preferred_element_type=F32)

    ps = p[:, :tq]
    for r in range(1, R):
        ps = ps + p[:, r * tq:(r + 1) * tq]
    ps_hi = ps.astype(BF16)
    ps_lo = (ps - ps_hi.astype(F32)).astype(BF16)
    imp = (jnp.dot(ovt_ref[...], ps_hi, preferred_element_type=F32)
           + jnp.dot(ovt_ref[...], ps_lo, preferred_element_type=F32))
    jb = lax.broadcasted_iota(I32, (nb, tq), 0)
    tb = (t0 + lax.broadcasted_iota(I32, (1, tq), 1)) // SLC_LEN
    forced = (jb == 0) | (jb == tb) | (jb == tb - 1)
    score = jnp.where(forced, FORCE, jnp.where(jb <= tb, imp, -FORCE))
    sub = 8
    slabs = [score[v * sub:(v + 1) * sub] for v in range(nb // sub)]
    ranks = [jnp.zeros((sub, tq), I32) for _ in slabs]
    for j in range(nb):
        row = score[j:j + 1, :]
        for v, sl in enumerate(slabs):
            if v * sub > j:
                beats = row >= sl
            elif (v + 1) * sub - 1 < j:
                beats = row > sl
            else:
                beats = (row > sl) | ((row == sl) & (j < v * sub + lax.broadcasted_iota(I32, (sub, tq), 0)))
            ranks[v] = ranks[v] + beats.astype(I32)
    rank = jnp.concatenate(ranks, axis=0)
    unsel = (rank >= topk).astype(F32)
    unsel = jnp.concatenate([jnp.zeros((SEL_LANE0, tq), F32), unsel,
                             jnp.zeros((LANES - SEL_LANE0 - nb, tq), F32)], axis=0)
    bias = unsel.T * NEG_INF

    for r in range(R):
        qb = q_s[r * tq:(r + 1) * tq, LANES:].astype(F32)
        q_s[r * tq:(r + 1) * tq, LANES:] = (qb + bias).astype(BF16)

    i_k = lax.broadcasted_iota(I32, (tq, 1), 0)
    segs = [jnp.where(i_k > c_q, sw[:tq], NEG_INF)]
    if tq < WINDOW:
        segs.append(sw[tq:WINDOW])
    segs.append(jnp.where(i_k <= c_q, sw[WINDOW:], NEG_INF))
    m = functools.reduce(jnp.maximum, [_col_max(s) for s in segs])
    pw = jnp.concatenate([jnp.exp(s - m) for s in segs], axis=0).astype(BF16)
    acc = jnp.dot(vwt_s[:, pl.ds(t0, span)], pw, preferred_element_type=F32)
    o_w = acc[:DV] * (1.0 / acc[DV:DV + 1])

    k_at = lambda off: ks_s[pl.ds(off, tk), :]
    vt_at = lambda off: vst_s[:, pl.ds(off, tk)]
    chains = [(lambda r=r: q_s[r * tq:(r + 1) * tq], k_at, vt_at, t_cols[:, r * tq:(r + 1) * tq])
              for r in range(R)]
    o_s = jnp.concatenate(_flash_t(chains, t0 // tk, T // tk, tk, *flash_scratch), axis=1)

    gs = g_s[pl.ds(pl.multiple_of(pl.program_id(1) * GATE_PITCH, GATE_PITCH), GATE_PITCH), :]
    for r in range(R):
        cols = slice(r * tq, (r + 1) * tq)
        o = (gs[3 * r:3 * r + 1] * o_c[:, cols] + gs[3 * r + 1:3 * r + 2] * o_s[:, cols]
             + gs[3 * r + 2:3 * r + 3] * o_w[:, cols])
        o_ref[:, r * DV:(r + 1) * DV] = o.T.astype(BF16)


def _nsa_attention(proj, k_c, v_c, gq, gks, gkw, tabs, ovt, B, T, tq=512, tk=512):
    nq = T // tq
    G, R = NSA_GROUPS, NSA_HPG
    ncp = k_c.shape[2]
    c, s = tabs
    qtab = pl.BlockSpec((tq, LANES), lambda b, g, i: (i, 0))
    ktab = pl.BlockSpec((T, LANES), lambda b, g, i: (0, 0))
    gspec = pl.BlockSpec((1, HEAD_PAD), lambda b, g, i: (0, 0))
    return pl.pallas_call(
        functools.partial(_nsa_kernel, tq=tq, tk=tk, T=T, scale=NSA_DK ** -0.5),
        grid=(B, G, nq),
        in_specs=[pl.BlockSpec((tq, R * HEAD_PAD), lambda b, g, i: (b * nq + i, g)),
                  pl.BlockSpec((T, HEAD_PAD), lambda b, g, i: (b, C_KS // HEAD_PAD + g)),
                  pl.BlockSpec((T, HEAD_PAD), lambda b, g, i: (b, C_KW // HEAD_PAD + g)),
                  pl.BlockSpec((T, DV), lambda b, g, i: (b, C_VS // DV + g)),
                  pl.BlockSpec((T, DV), lambda b, g, i: (b, C_VW // DV + g)),
                  pl.BlockSpec((tq, LANES), lambda b, g, i: (b * nq + i, C_GATE // LANES)),
                  pl.BlockSpec((1, 1, ncp, HEAD_PAD), lambda b, g, i: (b, g, 0, 0)),
                  pl.BlockSpec((1, 1, ncp, DV), lambda b, g, i: (b, g, 0, 0)),
                  gspec, gspec, gspec, qtab, qtab, ktab, ktab,
                  pl.BlockSpec(ovt.shape, lambda b, g, i: (0, 0))],
        out_specs=pl.BlockSpec((tq, R * DV), lambda b, g, i: (b * nq + i, g)),
        out_shape=jax.ShapeDtypeStruct((B * T, NSA_HEADS * DV), BF16),
        scratch_shapes=[pltpu.VMEM((T, HEAD_PAD), BF16), pltpu.VMEM((WINDOW + T, HEAD_PAD), BF16),
                        pltpu.VMEM((DV_EXT, T), BF16), pltpu.VMEM((DV_EXT, WINDOW + T), BF16),
                        pltpu.VMEM((DV, ncp), BF16), pltpu.VMEM((R * tq, HEAD_PAD), BF16),
                        pltpu.VMEM((LANES, tq), F32)]
        + _flash_scratch(R, tk, tq),
        compiler_params=_params(("parallel", "parallel", "arbitrary")),
        name="nsa_attention",
    )(proj, proj, proj, proj, proj, proj, k_c, v_c, gq, gks, gkw, c, s, c, s, ovt)


def _pad_cols(w, n):
    return jnp.pad(w, ((0, 0), (0, n - w.shape[1])))


def _take_cols(w, plan):
    zeros = lambda n: jnp.zeros((w.shape[0], n), w.dtype)
    return jnp.concatenate([zeros(p) if isinstance(p, int) else w[:, p[0]:p[1]] for p in plan], axis=1)


def _nsa_head_plan(c0):
    h = NSA_ROPE // 2
    fill = HALF_LANES - h
    return [(c0, c0 + h), (c0 + 2 * h, c0 + 2 * h + fill), (c0 + h, c0 + 2 * h),
            (c0 + 2 * h + fill, c0 + NSA_DK), HEAD_PAD - NSA_DK]


def _mla_rope_plan(c0):
    h = MLA_ROPE // 2
    return [(c0, c0 + h), HALF_LANES - h, (c0 + h, c0 + 2 * h), HALF_LANES - h]


def _mla_head_plan(c0):
    return [(c0, c0 + MLA_NOPE)] + _mla_rope_plan(c0 + MLA_NOPE)


def _heads(plan_fn, c0, nh, d):
    return [piece for h in range(nh) for piece in plan_fn(c0 + h * d)]


def _nsa_head_layout(w, nh):
    return _take_cols(w, _heads(_nsa_head_plan, 0, nh, NSA_DK))


def _mla_head_layout(w, nh):
    return _take_cols(w, _heads(_mla_head_plan, 0, nh, MLA_QK))


def _layout_w_in(w_in):
    sizes = (MLA_RANK, MLA_RANK, MLA_ROPE, NSA_HEADS * NSA_DK,
             NSA_GROUPS * NSA_DK, NSA_GROUPS * NSA_DV, NSA_GROUPS * NSA_DK, NSA_GROUPS * NSA_DV,
             NSA_GROUPS * NSA_DK, NSA_GROUPS * NSA_DV, NSA_HEADS * 3, 2 * D_MODEL)
    offs, o = [], 0
    for s in sizes:
        offs.append(o)
        o += s
    cq, ckv, kr, qn, kc, vc, ks, vs, kw, vw, gn, gm = offs
    whole = lambda c0, n: [(c0, c0 + n)]
    ngate = NSA_HPG * 3
    plan = (_heads(_nsa_head_plan, qn, NSA_HEADS, NSA_DK) + whole(gm, 2 * D_MODEL)
            + _heads(_nsa_head_plan, ks, NSA_GROUPS, NSA_DK) + _heads(_nsa_head_plan, kw, NSA_GROUPS, NSA_DK)
            + whole(vs, NSA_GROUPS * NSA_DV) + whole(vw, NSA_GROUPS * NSA_DV)
            + whole(cq, MLA_RANK) + whole(ckv, MLA_RANK) + _mla_rope_plan(kr)
            + _heads(lambda c: [(c, c + ngate), GATE_PITCH - ngate], gn, NSA_GROUPS, ngate)
            + [LANES - NSA_GROUPS * GATE_PITCH]
            + _heads(lambda c: [(c, c + NSA_DK), HEAD_PAD - NSA_DK], kc, NSA_GROUPS, NSA_DK)
            + whole(vc, NSA_GROUPS * NSA_DV))
    width = sum(p if isinstance(p, int) else p[1] - p[0] for p in plan)
    assert width == C_VC + NSA_GROUPS * NSA_DV <= C_END
    return _permute_rows(w_in.T, plan + [C_END - width])


def _permute_rows_kernel(tab_ref, a_ref, b_ref, p_ref, o_ref, *, n_src):
    j = pl.program_id(0)
    row = lax.broadcasted_iota(I32, (LANES, 1), 0)
    acc = None
    for k, ref in enumerate((a_ref, b_ref)):
        in_bounds = tab_ref[j, k] * LANES + row < n_src
        x = jnp.where(in_bounds, ref[...], 0.0).astype(BF16)
        y = jnp.dot(p_ref[0, k], x, preferred_element_type=F32)
        acc = y if acc is None else acc + y
    o_ref[...] = acc.astype(BF16)


def _permute_rows(w, plan):
    n_src, cols = w.shape
    src = np.concatenate([np.full(p, -1) if isinstance(p, int) else np.arange(p[0], p[1]) for p in plan])
    nt = len(src) // LANES
    tab = np.zeros((nt, 2), np.int32)
    place = np.zeros((nt, 2, LANES, LANES), np.float32)
    for t in range(nt):
        rows = src[t * LANES:(t + 1) * LANES]
        used = rows[rows >= 0]
        if used.size == 0:
            continue
        t0, t1 = used.min() // LANES, used.max() // LANES
        assert t1 - t0 <= 1
        tab[t] = (t0, t1)
        for out_row, r in enumerate(rows):
            if r >= 0:
                place[t, r // LANES - t0, out_row, r % LANES] = 1.0
    src_tile = lambda k: pl.BlockSpec((LANES, cols), lambda j, tab: (tab[j, k], 0))
    return pl.pallas_call(
        functools.partial(_permute_rows_kernel, n_src=n_src),
        grid_spec=pltpu.PrefetchScalarGridSpec(
            num_scalar_prefetch=1, grid=(nt,),
            in_specs=[src_tile(0), src_tile(1), pl.BlockSpec((1, 2, LANES, LANES), lambda j, tab: (j, 0, 0, 0))],
            out_specs=pl.BlockSpec((LANES, cols), lambda j, tab: (j, 0))),
        out_shape=jax.ShapeDtypeStruct((nt * LANES, cols), BF16),
        compiler_params=_params(("parallel",)),
        name="weight_layout",
    )(jnp.asarray(tab), w, w, jnp.asarray(place, BF16))


def _rope_tables(pos, rot_dim):
    half = rot_dim // 2
    inv = ROPE_THETA ** (-np.arange(0, rot_dim, 2, dtype=np.float64) / rot_dim)
    ang = np.asarray(pos, np.float64)[:, None] * inv[None, :]
    c = np.ones((len(pos), LANES))
    s = np.zeros((len(pos), LANES))
    c[:, :half] = c[:, HALF_LANES:HALF_LANES + half] = np.cos(ang)
    s[:, :half] = -np.sin(ang)
    s[:, HALF_LANES:HALF_LANES + half] = np.sin(ang)
    return jnp.asarray(c, F32), jnp.asarray(s, F32)


def _overlap_table(ncp, nb):
    cs = np.arange(ncp)[None, :] * CMP_STRIDE
    ss = np.arange(nb)[:, None] * SLC_LEN
    ov = np.clip(np.minimum(cs + CMP_LEN, ss + SLC_LEN) - np.maximum(cs, ss), 0, None) / CMP_LEN
    return jnp.asarray(ov, BF16)


def _layer(x, attn_norm_g, w_in, mla_q_lat_g, mla_kv_lat_g, mla_w_uq, mla_w_uk, mla_w_uv,
           mla_q_norm_g, mla_k_norm_g, nsa_q_norm_g, nsa_k_norm_g, cmp_pe_k, cmp_pe_v,
           cmp_k_w1, cmp_k_b1, cmp_k_w2, cmp_k_b2, cmp_v_w1, cmp_v_b1, cmp_v_w2, cmp_v_b2,
           w_proj_mla, w_proj_nsa, w_out, ffn_norm_g, w_ffn_gate, w_ffn_up, w_ffn_down):
    B, T, _ = x.shape
    n = B * T
    ncp = T // CMP_STRIDE
    nb = T // SLC_LEN
    assert T % 512 == 0 and ncp % LANES == 0 and nb % 8 == 0 and SEL_LANE0 + nb <= PAD_LANE and T >= WINDOW
    x2 = x.reshape(n, D_MODEL)
    pos = np.arange(T)

    proj = _norm_matmul(x2, 0, D_MODEL, attn_norm_g.reshape(1, -1), _layout_w_in(w_in), 1024, IN_PROJ_TN, "in_proj",
                        w_t=True)

    q_up = _norm_matmul(proj, C_CQ // MLA_RANK, MLA_RANK, mla_q_lat_g.reshape(1, -1),
                        _mla_head_layout(mla_w_uq.astype(BF16), MLA_HEADS), 1024, 4096, "mla_q_up",
                        out_dtype=BF16)
    kv_up = _norm_matmul(proj, C_CKV // MLA_RANK, MLA_RANK, mla_kv_lat_g.reshape(1, -1),
                         jnp.concatenate([mla_w_uk, mla_w_uv], axis=1).astype(BF16), 1024, 4096, "mla_kv_up",
                         out_dtype=BF16)
    o_mla = _mla_attention(q_up, kv_up, proj, _mla_head_layout(mla_q_norm_g[None], 1),
                           _mla_head_layout(mla_k_norm_g[None], 1), _rope_tables(pos, MLA_ROPE), B, T)

    cmp_tabs = _rope_tables(np.arange(ncp) * CMP_STRIDE + CMP_LEN - 1, NSA_ROPE)
    k_c = _compress(proj, C_KC, NSA_DK, B, T, cmp_pe_k, cmp_k_w1, cmp_k_b1, _nsa_head_layout(cmp_k_w2, 1),
                    _nsa_head_layout(cmp_k_b2[None], 1)[0], _nsa_head_layout(nsa_k_norm_g[0:1], 1), cmp_tabs,
                    True, "compress_k")
    v_c = _compress(proj, C_VC, NSA_DV, B, T, cmp_pe_v, cmp_v_w1, cmp_v_b1, cmp_v_w2, cmp_v_b2,
                    jnp.zeros((1, NSA_DV), F32), cmp_tabs, False, "compress_v")
    o_nsa = _nsa_attention(proj, k_c, v_c, _nsa_head_layout(nsa_q_norm_g[None], 1),
                           _nsa_head_layout(nsa_k_norm_g[1:2], 1), _nsa_head_layout(nsa_k_norm_g[2:3], 1),
                           _rope_tables(pos, NSA_ROPE), _overlap_table(ncp, nb), B, T)

    u = _merge(o_mla, o_nsa, w_proj_mla.astype(BF16), w_proj_nsa.astype(BF16), proj)
    h, hn = _out_proj(u, w_out.astype(BF16), x2, ffn_norm_g.reshape(1, -1))
    act = _ffn_up(hn, w_ffn_gate.astype(BF16), w_ffn_up.astype(BF16))
    out = _ffn_down(act, w_ffn_down.astype(BF16), h)
    return out.reshape(B, T, D_MODEL)


def kernel(x, attn_norm_g, w_in, mla_q_lat_g, mla_kv_lat_g, mla_w_uq, mla_w_uk, mla_w_uv, mla_q_norm_g, mla_k_norm_g, nsa_q_norm_g, nsa_k_norm_g, cmp_pe_k, cmp_pe_v, cmp_k_w1, cmp_k_b1, cmp_k_w2, cmp_k_b2, cmp_v_w1, cmp_v_b1, cmp_v_w2, cmp_v_b2, w_proj_mla, w_proj_nsa, w_out, ffn_norm_g, w_ffn_gate, w_ffn_up, w_ffn_down):
    params = (attn_norm_g, w_in, mla_q_lat_g, mla_kv_lat_g, mla_w_uq, mla_w_uk, mla_w_uv, mla_q_norm_g,
              mla_k_norm_g, nsa_q_norm_g, nsa_k_norm_g, cmp_pe_k, cmp_pe_v, cmp_k_w1, cmp_k_b1, cmp_k_w2,
              cmp_k_b2, cmp_v_w1, cmp_v_b1, cmp_v_w2, cmp_v_b2, w_proj_mla, w_proj_nsa, w_out, ffn_norm_g,
              w_ffn_gate, w_ffn_up, w_ffn_down)
    depth = attn_norm_g.shape[0]
    h = x
    for l in range(depth):
        h = _layer(h, *(p.reshape(p.shape[1:]) if depth == 1 else p[l] for p in params))
    return h
```

```python
import functools

import numpy as np
import jax
import jax.numpy as jnp
from jax import lax
from jax.experimental import pallas as pl
from jax.experimental.pallas import tpu as pltpu

F32, BF16, I32 = jnp.float32, jnp.bfloat16, jnp.int32

EPS = 1e-6
NEG_INF = -1e30
FORCE = 1e9
ROPE_THETA = 500000.0

D_MODEL = 2048
MLA_HEADS, MLA_NOPE, MLA_ROPE, MLA_V = 16, 128, 64, 128
MLA_QK = MLA_NOPE + MLA_ROPE
MLA_RANK = 512
NSA_HEADS, NSA_GROUPS, NSA_DK, NSA_DV = 16, 4, 192, 128
NSA_HPG = NSA_HEADS // NSA_GROUPS
NSA_ROPE = NSA_DK // 4
CMP_LEN, CMP_STRIDE = 32, 16
SLC_LEN, SLC_TOPK, WINDOW = 64, 16, 512
D_FF = 5632

LANES = 128
HALF_LANES = 64
HEAD_PAD = 256
SEL_LANE0 = 64
PAD_LANE = 127
DV = 128
DV_EXT = DV + 16
VMEM_LIMIT = 48 * 1024 * 1024

C_QN, C_GM, C_KS, C_KW, C_VS, C_VW = 0, 4096, 8192, 9216, 10240, 10752
C_CQ, C_CKV, C_KR, C_GATE, C_KC, C_VC, C_END = 11264, 11776, 12288, 12416, 12544, 13568, 14080
IN_PROJ_TN = 1280
GATE_PITCH = 16

_NT = (((1,), (1,)), ((), ()))


def _params(sem):
    return pltpu.CompilerParams(dimension_semantics=sem, vmem_limit_bytes=VMEM_LIMIT)


def _norm_matmul_kernel(x_ref, g_ref, w_ref, o_ref, xn_ref, *, w_t):
    @pl.when(pl.program_id(1) == 0)
    def _():
        x = x_ref[...]
        ms = jnp.mean(x * x, axis=-1, keepdims=True)
        xn_ref[...] = (x * lax.rsqrt(ms + EPS) * g_ref[...]).astype(BF16)

    if w_t:
        y = lax.dot_general(xn_ref[...], w_ref[...], _NT, preferred_element_type=F32)
    else:
        y = jnp.dot(xn_ref[...], w_ref[...], preferred_element_type=F32)
    o_ref[...] = y.astype(o_ref.dtype)


def _norm_matmul(x, xcol, k, g, w, tm, tn, name, w_t=False, out_dtype=F32):
    n, m = x.shape[0], w.shape[0 if w_t else 1]
    return pl.pallas_call(
        functools.partial(_norm_matmul_kernel, w_t=w_t),
        grid=(n // tm, m // tn),
        in_specs=[pl.BlockSpec((tm, k), lambda i, j: (i, xcol)),
                  pl.BlockSpec((1, k), lambda i, j: (0, 0)),
                  pl.BlockSpec((tn, k), lambda i, j: (j, 0)) if w_t else pl.BlockSpec((k, tn), lambda i, j: (0, j))],
        out_specs=pl.BlockSpec((tm, tn), lambda i, j: (i, j)),
        out_shape=jax.ShapeDtypeStruct((n, m), out_dtype),
        scratch_shapes=[pltpu.VMEM((tm, k), BF16)],
        compiler_params=_params(("parallel", "arbitrary")),
        name=name,
    )(x, g, w)


def _merge_kernel(om_ref, on_ref, wm_ref, wn_ref, g0_ref, g1_ref, o_ref):
    ym = jnp.dot(om_ref[...], wm_ref[...], preferred_element_type=F32)
    yn = jnp.dot(on_ref[...], wn_ref[...], preferred_element_type=F32)
    o_ref[...] = (jax.nn.sigmoid(g0_ref[...]) * ym + jax.nn.sigmoid(g1_ref[...]) * yn).astype(BF16)


def _merge(o_mla, o_nsa, w_pm, w_pn, proj, tm=1024, tn=512):
    n = o_mla.shape[0]
    g0, g1 = C_GM // tn, (C_GM + D_MODEL) // tn
    return pl.pallas_call(
        _merge_kernel,
        grid=(n // tm, D_MODEL // tn),
        in_specs=[pl.BlockSpec((tm, D_MODEL), lambda i, j: (i, 0)),
                  pl.BlockSpec((tm, D_MODEL), lambda i, j: (i, 0)),
                  pl.BlockSpec((D_MODEL, tn), lambda i, j: (0, j)),
                  pl.BlockSpec((D_MODEL, tn), lambda i, j: (0, j)),
                  pl.BlockSpec((tm, tn), lambda i, j: (i, g0 + j)),
                  pl.BlockSpec((tm, tn), lambda i, j: (i, g1 + j))],
        out_specs=pl.BlockSpec((tm, tn), lambda i, j: (i, j)),
        out_shape=jax.ShapeDtypeStruct((n, D_MODEL), BF16),
        compiler_params=_params(("parallel", "arbitrary")),
        name="merge",
    )(o_mla, o_nsa, w_pm, w_pn, proj, proj)


def _out_proj_kernel(u_ref, w_ref, x_ref, g_ref, h_ref, hn_ref):
    h = x_ref[...] + jnp.dot(u_ref[...], w_ref[...], preferred_element_type=F32)
    h_ref[...] = h
    ms = jnp.mean(h * h, axis=-1, keepdims=True)
    hn_ref[...] = (h * lax.rsqrt(ms + EPS) * g_ref[...]).astype(BF16)


def _out_proj(u, w_out, x, g, tm=512):
    n = u.shape[0]
    row = pl.BlockSpec((tm, D_MODEL), lambda i: (i, 0))
    return pl.pallas_call(
        _out_proj_kernel,
        grid=(n // tm,),
        in_specs=[row, pl.BlockSpec((D_MODEL, D_MODEL), lambda i: (0, 0)), row,
                  pl.BlockSpec((1, D_MODEL), lambda i: (0, 0))],
        out_specs=[row, row],
        out_shape=[jax.ShapeDtypeStruct((n, D_MODEL), F32), jax.ShapeDtypeStruct((n, D_MODEL), BF16)],
        compiler_params=_params(("parallel",)),
        name="out_proj",
    )(u, w_out, x, g)


def _ffn_up_kernel(hn_ref, wg_ref, wu_ref, o_ref):
    hn = hn_ref[...]
    a = jnp.dot(hn, wg_ref[...], preferred_element_type=F32)
    b = jnp.dot(hn, wu_ref[...], preferred_element_type=F32)
    o_ref[...] = (jax.nn.silu(a) * b).astype(BF16)


def _ffn_up(hn, w_gate, w_up, tm=1024, tn=512):
    n = hn.shape[0]
    return pl.pallas_call(
        _ffn_up_kernel,
        grid=(n // tm, D_FF // tn),
        in_specs=[pl.BlockSpec((tm, D_MODEL), lambda i, j: (i, 0)),
                  pl.BlockSpec((D_MODEL, tn), lambda i, j: (0, j)),
                  pl.BlockSpec((D_MODEL, tn), lambda i, j: (0, j))],
        out_specs=pl.BlockSpec((tm, tn), lambda i, j: (i, j)),
        out_shape=jax.ShapeDtypeStruct((n, D_FF), BF16),
        compiler_params=_params(("parallel", "arbitrary")),
        name="ffn_up",
    )(hn, w_gate, w_up)


def _ffn_down_kernel(a_ref, w_ref, h_ref, o_ref):
    o_ref[...] = h_ref[...] + jnp.dot(a_ref[...], w_ref[...], preferred_element_type=F32)


def _ffn_down(act, w_down, h, tm=1024, tn=512):
    n = act.shape[0]
    return pl.pallas_call(
        _ffn_down_kernel,
        grid=(n // tm, D_MODEL // tn),
        in_specs=[pl.BlockSpec((tm, D_FF), lambda i, j: (i, 0)),
                  pl.BlockSpec((D_FF, tn), lambda i, j: (0, j)),
                  pl.BlockSpec((tm, tn), lambda i, j: (i, j))],
        out_specs=pl.BlockSpec((tm, tn), lambda i, j: (i, j)),
        out_shape=jax.ShapeDtypeStruct((n, D_MODEL), F32),
        compiler_params=_params(("parallel", "arbitrary")),
        name="ffn_down",
    )(act, w_down, h)


def _rms_inv(xa, xb, d):
    return lax.rsqrt(jnp.sum(xa * xa + xb * xb, axis=-1, keepdims=True) * (1.0 / d) + EPS)


def _rope(x, c, s):
    return x * c + pltpu.roll(x, HALF_LANES, 1) * s


def _col_max(s, slabs=8):
    rows = s.shape[0]
    if rows % (8 * slabs) == 0:
        s = jnp.max(s.reshape(slabs, rows // slabs, s.shape[1]), axis=0)
    return jnp.max(s, axis=0, keepdims=True)


def _col_sum(s, slabs=8):
    rows = s.shape[0]
    if rows % (8 * slabs) == 0:
        s = jnp.sum(s.reshape(slabs, rows // slabs, s.shape[1]), axis=0)
    return jnp.sum(s, axis=0, keepdims=True)


def _flash_t(chains, n_last, n_max, tk, s_buf0, s_buf1, mt_buf0, mt_buf1, m_ref, acc_ref):
    nc = len(chains)
    s_buf, mt_buf = (s_buf0, s_buf1), (mt_buf0, mt_buf1)

    def stage_a(j, masked=True):
        off = j * tk
        for c, (q_at, k_at, _, t_cols) in enumerate(chains):
            s = lax.dot_general(k_at(off), q_at(), _NT, preferred_element_type=F32)
            if masked:
                pk = off + lax.broadcasted_iota(I32, (tk, 1), 0)
                s = jnp.where(pk <= t_cols, s, NEG_INF)
            s_buf[j % 2][c] = s
            mt_buf[j % 2][c] = _col_max(s)

    def stage_b(j):
        off = j * tk
        for c, (_, _, vt_at, _) in enumerate(chains):
            m = m_ref[c]
            m_new = jnp.maximum(m, mt_buf[j % 2][c])
            p = jnp.exp(s_buf[j % 2][c] - m_new).astype(BF16)
            acc_ref[c] = jnp.exp(m - m_new) * acc_ref[c] + jnp.dot(vt_at(off), p, preferred_element_type=F32)
            m_ref[c] = m_new

    m_ref[...] = jnp.full(m_ref.shape, NEG_INF, F32)
    acc_ref[...] = jnp.zeros(acc_ref.shape, F32)
    stage_a(0)
    for j in range(n_max):
        if j + 1 < n_max:
            @pl.when(j + 1 < n_last)
            def _(j=j):
                stage_a(j + 1, masked=False)
                stage_b(j)

            @pl.when(j + 1 == n_last)
            def _(j=j):
                stage_a(j + 1)
                stage_b(j)

        @pl.when(j == n_last)
        def _(j=j):
            stage_b(j)

    return [acc_ref[c, :DV] * (1.0 / acc_ref[c, DV:DV + 1]) for c in range(nc)]


def _flash_scratch(nc, tk, nq):
    return [pltpu.VMEM((nc, tk, nq), F32), pltpu.VMEM((nc, tk, nq), F32),
            pltpu.VMEM((nc, 1, nq), F32), pltpu.VMEM((nc, 1, nq), F32),
            pltpu.VMEM((nc, 1, nq), F32), pltpu.VMEM((nc, DV_EXT, nq), F32)]


def _mla_kernel(q_ref, kn_ref, kr_ref, v_ref, gq_ref, gk_ref, cq_ref, sq_ref, ck_ref, sk_ref,
                o_ref, k_s, vt_s, q_s, *flash_scratch, tq, tk, scale, hps):
    qi = pl.program_id(2)
    T = kn_ref.shape[0]

    @pl.when(qi == 0)
    def _():
        kr = kr_ref[...]
        ss_r = jnp.sum(kr * kr, axis=-1, keepdims=True)
        kr_rot = _rope(kr * gk_ref[:, LANES:], ck_ref[...], sk_ref[...])
        for h in range(hps):
            kn = kn_ref[:, h * LANES:(h + 1) * LANES].astype(F32)
            r = lax.rsqrt((jnp.sum(kn * kn, axis=-1, keepdims=True) + ss_r) * (1.0 / MLA_QK) + EPS)
            k_s[h, :, :LANES] = (kn * r * gk_ref[:, :LANES]).astype(BF16)
            k_s[h, :, LANES:] = (kr_rot * r).astype(BF16)
            vt_s[h, :DV, :] = v_ref[:, h * LANES:(h + 1) * LANES].astype(F32).T.astype(BF16)
            vt_s[h, DV:, :] = jnp.ones((DV_EXT - DV, T), BF16)

    t0 = qi * tq
    t_cols = t0 + lax.broadcasted_iota(I32, (1, tq), 1)
    chains = []
    for h in range(hps):
        qa = q_ref[:, h * HEAD_PAD:h * HEAD_PAD + LANES].astype(F32)
        qb = q_ref[:, h * HEAD_PAD + LANES:(h + 1) * HEAD_PAD].astype(F32)
        r = _rms_inv(qa, qb, MLA_QK) * scale
        qa = qa * r * gq_ref[:, :LANES]
        qb = _rope(qb * r * gq_ref[:, LANES:], cq_ref[...], sq_ref[...])
        q_s[h, :, :LANES] = qa.astype(BF16)
        q_s[h, :, LANES:] = qb.astype(BF16)
        chains.append((lambda h=h: q_s[h], lambda off, h=h: k_s[h, pl.ds(off, tk), :],
                       lambda off, h=h: vt_s[h, :, pl.ds(off, tk)], t_cols))
    for h, o_t in enumerate(_flash_t(chains, t0 // tk, T // tk, tk, *flash_scratch)):
        o_ref[:, h * LANES:(h + 1) * LANES] = o_t.T.astype(BF16)


def _mla_attention(q_up, kv_up, proj, gq, gk, tabs, B, T, tq=512, tk=512, hps=4):
    nq = T // tq
    ng = MLA_HEADS // hps
    c, s = tabs
    qtab = pl.BlockSpec((tq, LANES), lambda b, h, i: (i, 0))
    ktab = pl.BlockSpec((T, LANES), lambda b, h, i: (0, 0))
    gspec = pl.BlockSpec((1, HEAD_PAD), lambda b, h, i: (0, 0))
    return pl.pallas_call(
        functools.partial(_mla_kernel, tq=tq, tk=tk, scale=MLA_QK ** -0.5, hps=hps),
        grid=(B, ng, nq),
        in_specs=[pl.BlockSpec((tq, hps * HEAD_PAD), lambda b, h, i: (b * nq + i, h)),
                  pl.BlockSpec((T, hps * LANES), lambda b, h, i: (b, h)),
                  pl.BlockSpec((T, LANES), lambda b, h, i: (b, C_KR // LANES)),
                  pl.BlockSpec((T, hps * LANES), lambda b, h, i: (b, ng + h)),
                  gspec, gspec, qtab, qtab, ktab, ktab],
        out_specs=pl.BlockSpec((tq, hps * LANES), lambda b, h, i: (b * nq + i, h)),
        out_shape=jax.ShapeDtypeStruct((B * T, MLA_HEADS * MLA_V), BF16),
        scratch_shapes=[pltpu.VMEM((hps, T, HEAD_PAD), BF16), pltpu.VMEM((hps, DV_EXT, T), BF16),
                        pltpu.VMEM((hps, tq, HEAD_PAD), BF16)] + _flash_scratch(hps, tk, tq),
        compiler_params=_params(("parallel", "parallel", "arbitrary")),
        name="mla_attention",
    )(q_up, kv_up, proj, kv_up, gq, gk, c, s, c, s)


def _compress_kernel(*refs, is_key, gb, ncp, nt):
    x_tiles = refs[:gb * nt]
    pe_ref, w1a_ref, w1b_ref, b1_ref, w2_ref, b2_ref, g_ref, c_ref, s_ref, o_ref, xa_s, xb_s = refs[gb * nt:]
    rows = gb * ncp
    for g in range(gb):
        for l in range(CMP_STRIDE):
            for t in range(nt):
                x = x_tiles[g * nt + t][pl.ds(l, ncp, stride=CMP_STRIDE), :]
                lanes = slice((l * nt + t) * LANES, (l * nt + t + 1) * LANES)
                pe_lanes = slice(t * LANES, (t + 1) * LANES)
                xa_s[g * ncp:(g + 1) * ncp, lanes] = (x + pe_ref[l:l + 1, pe_lanes]).astype(BF16)
                xb_s[g * ncp:(g + 1) * ncp, lanes] = (
                    x + pe_ref[CMP_STRIDE + l:CMP_STRIDE + l + 1, pe_lanes]).astype(BF16)
    pa = jnp.dot(xa_s[...], w1a_ref[...], preferred_element_type=F32)
    pb = jnp.dot(xb_s[...], w1b_ref[...], preferred_element_type=F32)
    h = pa + pltpu.roll(pb, rows - 1, 0) + b1_ref[...]
    y = jnp.dot(jax.nn.gelu(h).astype(BF16), w2_ref[...], preferred_element_type=F32) + b2_ref[...]
    if is_key:
        ya, yb = y[:, :LANES], y[:, LANES:]
        r = _rms_inv(ya, yb, NSA_DK)
        c = jnp.concatenate([c_ref[...]] * gb, axis=0)
        s = jnp.concatenate([s_ref[...]] * gb, axis=0)
        ya = _rope(ya * r * g_ref[:, :LANES], c, s)
        yb = yb * r * g_ref[:, LANES:]
        y = jnp.concatenate([ya, yb], axis=1)
    o_ref[0] = y.astype(BF16).reshape(gb, ncp, y.shape[-1])


def _compress(proj, c0, d, B, T, pe, w1, b1, w2, b2, g, tabs, is_key, name, gb=2):
    G = NSA_GROUPS
    ncp = T // CMP_STRIDE
    nt = -(-d // LANES)
    dp = nt * LANES
    kd = CMP_STRIDE * dp
    dout = w2.shape[1]
    hid = w1.shape[1]
    w1 = jnp.pad(w1.astype(BF16).reshape(CMP_LEN, d, hid), ((0, 0), (0, dp - d), (0, 0))).reshape(CMP_LEN * dp, hid)
    pe = _pad_cols(pe, dp)
    c, s = tabs
    const = lambda shape: pl.BlockSpec(shape, lambda b, gi: (0,) * len(shape))
    tile = lambda k: pl.BlockSpec((T, LANES), lambda b, gi: (b, c0 // LANES + gi * gb * nt + k))
    return pl.pallas_call(
        functools.partial(_compress_kernel, is_key=is_key, gb=gb, ncp=ncp, nt=nt),
        grid=(B, G // gb),
        in_specs=[tile(k) for k in range(gb * nt)] + [
                  const((CMP_LEN, dp)), const((kd, hid)), const((kd, hid)), const((1, hid)),
                  const((hid, dout)), const((1, dout)), const((1, dout)),
                  const((ncp, LANES)), const((ncp, LANES))],
        out_specs=pl.BlockSpec((1, gb, ncp, dout), lambda b, gi: (b, gi, 0, 0)),
        out_shape=jax.ShapeDtypeStruct((B, G, ncp, dout), BF16),
        scratch_shapes=[pltpu.VMEM((gb * ncp, kd), BF16), pltpu.VMEM((gb * ncp, kd), BF16)],
        compiler_params=_params(("parallel", "parallel")),
        name=name,
    )(*([proj] * (gb * nt)), pe, w1[:kd], w1[kd:], b1.reshape(1, hid), w2.astype(BF16), b2.reshape(1, dout),
      g, c, s)


def _nsa_kernel(q_ref, ks_ref, kw_ref, vs_ref, vw_ref, gate_ref, kc_ref, vc_ref, gq_ref, gks_ref, gkw_ref,
                cq_ref, sq_ref, ck_ref, sk_ref, ovt_ref,
                o_ref, ks_s, kw_s, vst_s, vwt_s, vct_s, q_s, g_s, *flash_scratch, tq, tk, T, scale):
    qi = pl.program_id(2)
    R = NSA_HPG
    nq = R * tq
    nb = T // SLC_LEN
    topk = min(SLC_TOPK, nb)
    ncp = kc_ref.shape[2]

    @pl.when(qi == 0)
    def _():
        def prep_k(k_ref, g_ref):
            ka, kb = k_ref[:, :LANES], k_ref[:, LANES:]
            r = _rms_inv(ka, kb, NSA_DK)
            return _rope(ka * r * g_ref[:, :LANES], ck_ref[...], sk_ref[...]), kb * r * g_ref[:, LANES:]

        ka, kb = prep_k(ks_ref, gks_ref)
        blk = lax.broadcasted_iota(I32, (T, LANES), 0) // SLC_LEN
        lane = lax.broadcasted_iota(I32, (T, LANES), 1)
        ks_s[:, :LANES] = ka.astype(BF16)
        ks_s[:, LANES:] = (kb + (lane - SEL_LANE0 == blk).astype(F32)).astype(BF16)
        ka, kb = prep_k(kw_ref, gkw_ref)
        lane_w = lax.broadcasted_iota(I32, (WINDOW, LANES), 1)
        kw_s[:WINDOW, :LANES] = jnp.zeros((WINDOW, LANES), BF16)
        kw_s[:WINDOW, LANES:] = jnp.where(lane_w == PAD_LANE, NEG_INF, 0.0).astype(BF16)
        kw_s[WINDOW:, :LANES] = ka.astype(BF16)
        kw_s[WINDOW:, LANES:] = kb.astype(BF16)
        vst_s[:DV, :] = vs_ref[...].T.astype(BF16)
        vst_s[DV:, :] = jnp.ones((DV_EXT - DV, T), BF16)
        vwt_s[:DV, :WINDOW] = jnp.zeros((DV, WINDOW), BF16)
        vwt_s[:DV, WINDOW:] = vw_ref[...].T.astype(BF16)
        vwt_s[DV:, :] = jnp.ones((DV_EXT - DV, WINDOW + T), BF16)
        vct_s[...] = vc_ref[0, 0].astype(F32).T.astype(BF16)

    t0 = pl.multiple_of(qi * tq, tq)
    lane_q = lax.broadcasted_iota(I32, (tq, LANES), 1)
    for r in range(R):
        qa = q_ref[:, r * HEAD_PAD:r * HEAD_PAD + LANES]
        qb = q_ref[:, r * HEAD_PAD + LANES:(r + 1) * HEAD_PAD]
        rs = _rms_inv(qa, qb, NSA_DK) * scale
        qa = _rope(qa * rs * gq_ref[:, :LANES], cq_ref[...], sq_ref[...])
        qb = jnp.where(lane_q == PAD_LANE, 1.0, qb * rs * gq_ref[:, LANES:])
        q_s[r * tq:(r + 1) * tq, :LANES] = qa.astype(BF16)
        q_s[r * tq:(r + 1) * tq, LANES:] = qb.astype(BF16)

    c_q = lax.broadcasted_iota(I32, (1, nq), 1) & (tq - 1)
    t_cols = t0 + c_q
    q = q_s[...]

    g_s[...] = jax.nn.sigmoid(gate_ref[...]).T

    span = tq + WINDOW
    sw = lax.dot_general(kw_s[pl.ds(t0, span), :], q, _NT, preferred_element_type=F32)
    sc = lax.dot_general(kc_ref[0, 0], q, _NT, preferred_element_type=F32)

    cend = lax.broadcasted_iota(I32, (ncp, 1), 0) * CMP_STRIDE + (CMP_LEN - 1)
    mask_c = cend <= t_cols
    sc = jnp.where(mask_c, sc, NEG_INF)
    e = jnp.where(mask_c, jnp.exp(sc - _col_max(sc)), 0.0)
    p = e * (1.0 / jnp.maximum(_col_sum(e), 1e-30))
    o_c = jnp.dot(vct_s[...], p.astype(BF16), preferred_element_type=F32)

    ps = p[:, :tq]
    for r in range(1, R):
        ps = ps + p[:, r * tq:(r + 1) * tq]
    ps_hi = ps.astype(BF16)
    ps_lo = (ps - ps_hi.astype(F32)).astype(BF16)
    imp = (jnp.dot(ovt_ref[...], ps_hi, preferred_element_type=F32)
           + jnp.dot(ovt_ref[...], ps_lo, preferred_element_type=F32))
    jb = lax.broadcasted_iota(I32, (nb, tq), 0)
    tb = (t0 + lax.broadcasted_iota(I32, (1, tq), 1)) // SLC_LEN
    forced = (jb == 0) | (jb == tb) | (jb == tb - 1)
    score = jnp.where(forced, FORCE, jnp.where(jb <= tb, imp, -FORCE))
    sub = 8
    slabs = [score[v * sub:(v + 1) * sub] for v in range(nb // sub)]
    ranks = [jnp.zeros((sub, tq), I32) for _ in slabs]
    for j in range(nb):
        row = score[j:j + 1, :]
        for v, sl in enumerate(slabs):
            if v * sub > j:
                beats = row >= sl
            elif (v + 1) * sub - 1 < j:
                beats = row > sl
            else:
                beats = (row > sl) | ((row == sl) & (j < v * sub + lax.broadcasted_iota(I32, (sub, tq), 0)))
            ranks[v] = ranks[v] + beats.astype(I32)
    rank = jnp.concatenate(ranks, axis=0)
    unsel = (rank >= topk).astype(F32)
    unsel = jnp.concatenate([jnp.zeros((SEL_LANE0, tq), F32), unsel,
                             jnp.zeros((LANES - SEL_LANE0 - nb, tq), F32)], axis=0)
    bias = unsel.T * NEG_INF

    for r in range(R):
        qb = q_s[r * tq:(r + 1) * tq, LANES:].astype(F32)
        q_s[r * tq:(r + 1) * tq, LANES:] = (qb + bias).astype(BF16)

    i_k = lax.broadcasted_iota(I32, (tq, 1), 0)
    segs = [jnp.where(i_k > c_q, sw[:tq], NEG_INF)]
    if tq < WINDOW:
        segs.append(sw[tq:WINDOW])
    segs.append(jnp.where(i_k <= c_q, sw[WINDOW:], NEG_INF))
    m = functools.reduce(jnp.maximum, [_col_max(s) for s in segs])
    pw = jnp.concatenate([jnp.exp(s - m) for s in segs], axis=0).astype(BF16)
    acc = jnp.dot(vwt_s[:, pl.ds(t0, span)], pw, preferred_element_type=F32)
    o_w = acc[:DV] * (1.0 / acc[DV:DV + 1])

    k_at = lambda off: ks_s[pl.ds(off, tk), :]
    vt_at = lambda off: vst_s[:, pl.ds(off, tk)]
    chains = [(lambda r=r: q_s[r * tq:(r + 1) * tq], k_at, vt_at, t_cols[:, r * tq:(r + 1) * tq])
              for r in range(R)]
    o_s = jnp.concatenate(_flash_t(chains, t0 // tk, T // tk, tk, *flash_scratch), axis=1)

    gs = g_s[pl.ds(pl.multiple_of(pl.program_id(1) * GATE_PITCH, GATE_PITCH), GATE_PITCH), :]
    for r in range(R):
        cols = slice(r * tq, (r + 1) * tq)
        o = (gs[3 * r:3 * r + 1] * o_c[:, cols] + gs[3 * r + 1:3 * r + 2] * o_s[:, cols]
             + gs[3 * r + 2:3 * r + 3] * o_w[:, cols])
        o_ref[:, r * DV:(r + 1) * DV] = o.T.astype(BF16)


def _nsa_attention(proj, k_c, v_c, gq, gks, gkw, tabs, ovt, B, T, tq=512, tk=512):
    nq = T // tq
    G, R = NSA_GROUPS, NSA_HPG
    ncp = k_c.shape[2]
    c, s = tabs
    qtab = pl.BlockSpec((tq, LANES), lambda b, g, i: (i, 0))
    ktab = pl.BlockSpec((T, LANES), lambda b, g, i: (0, 0))
    gspec = pl.BlockSpec((1, HEAD_PAD), lambda b, g, i: (0, 0))
    return pl.pallas_call(
        functools.partial(_nsa_kernel, tq=tq, tk=tk, T=T, scale=NSA_DK ** -0.5),
        grid=(B, G, nq),
        in_specs=[pl.BlockSpec((tq, R * HEAD_PAD), lambda b, g, i: (b * nq + i, g)),
                  pl.BlockSpec((T, HEAD_PAD), lambda b, g, i: (b, C_KS // HEAD_PAD + g)),
                  pl.BlockSpec((T, HEAD_PAD), lambda b, g, i: (b, C_KW // HEAD_PAD + g)),
                  pl.BlockSpec((T, DV), lambda b, g, i: (b, C_VS // DV + g)),
                  pl.BlockSpec((T, DV), lambda b, g, i: (b, C_VW // DV + g)),
                  pl.BlockSpec((tq, LANES), lambda b, g, i: (b * nq + i, C_GATE // LANES)),
                  pl.BlockSpec((1, 1, ncp, HEAD_PAD), lambda b, g, i: (b, g, 0, 0)),
                  pl.BlockSpec((1, 1, ncp, DV), lambda b, g, i: (b, g, 0, 0)),
                  gspec, gspec, gspec, qtab, qtab, ktab, ktab,
                  pl.BlockSpec(ovt.shape, lambda b, g, i: (0, 0))],
        out_specs=pl.BlockSpec((tq, R * DV), lambda b, g, i: (b * nq + i, g)),
        out_shape=jax.ShapeDtypeStruct((B * T, NSA_HEADS * DV), BF16),
        scratch_shapes=[pltpu.VMEM((T, HEAD_PAD), BF16), pltpu.VMEM((WINDOW + T, HEAD_PAD), BF16),
                        pltpu.VMEM((DV_EXT, T), BF16), pltpu.VMEM((DV_EXT, WINDOW + T), BF16),
                        pltpu.VMEM((DV, ncp), BF16), pltpu.VMEM((R * tq, HEAD_PAD), BF16),
                        pltpu.VMEM((LANES, tq), F32)]
        + _flash_scratch(R, tk, tq),
        compiler_params=_params(("parallel", "parallel", "arbitrary")),
        name="nsa_attention",
    )(proj, proj, proj, proj, proj, proj, k_c, v_c, gq, gks, gkw, c, s, c, s, ovt)


def _pad_cols(w, n):
    return jnp.pad(w, ((0, 0), (0, n - w.shape[1])))


def _take_cols(w, plan):
    zeros = lambda n: jnp.zeros((w.shape[0], n), w.dtype)
    return jnp.concatenate([zeros(p) if isinstance(p, int) else w[:, p[0]:p[1]] for p in plan], axis=1)


def _nsa_head_plan(c0):
    h = NSA_ROPE // 2
    fill = HALF_LANES - h
    return [(c0, c0 + h), (c0 + 2 * h, c0 + 2 * h + fill), (c0 + h, c0 + 2 * h),
            (c0 + 2 * h + fill, c0 + NSA_DK), HEAD_PAD - NSA_DK]


def _mla_rope_plan(c0):
    h = MLA_ROPE // 2
    return [(c0, c0 + h), HALF_LANES - h, (c0 + h, c0 + 2 * h), HALF_LANES - h]


def _mla_head_plan(c0):
    return [(c0, c0 + MLA_NOPE)] + _mla_rope_plan(c0 + MLA_NOPE)


def _heads(plan_fn, c0, nh, d):
    return [piece for h in range(nh) for piece in plan_fn(c0 + h * d)]


def _nsa_head_layout(w, nh):
    return _take_cols(w, _heads(_nsa_head_plan, 0, nh, NSA_DK))


def _mla_head_layout(w, nh):
    return _take_cols(w, _heads(_mla_head_plan, 0, nh, MLA_QK))


def _layout_w_in(w_in):
    sizes = (MLA_RANK, MLA_RANK, MLA_ROPE, NSA_HEADS * NSA_DK,
             NSA_GROUPS * NSA_DK, NSA_GROUPS * NSA_DV, NSA_GROUPS * NSA_DK, NSA_GROUPS * NSA_DV,
             NSA_GROUPS * NSA_DK, NSA_GROUPS * NSA_DV, NSA_HEADS * 3, 2 * D_MODEL)
    offs, o = [], 0
    for s in sizes:
        offs.append(o)
        o += s
    cq, ckv, kr, qn, kc, vc, ks, vs, kw, vw, gn, gm = offs
    whole = lambda c0, n: [(c0, c0 + n)]
    ngate = NSA_HPG * 3
    plan = (_heads(_nsa_head_plan, qn, NSA_HEADS, NSA_DK) + whole(gm, 2 * D_MODEL)
            + _heads(_nsa_head_plan, ks, NSA_GROUPS, NSA_DK) + _heads(_nsa_head_plan, kw, NSA_GROUPS, NSA_DK)
            + whole(vs, NSA_GROUPS * NSA_DV) + whole(vw, NSA_GROUPS * NSA_DV)
            + whole(cq, MLA_RANK) + whole(ckv, MLA_RANK) + _mla_rope_plan(kr)
            + _heads(lambda c: [(c, c + ngate), GATE_PITCH - ngate], gn, NSA_GROUPS, ngate)
            + [LANES - NSA_GROUPS * GATE_PITCH]
            + _heads(lambda c: [(c, c + NSA_DK), HEAD_PAD - NSA_DK], kc, NSA_GROUPS, NSA_DK)
            + whole(vc, NSA_GROUPS * NSA_DV))
    width = sum(p if isinstance(p, int) else p[1] - p[0] for p in plan)
    assert width == C_VC + NSA_GROUPS * NSA_DV <= C_END
    return _permute_rows(w_in.T, plan + [C_END - width])


PERM_OUT_ROWS = 256
PERM_SOURCES = 3


def _permute_rows_kernel(tab_ref, *refs, n_src):
    src_refs, p_ref, o_ref = refs[:PERM_SOURCES], refs[PERM_SOURCES], refs[PERM_SOURCES + 1]
    j = pl.program_id(0)
    row = lax.broadcasted_iota(I32, (LANES, 1), 0)
    acc = None
    for k, ref in enumerate(src_refs):
        in_bounds = tab_ref[j, k] * LANES + row < n_src
        x = jnp.where(in_bounds, ref[...], 0.0).astype(BF16)
        y = jnp.dot(p_ref[0, k], x, preferred_element_type=F32)
        acc = y if acc is None else acc + y
    o_ref[...] = acc.astype(BF16)


def _permute_rows(w, plan):
    n_src, cols = w.shape
    src = np.concatenate([np.full(p, -1) if isinstance(p, int) else np.arange(p[0], p[1]) for p in plan])
    nt = len(src) // PERM_OUT_ROWS
    assert nt * PERM_OUT_ROWS == len(src)
    tab = np.zeros((nt, PERM_SOURCES), np.int32)
    place = np.zeros((nt, PERM_SOURCES, PERM_OUT_ROWS, LANES), np.float32)
    for t in range(nt):
        rows = src[t * PERM_OUT_ROWS:(t + 1) * PERM_OUT_ROWS]
        tiles = sorted(set(int(r) // LANES for r in rows if r >= 0))
        assert len(tiles) <= PERM_SOURCES
        tab[t, :len(tiles)] = tiles
        for out_row, r in enumerate(rows):
            if r >= 0:
                place[t, tiles.index(int(r) // LANES), out_row, r % LANES] = 1.0
    src_tile = lambda k: pl.BlockSpec((LANES, cols), lambda j, tab: (tab[j, k], 0))
    return pl.pallas_call(
        functools.partial(_permute_rows_kernel, n_src=n_src),
        grid_spec=pltpu.PrefetchScalarGridSpec(
            num_scalar_prefetch=1, grid=(nt,),
            in_specs=[src_tile(k) for k in range(PERM_SOURCES)]
            + [pl.BlockSpec((1, PERM_SOURCES, PERM_OUT_ROWS, LANES), lambda j, tab: (j, 0, 0, 0))],
            out_specs=pl.BlockSpec((PERM_OUT_ROWS, cols), lambda j, tab: (j, 0))),
        out_shape=jax.ShapeDtypeStruct((nt * PERM_OUT_ROWS, cols), BF16),
        compiler_params=_params(("parallel",)),
        name="weight_layout",
    )(jnp.asarray(tab), *([w] * PERM_SOURCES), jnp.asarray(place, BF16))


def _rope_tables(pos, rot_dim):
    half = rot_dim // 2
    inv = ROPE_THETA ** (-np.arange(0, rot_dim, 2, dtype=np.float64) / rot_dim)
    ang = np.asarray(pos, np.float64)[:, None] * inv[None, :]
    c = np.ones((len(pos), LANES))
    s = np.zeros((len(pos), LANES))
    c[:, :half] = c[:, HALF_LANES:HALF_LANES + half] = np.cos(ang)
    s[:, :half] = -np.sin(ang)
    s[:, HALF_LANES:HALF_LANES + half] = np.sin(ang)
    return jnp.asarray(c, F32), jnp.asarray(s, F32)


def _overlap_table(ncp, nb):
    cs = np.arange(ncp)[None, :] * CMP_STRIDE
    ss = np.arange(nb)[:, None] * SLC_LEN
    ov = np.clip(np.minimum(cs + CMP_LEN, ss + SLC_LEN) - np.maximum(cs, ss), 0, None) / CMP_LEN
    return jnp.asarray(ov, BF16)


def _layer(x, attn_norm_g, w_in, mla_q_lat_g, mla_kv_lat_g, mla_w_uq, mla_w_uk, mla_w_uv,
           mla_q_norm_g, mla_k_norm_g, nsa_q_norm_g, nsa_k_norm_g, cmp_pe_k, cmp_pe_v,
           cmp_k_w1, cmp_k_b1, cmp_k_w2, cmp_k_b2, cmp_v_w1, cmp_v_b1, cmp_v_w2, cmp_v_b2,
           w_proj_mla, w_proj_nsa, w_out, ffn_norm_g, w_ffn_gate, w_ffn_up, w_ffn_down):
    B, T, _ = x.shape
    n = B * T
    ncp = T // CMP_STRIDE
    nb = T // SLC_LEN
    assert T % 512 == 0 and ncp % LANES == 0 and nb % 8 == 0 and SEL_LANE0 + nb <= PAD_LANE and T >= WINDOW
    x2 = x.reshape(n, D_MODEL)
    pos = np.arange(T)

    proj = _norm_matmul(x2, 0, D_MODEL, attn_norm_g.reshape(1, -1), _layout_w_in(w_in), 1024, IN_PROJ_TN, "in_proj",
                        w_t=True)

    q_up = _norm_matmul(proj, C_CQ // MLA_RANK, MLA_RANK, mla_q_lat_g.reshape(1, -1),
                        _mla_head_layout(mla_w_uq.astype(BF16), MLA_HEADS), 1024, 4096, "mla_q_up",
                        out_dtype=BF16)
    kv_up = _norm_matmul(proj, C_CKV // MLA_RANK, MLA_RANK, mla_kv_lat_g.reshape(1, -1),
                         jnp.concatenate([mla_w_uk, mla_w_uv], axis=1).astype(BF16), 1024, 4096, "mla_kv_up",
                         out_dtype=BF16)
    o_mla = _mla_attention(q_up, kv_up, proj, _mla_head_layout(mla_q_norm_g[None], 1),
                           _mla_head_layout(mla_k_norm_g[None], 1), _rope_tables(pos, MLA_ROPE), B, T)

    cmp_tabs = _rope_tables(np.arange(ncp) * CMP_STRIDE + CMP_LEN - 1, NSA_ROPE)
    k_c = _compress(proj, C_KC, NSA_DK, B, T, cmp_pe_k, cmp_k_w1, cmp_k_b1, _nsa_head_layout(cmp_k_w2, 1),
                    _nsa_head_layout(cmp_k_b2[None], 1)[0], _nsa_head_layout(nsa_k_norm_g[0:1], 1), cmp_tabs,
                    True, "compress_k")
    v_c = _compress(proj, C_VC, NSA_DV, B, T, cmp_pe_v, cmp_v_w1, cmp_v_b1, cmp_v_w2, cmp_v_b2,
                    jnp.zeros((1, NSA_DV), F32), cmp_tabs, False, "compress_v")
    o_nsa = _nsa_attention(proj, k_c, v_c, _nsa_head_layout(nsa_q_norm_g[None], 1),
                           _nsa_head_layout(nsa_k_norm_g[1:2], 1), _nsa_head_layout(nsa_k_norm_g[2:3], 1),
                           _rope_tables(pos, NSA_ROPE), _overlap_table(ncp, nb), B, T)

    u = _merge(o_mla, o_nsa, w_proj_mla.astype(BF16), w_proj_nsa.astype(BF16), proj)
    h, hn = _out_proj(u, w_out.astype(BF16), x2, ffn_norm_g.reshape(1, -1))
    act = _ffn_up(hn, w_ffn_gate.astype(BF16), w_ffn_up.astype(BF16))
    out = _ffn_down(act, w_ffn_down.astype(BF16), h)
    return out.reshape(B, T, D_MODEL)


def kernel(x, attn_norm_g, w_in, mla_q_lat_g, mla_kv_lat_g, mla_w_uq, mla_w_uk, mla_w_uv, mla_q_norm_g, mla_k_norm_g, nsa_q_norm_g, nsa_k_norm_g, cmp_pe_k, cmp_pe_v, cmp_k_w1, cmp_k_b1, cmp_k_w2, cmp_k_b2, cmp_v_w1, cmp_v_b1, cmp_v_w2, cmp_v_b2, w_proj_mla, w_proj_nsa, w_out, ffn_norm_g, w_ffn_gate, w_ffn_up, w_ffn_down):
    params = (attn_norm_g, w_in, mla_q_lat_g, mla_kv_lat_g, mla_w_uq, mla_w_uk, mla_w_uv, mla_q_norm_g,
              mla_k_norm_g, nsa_q_norm_g, nsa_k_norm_g, cmp_pe_k, cmp_pe_v, cmp_k_w1, cmp_k_b1, cmp_k_w2,
              cmp_k_b2, cmp_v_w1, cmp_v_b1, cmp_v_w2, cmp_v_b2, w_proj_mla, w_proj_nsa, w_out, ffn_norm_g,
              w_ffn_gate, w_ffn_up, w_ffn_down)
    depth = attn_norm_g.shape[0]
    h = x
    for l in range(depth):
        h = _layer(h, *(p.reshape(p.shape[1:]) if depth == 1 else p[l] for p in params))
    return h
```
